```python
import jax, jax.numpy as jnp
from jax import lax
import numpy as np

D_MODEL = 1024
BATCH = 4
SEQ = 4096
DEPTH = 4
DEC_BATCH = 8
DEC_SEQ = 2048
PAST_LEN = 128

GRID_W = 64
EPS = 1e-6
GLA_HEADS = 4
GLA_KEY_DIM = D_MODEL // 2
GLA_VALUE_DIM = D_MODEL
GLA_DK = GLA_KEY_DIM // GLA_HEADS
GLA_DV = GLA_VALUE_DIM // GLA_HEADS
GATE_RANK = 16
GATE_NORMALIZER = 16.0
GLA_CHUNK = 64
ATT_HEAD_DIM = 64
ATT_Q_HEADS = D_MODEL // ATT_HEAD_DIM
ATT_KV_HEADS = 4
ATT_GROUP = ATT_Q_HEADS // ATT_KV_HEADS
ATT_Q_DIM = ATT_Q_HEADS * ATT_HEAD_DIM
ATT_KV_DIM = ATT_KV_HEADS * ATT_HEAD_DIM
ROPE_AXIS_DIM = ATT_HEAD_DIM // 2
ROPE_THETA = 10000.0
Q_BLOCK = 128
IN_SIZES = (GLA_KEY_DIM, GLA_KEY_DIM, GLA_VALUE_DIM, GATE_RANK, GATE_RANK, GLA_VALUE_DIM,
            ATT_Q_DIM, ATT_KV_DIM, ATT_KV_DIM, ATT_Q_DIM, D_MODEL, D_MODEL)
IN_DIM = sum(IN_SIZES)

kernel_name = 'hybrid_gla_axial_gqa_encoder'


def rmsnorm(x, g):
    xf = x.astype(jnp.float32)
    y = xf * lax.rsqrt(jnp.mean(xf * xf, axis=-1, keepdims=True) + EPS)
    return (y * g.astype(jnp.float32)).astype(x.dtype)


def split_points():
    pts, acc = [], 0
    for s in IN_SIZES[:-1]:
        acc += s
        pts.append(acc)
    return pts


def gla_scan(q, k, v, g, strict):
    B, H, T, dk = q.shape
    dv = v.shape[-1]
    n = T // GLA_CHUNK

    def to_chunks(a):
        return jnp.moveaxis(a.reshape(B, H, n, GLA_CHUNK, a.shape[-1]), 2, 0)

    idx = jnp.arange(GLA_CHUNK)
    mask = (idx[:, None] > idx[None, :]) if strict else (idx[:, None] >= idx[None, :])

    def step(S, inp):
        qi, ki, vi, gi = inp
        b = jnp.cumsum(gi, axis=2)
        diff = b[:, :, :, None, :] - b[:, :, None, :, :]
        decay = jnp.exp(jnp.where(mask[None, None, :, :, None], diff, -jnp.inf))
        A = jnp.einsum('bhic,bhjc,bhijc->bhij', qi, ki, decay)
        o = (jnp.einsum('bhij,bhjv->bhiv', A, vi)
             + jnp.einsum('bhic,bhcv->bhiv', qi * jnp.exp(b), S))
        b_last = b[:, :, -1:, :]
        S = (jnp.exp(b_last[:, :, 0, :])[..., None] * S
             + jnp.einsum('bhjc,bhjv->bhcv', ki * jnp.exp(b_last - b), vi))
        return S, o

    S0 = jnp.zeros((B, H, dk, dv), jnp.float32)
    _, o = lax.scan(step, S0, (to_chunks(q), to_chunks(k), to_chunks(v), to_chunks(g)))
    return jnp.moveaxis(o, 0, 2).reshape(B, H, T, dv)


def gla_branch(qa, ka, va, gfl, gbl, za, w_gate_f, b_gate_f, w_gate_b, b_gate_b,
               gla_norm_g, w_branch_a):
    B, T, _ = qa.shape

    def heads(a, d):
        return a.reshape(B, T, GLA_HEADS, d).transpose(0, 2, 1, 3).astype(jnp.float32)

    q = heads(qa, GLA_DK) * (GLA_DK ** -0.5)
    k = heads(ka, GLA_DK)
    v = heads(va, GLA_DV)
    gf = heads(jax.nn.log_sigmoid((gfl @ w_gate_f + b_gate_f).astype(jnp.float32)) / GATE_NORMALIZER, GLA_DK)
    gb = heads(jax.nn.log_sigmoid((gbl @ w_gate_b + b_gate_b).astype(jnp.float32)) / GATE_NORMALIZER, GLA_DK)
    flip = lambda a: jnp.flip(a, axis=2)
    o_f = gla_scan(q, k, v, gf, False)
    o_b = flip(gla_scan(flip(q), flip(k), flip(v), flip(gb), True))
    o = (o_f + o_b).transpose(0, 2, 1, 3)
    o = rmsnorm(o, gla_norm_g).reshape(B, T, GLA_VALUE_DIM).astype(za.dtype)
    return (o * jax.nn.silu(za)) @ w_branch_a


def axial_rope_angles(T):
    rows = T // GRID_W
    r = jnp.repeat(jnp.arange(rows, dtype=jnp.float32), GRID_W)
    c = jnp.tile(jnp.arange(GRID_W, dtype=jnp.float32), rows)
    nf = ROPE_AXIS_DIM // 2
    inv = ROPE_THETA ** (-jnp.arange(nf, dtype=jnp.float32) / nf)
    return r[:, None] * inv, c[:, None] * inv


def rotate(xs, ang):
    half = xs.shape[-1] // 2
    cos = jnp.cos(ang)[None, :, None, :]
    sin = jnp.sin(ang)[None, :, None, :]
    x1, x2 = xs[..., :half], xs[..., half:]
    return jnp.concatenate([x1 * cos - x2 * sin, x2 * cos + x1 * sin], axis=-1)


def apply_axial_rope(x, ang_r, ang_c):
    xf = x.astype(jnp.float32)
    out = jnp.concatenate([rotate(xf[..., :ROPE_AXIS_DIM], ang_r),
                           rotate(xf[..., ROPE_AXIS_DIM:], ang_c)], axis=-1)
    return out.astype(x.dtype)


def block_attention(q, k, v):
    B, T, _, dh = q.shape
    nb = T // Q_BLOCK
    qb = jnp.moveaxis(q.reshape(B, nb, Q_BLOCK, ATT_KV_HEADS, ATT_GROUP, dh), 1, 0)
    scale = ATT_HEAD_DIM ** -0.5

    def one(qblk):
        s = jnp.einsum('bqkgd,bskd->bkgqs', qblk, k).astype(jnp.float32) * scale
        p = jax.nn.softmax(s, axis=-1)
        return jnp.einsum('bkgqs,bskd->bqkgd', p.astype(v.dtype), v)

    o = lax.map(one, qb)
    return jnp.moveaxis(o, 0, 1).reshape(B, T, ATT_Q_DIM)


def attn_branch(qb, kb, vb, zb, q_norm_g, k_norm_g, w_branch_b):
    B, T, _ = qb.shape
    q = rmsnorm(qb.reshape(B, T, ATT_Q_HEADS, ATT_HEAD_DIM), q_norm_g)
    k = rmsnorm(kb.reshape(B, T, ATT_KV_HEADS, ATT_HEAD_DIM), k_norm_g)
    v = vb.reshape(B, T, ATT_KV_HEADS, ATT_HEAD_DIM)
    ang_r, ang_c = axial_rope_angles(T)
    q = apply_axial_rope(q, ang_r, ang_c)
    k = apply_axial_rope(k, ang_r, ang_c)
    o = block_attention(q, k, v)
    return (o * jax.nn.silu(zb)) @ w_branch_b


def layer(x, norm_g, w_in, w_gate_f, b_gate_f, w_gate_b, b_gate_b, gla_norm_g,
          q_norm_g, k_norm_g, w_branch_a, w_branch_b, w_out):
    h = rmsnorm(x, norm_g)
    proj = h @ w_in
    qa, ka, va, gfl, gbl, za, qb, kb, vb, zb, ma, mb = jnp.split(proj, split_points(), axis=-1)
    ya = gla_branch(qa, ka, va, gfl, gbl, za, w_gate_f, b_gate_f, w_gate_b, b_gate_b,
                    gla_norm_g, w_branch_a)
    yb = attn_branch(qb, kb, vb, zb, q_norm_g, k_norm_g, w_branch_b)
    merged = jax.nn.sigmoid(ma) * ya + jax.nn.sigmoid(mb) * yb
    return x + merged @ w_out


def trunk(x, norm_g, w_in, w_gate_f, b_gate_f, w_gate_b, b_gate_b, gla_norm_g,
          q_norm_g, k_norm_g, w_branch_a, w_branch_b, w_out, final_norm_g):
    for l in range(DEPTH):
        x = layer(x, norm_g[l], w_in[l], w_gate_f[l], b_gate_f[l], w_gate_b[l], b_gate_b[l],
                  gla_norm_g[l], q_norm_g[l], k_norm_g[l], w_branch_a[l], w_branch_b[l], w_out[l])
    return rmsnorm(x, final_norm_g)


def setup_inputs(seed: int = 0) -> dict:
    key = jax.random.key(seed)
    ks = jax.random.split(key, 16)
    f32 = jnp.float32
    nrm = lambda k, shape, s: jax.random.normal(k, shape, f32) * s
    return {
        'x_prompt': nrm(ks[0], (BATCH, SEQ, D_MODEL), 1.0),
        'x_sample': nrm(ks[1], (DEC_BATCH, DEC_SEQ, D_MODEL), 1.0),
        'norm_g': 1.0 + nrm(ks[2], (DEPTH, D_MODEL), 0.02),
        'w_in': nrm(ks[3], (DEPTH, D_MODEL, IN_DIM), D_MODEL ** -0.5),
        'w_gate_f': nrm(ks[4], (DEPTH, GATE_RANK, GLA_KEY_DIM), GATE_RANK ** -0.5),
        'b_gate_f': nrm(ks[5], (DEPTH, GLA_KEY_DIM), 0.1),
        'w_gate_b': nrm(ks[6], (DEPTH, GATE_RANK, GLA_KEY_DIM), GATE_RANK ** -0.5),
        'b_gate_b': nrm(ks[7], (DEPTH, GLA_KEY_DIM), 0.1),
        'gla_norm_g': 1.0 + nrm(ks[8], (DEPTH, GLA_DV), 0.02),
        'q_norm_g': 1.0 + nrm(ks[9], (DEPTH, ATT_HEAD_DIM), 0.02),
        'k_norm_g': 1.0 + nrm(ks[10], (DEPTH, ATT_HEAD_DIM), 0.02),
        'w_branch_a': nrm(ks[11], (DEPTH, GLA_VALUE_DIM, D_MODEL), GLA_VALUE_DIM ** -0.5),
        'w_branch_b': nrm(ks[12], (DEPTH, ATT_Q_DIM, D_MODEL), ATT_Q_DIM ** -0.5),
        'w_out': nrm(ks[13], (DEPTH, D_MODEL, D_MODEL), D_MODEL ** -0.5),
        'final_norm_g': 1.0 + nrm(ks[14], (D_MODEL,), 0.02),
    }


def reference(x_prompt, x_sample, norm_g, w_in, w_gate_f, b_gate_f, w_gate_b, b_gate_b,
              gla_norm_g, q_norm_g, k_norm_g, w_branch_a, w_branch_b, w_out, final_norm_g):
    y_prompt = trunk(x_prompt, norm_g, w_in, w_gate_f, b_gate_f, w_gate_b, b_gate_b, gla_norm_g,
                     q_norm_g, k_norm_g, w_branch_a, w_branch_b, w_out, final_norm_g)
    y_sample = trunk(x_sample, norm_g, w_in, w_gate_f, b_gate_f, w_gate_b, b_gate_b, gla_norm_g,
                     q_norm_g, k_norm_g, w_branch_a, w_branch_b, w_out, final_norm_g)
    return (y_prompt, y_sample)
```

```python
import functools

import numpy as np
import jax
import jax.numpy as jnp
from jax import lax
from jax.experimental import pallas as pl
from jax.experimental.pallas import tpu as pltpu

F32, BF16 = jnp.float32, jnp.bfloat16

D_MODEL = 1024
DEPTH = 4
EPS = 1e-6
GRID_W = 64
GLA_HEADS = 4
GLA_KEY_DIM = D_MODEL // 2
GLA_VALUE_DIM = D_MODEL
GLA_DK = GLA_KEY_DIM // GLA_HEADS
GLA_DV = GLA_VALUE_DIM // GLA_HEADS
GATE_RANK = 16
GATE_NORMALIZER = 16.0
ATT_HEAD_DIM = 64
ATT_Q_HEADS = D_MODEL // ATT_HEAD_DIM
ATT_KV_HEADS = 4
ATT_GROUP = ATT_Q_HEADS // ATT_KV_HEADS
ATT_Q_DIM = ATT_Q_HEADS * ATT_HEAD_DIM
ATT_KV_DIM = ATT_KV_HEADS * ATT_HEAD_DIM
ROPE_AXIS_DIM = ATT_HEAD_DIM // 2
ROPE_THETA = 10000.0
IN_SIZES = (GLA_KEY_DIM, GLA_KEY_DIM, GLA_VALUE_DIM, GATE_RANK, GATE_RANK, GLA_VALUE_DIM,
            ATT_Q_DIM, ATT_KV_DIM, ATT_KV_DIM, ATT_Q_DIM, D_MODEL, D_MODEL)

V7X_LANES = 128
V7X_SUBLANES = 8
V7X_MXU_DIM = 256
V7X_VMEM_BYTES = 64 * 1024 * 1024
V7X_VMEM_BUDGET = V7X_VMEM_BYTES - 8 * 1024 * 1024

INPROJ_ROWS = 256
OUTPROJ_ROWS = 512
GLA_CHUNK = 64
GLA_LEVELS = (1, 2, 4, 8, 16, 32)
GLA_FINAL_ROWS = 256
ATT_Q_ROWS = 128
ATT_KEY_CHUNK = 512

_NT = (((1,), (1,)), ((), ()))
_TN = (((0,), (0,)), ((), ()))


def _vmem_limit(block_bytes, scratch_bytes=0):
    want = 2 * block_bytes + scratch_bytes + 16 * 1024 * 1024
    return int(min(want, V7X_VMEM_BUDGET))


def _nbytes(shape, dtype):
    return int(np.prod(shape)) * jnp.dtype(dtype).itemsize


def _log_sigmoid(z):
    return -(jnp.maximum(-z, 0.0) + jnp.log1p(jnp.exp(-jnp.abs(z))))


def _silu(z):
    return z * jax.nn.sigmoid(z)


def _inproj_body(x_ref, ng_ref, wgla_ref, wgate_ref, wgf_ref, wgb_ref, bgf_ref, bgb_ref,
                 watt_ref, wm_ref, qng_ref, kng_ref, cos_ref, sin_ref, bd_ref,
                 qa_ref, ka_ref, va_ref, za_ref, gf_ref, gb_ref,
                 qb_ref, kbt_ref, vb_ref, zb_ref, ma_ref, mb_ref):
    x = x_ref[...]
    ms = jnp.mean(x * x, axis=-1, keepdims=True)
    h = (x * lax.rsqrt(ms + EPS) * ng_ref[...]).astype(BF16)

    def proj(w_ref, lo, hi):
        return jnp.dot(h, w_ref[:, lo:hi], preferred_element_type=F32)

    qa_ref[...] = (proj(wgla_ref, 0, 512) * (GLA_DK ** -0.5)).astype(BF16)
    ka_ref[...] = proj(wgla_ref, 512, 1024).astype(BF16)
    va_ref[...] = proj(wgla_ref, 1024, 2048).astype(BF16)
    za_ref[...] = proj(wgla_ref, 2048, 3072).astype(BF16)

    pg = proj(wgate_ref, 0, V7X_LANES).astype(BF16)
    zf = jnp.dot(pg, wgf_ref[...], preferred_element_type=F32) + bgf_ref[...]
    zb = jnp.dot(pg, wgb_ref[...], preferred_element_type=F32) + bgb_ref[...]
    gf_ref[...] = _log_sigmoid(zf) * (1.0 / GATE_NORMALIZER)
    gb_ref[...] = _log_sigmoid(zb) * (1.0 / GATE_NORMALIZER)

    cos = cos_ref[...]
    sin = sin_ref[...]
    bd = bd_ref[...]
    lane = lax.broadcasted_iota(jnp.int32, cos.shape, 1)
    first_half = (lane & (ROPE_AXIS_DIM // 2)) == 0

    def norm_rope(p, gain):
        sq = (p * p).astype(BF16)
        msq = jnp.dot(sq, bd, preferred_element_type=F32) * (1.0 / ATT_HEAD_DIM)
        r = lax.rsqrt(msq + EPS)
        xg = p * gain
        outs = []
        for s in range(2):
            xs = xg[:, V7X_LANES * s:V7X_LANES * (s + 1)]
            partner = jnp.where(first_half,
                                pltpu.roll(xs, V7X_LANES - ROPE_AXIS_DIM // 2, 1),
                                pltpu.roll(xs, ROPE_AXIS_DIM // 2, 1))
            outs.append(xs * cos + partner * sin)
        return jnp.concatenate(outs, axis=1) * r

    for c in range(ATT_Q_DIM // V7X_MXU_DIM):
        lo = V7X_MXU_DIM * c
        p = proj(watt_ref, lo, lo + V7X_MXU_DIM)
        y = norm_rope(p, qng_ref[:, lo:lo + V7X_MXU_DIM]) * (ATT_HEAD_DIM ** -0.5)
        qb_ref[:, lo:lo + V7X_MXU_DIM] = y.astype(BF16)
    kk = norm_rope(proj(watt_ref, 1024, 1280), kng_ref[...])
    kbt_ref[...] = jnp.transpose(kk).astype(BF16)
    vv = proj(watt_ref, 1280, 1536).astype(BF16)
    for g in range(ATT_KV_HEADS):
        vb_ref[g] = vv[:, ATT_HEAD_DIM * g:ATT_HEAD_DIM * (g + 1)]
    zb_ref[...] = proj(watt_ref, 1536, 2560).astype(BF16)

    ma_ref[...] = proj(wm_ref, 0, 1024).astype(BF16)
    mb_ref[...] = proj(wm_ref, 1024, 2048).astype(BF16)


def _inproj(x, lw, cos, sin, bd, seq_len):
    n = x.shape[0]
    tm = INPROJ_ROWS
    pos_blocks = seq_len // tm
    row = lambda i: (i, 0)
    const = lambda i: (0, 0)

    def wspec(shape):
        return pl.BlockSpec(shape, const, pipeline_mode=pl.Buffered(1))

    in_specs = [
        pl.BlockSpec((tm, D_MODEL), row),
        wspec((1, D_MODEL)),
        wspec((D_MODEL, 3072)),
        wspec((D_MODEL, V7X_LANES)),
        wspec((V7X_LANES, GLA_KEY_DIM)),
        wspec((V7X_LANES, GLA_KEY_DIM)),
        wspec((1, GLA_KEY_DIM)),
        wspec((1, GLA_KEY_DIM)),
        wspec((D_MODEL, 2560)),
        wspec((D_MODEL, 2048)),
        wspec((1, ATT_Q_DIM)),
        wspec((1, ATT_KV_DIM)),
        pl.BlockSpec((tm, V7X_LANES), lambda i: (i % pos_blocks, 0)),
        pl.BlockSpec((tm, V7X_LANES), lambda i: (i % pos_blocks, 0)),
        wspec((V7X_MXU_DIM, V7X_MXU_DIM)),
    ]
    out_shape = [
        jax.ShapeDtypeStruct((n, GLA_KEY_DIM), BF16),
        jax.ShapeDtypeStruct((n, GLA_KEY_DIM), BF16),
        jax.ShapeDtypeStruct((n, GLA_VALUE_DIM), BF16),
        jax.ShapeDtypeStruct((n, GLA_VALUE_DIM), BF16),
        jax.ShapeDtypeStruct((n, GLA_KEY_DIM), F32),
        jax.ShapeDtypeStruct((n, GLA_KEY_DIM), F32),
        jax.ShapeDtypeStruct((n, ATT_Q_DIM), BF16),
        jax.ShapeDtypeStruct((ATT_KV_DIM, n), BF16),
        jax.ShapeDtypeStruct((ATT_KV_HEADS, n, ATT_HEAD_DIM), BF16),
        jax.ShapeDtypeStruct((n, ATT_Q_DIM), BF16),
        jax.ShapeDtypeStruct((n, D_MODEL), BF16),
        jax.ShapeDtypeStruct((n, D_MODEL), BF16),
    ]
    out_specs = [
        pl.BlockSpec((tm, GLA_KEY_DIM), row),
        pl.BlockSpec((tm, GLA_KEY_DIM), row),
        pl.BlockSpec((tm, GLA_VALUE_DIM), row),
        pl.BlockSpec((tm, GLA_VALUE_DIM), row),
        pl.BlockSpec((tm, GLA_KEY_DIM), row),
        pl.BlockSpec((tm, GLA_KEY_DIM), row),
        pl.BlockSpec((tm, ATT_Q_DIM), row),
        pl.BlockSpec((ATT_KV_DIM, tm), lambda i: (0, i)),
        pl.BlockSpec((ATT_KV_HEADS, tm, ATT_HEAD_DIM), lambda i: (0, i, 0)),
        pl.BlockSpec((tm, ATT_Q_DIM), row),
        pl.BlockSpec((tm, D_MODEL), row),
        pl.BlockSpec((tm, D_MODEL), row),
    ]
    weight_bytes = _nbytes((D_MODEL, 3072 + V7X_LANES + 2560 + 2048), BF16)
    block_bytes = (_nbytes((tm, D_MODEL), F32) + _nbytes((tm, 7 * 1024 + 512), BF16)
                   + 2 * _nbytes((tm, GLA_KEY_DIM), F32) + 2 * _nbytes((tm, V7X_LANES), F32))
    return pl.pallas_call(
        _inproj_body,
        grid=(n // tm,),
        in_specs=in_specs,
        out_specs=out_specs,
        out_shape=out_shape,
        compiler_params=pltpu.CompilerParams(
            dimension_semantics=("arbitrary",),
            vmem_limit_bytes=_vmem_limit(block_bytes, weight_bytes)),
        name="inproj",
    )(x, lw["norm_g"], lw["w_gla"], lw["w_gate"], lw["w_gf"], lw["w_gb"], lw["b_gf"], lw["b_gb"],
      lw["w_att"], lw["w_m"], lw["q_gain"], lw["k_gain"], cos, sin, bd)


def _gla_boundary(beta, m, reverse):
    c, width = beta.shape
    two_m = 2 * m
    off = m if reverse else m - 1
    pieces = []
    if two_m >= V7X_SUBLANES:
        for v in range(c // V7X_SUBLANES):
            r = (V7X_SUBLANES * v // two_m) * two_m + off
            pieces.append(jnp.broadcast_to(beta[r:r + 1, :], (V7X_SUBLANES, width)))
    else:
        sub = lax.broadcasted_iota(jnp.int32, (V7X_SUBLANES, width), 0)
        per = V7X_SUBLANES // two_m
        for v in range(c // V7X_SUBLANES):
            rows = [jnp.broadcast_to(beta[V7X_SUBLANES * v + two_m * j + off:V7X_SUBLANES * v + two_m * j + off + 1, :],
                                     (V7X_SUBLANES, width)) for j in range(per)]
            acc = rows[-1]
            for j in range(per - 2, -1, -1):
                acc = jnp.where(sub < two_m * (j + 1), rows[j], acc)
            pieces.append(acc)
    return jnp.concatenate(pieces, axis=0)


def _gla_chunk(q, k, v, g, s_ref, mask_ref, mask_base, reverse):
    c = GLA_CHUNK
    qf = q.astype(F32)
    kf = k.astype(F32)
    row = lax.broadcasted_iota(jnp.int32, g.shape, 0)
    beta = g
    for sh in GLA_LEVELS:
        if reverse:
            beta = beta + jnp.where(row < c - sh, pltpu.roll(beta, c - sh, 0), 0.0)
        else:
            beta = beta + jnp.where(row >= sh, pltpu.roll(beta, sh, 0), 0.0)

    a = None
    nmask = 0
    if not reverse:
        a = mask_ref[mask_base] * lax.dot_general(q, k, _NT, preferred_element_type=F32)
        nmask = 1
    for m in GLA_LEVELS:
        if m == 1:
            parity = (row & 1) == (0 if reverse else 1)
            e = jnp.where(parity, g, 0.0)
        else:
            e = beta - _gla_boundary(beta, m, reverse)
        f = jnp.exp(-jnp.abs(e))
        am = lax.dot_general((qf * f).astype(BF16), (kf * f).astype(BF16), _NT, preferred_element_type=F32)
        am = mask_ref[mask_base + nmask] * am
        a = am if a is None else a + am
        nmask += 1

    o = jnp.dot(a.astype(BF16), v, preferred_element_type=F32)
    s = s_ref[...]
    o = o + jnp.dot((qf * jnp.exp(beta)).astype(BF16), s.astype(BF16), preferred_element_type=F32)
    tot = beta[0:1, :] if reverse else beta[c - 1:c, :]
    kd = (kf * jnp.exp(tot - beta)).astype(BF16)
    kv = lax.dot_general(kd, v, _TN, preferred_element_type=F32)
    decay_col = jnp.transpose(jnp.broadcast_to(jnp.exp(tot), (V7X_SUBLANES, GLA_DK)))[:, 0:1]
    s_ref[...] = decay_col * s + kv
    return o


def _gla_body(q_ref, k_ref, v_ref, gf_ref, gb_ref, za_ref, gn_ref, mask_ref, o_ref, acc_ref, sf_ref, sb_ref):
    t = q_ref.shape[0]
    c = GLA_CHUNK
    n = t // c
    acc_ref[...] = jnp.zeros_like(acc_ref)
    sf_ref[...] = jnp.zeros_like(sf_ref)
    sb_ref[...] = jnp.zeros_like(sb_ref)

    def step(i, carry):
        lo = pl.multiple_of(i * c, c)
        sl = pl.ds(lo, c)
        of = _gla_chunk(q_ref[sl, :], k_ref[sl, :], v_ref[sl, :], gf_ref[sl, :], sf_ref, mask_ref, 0, False)
        acc_ref[sl, :] += of
        hi = pl.multiple_of((n - 1 - i) * c, c)
        sh = pl.ds(hi, c)
        ob = _gla_chunk(q_ref[sh, :], k_ref[sh, :], v_ref[sh, :], gb_ref[sh, :], sb_ref, mask_ref,
                        len(GLA_LEVELS) + 1, True)
        acc_ref[sh, :] += ob
        return carry

    lax.fori_loop(0, n, step, 0)

    rows = GLA_FINAL_ROWS

    def fin(i, carry):
        sl = pl.ds(pl.multiple_of(i * rows, rows), rows)
        o = acc_ref[sl, :]
        ms = jnp.mean(o * o, axis=-1, keepdims=True)
        y = o * lax.rsqrt(ms + EPS) * gn_ref[...]
        o_ref[sl, :] = (y * _silu(za_ref[sl, :].astype(F32))).astype(BF16)
        return carry

    lax.fori_loop(0, t // rows, fin, 0)


def _gla_masks():
    c = GLA_CHUNK
    ri = np.arange(c)[:, None]
    ci = np.arange(c)[None, :]
    masks = [ri == ci]
    for m in GLA_LEVELS:
        same = (ri // (2 * m)) == (ci // (2 * m))
        masks.append(same & ((ri % (2 * m)) >= m) & ((ci % (2 * m)) < m))
    for m in GLA_LEVELS:
        same = (ri // (2 * m)) == (ci // (2 * m))
        masks.append(same & ((ri % (2 * m)) < m) & ((ci % (2 * m)) >= m))
    return jnp.asarray(np.stack(masks).astype(np.float32))


def _gla(qa, ka, va, gf, gb, za, gn, masks, batch, seq_len):
    n = qa.shape[0]
    t = seq_len
    kspec = pl.BlockSpec((t, GLA_DK), lambda b, h: (b, h))
    vspec = pl.BlockSpec((t, GLA_DV), lambda b, h: (b, h))
    nm = masks.shape[0]
    block_bytes = (2 * _nbytes((t, GLA_DK), BF16) + 3 * _nbytes((t, GLA_DV), BF16)
                   + 2 * _nbytes((t, GLA_DK), F32) + _nbytes(masks.shape, F32))
    scratch_bytes = _nbytes((t, GLA_DV), F32) + 2 * _nbytes((GLA_DK, GLA_DV), F32)
    return pl.pallas_call(
        _gla_body,
        grid=(batch, GLA_HEADS),
        in_specs=[kspec, kspec, vspec, kspec, kspec, vspec,
                  pl.BlockSpec((1, GLA_DV), lambda b, h: (0, 0)),
                  pl.BlockSpec((nm, GLA_CHUNK, GLA_CHUNK), lambda b, h: (0, 0, 0))],
        out_specs=vspec,
        out_shape=jax.ShapeDtypeStruct((n, GLA_VALUE_DIM), BF16),
        scratch_shapes=[pltpu.VMEM((t, GLA_DV), F32),
                        pltpu.VMEM((GLA_DK, GLA_DV), F32),
                        pltpu.VMEM((GLA_DK, GLA_DV), F32)],
        compiler_params=pltpu.CompilerParams(
            dimension_semantics=("arbitrary", "arbitrary"),
            vmem_limit_bytes=_vmem_limit(block_bytes, scratch_bytes)),
        name="gla",
    )(qa, ka, va, gf, gb, za, gn, masks)


def _attn_body(q_ref, kt_ref, v_ref, z_ref, o_ref, s_ref):
    tq = q_ref.shape[0]
    t = kt_ref.shape[1]
    kc = ATT_KEY_CHUNK
    nk = t // kc
    slabs = kc // V7X_LANES
    outs = []
    for j in range(ATT_GROUP):
        qj = q_ref[:, ATT_HEAD_DIM * j:ATT_HEAD_DIM * (j + 1)]

        def scores(c, m):
            sl = pl.ds(pl.multiple_of(c * kc, kc), kc)
            s = jnp.dot(qj, kt_ref[:, sl], preferred_element_type=F32)
            s_ref[:, sl] = s
            for u in range(slabs):
                m = jnp.maximum(m, s[:, V7X_LANES * u:V7X_LANES * (u + 1)])
            return m

        m = lax.fori_loop(0, nk, scores, jnp.full((tq, V7X_LANES), -jnp.inf, F32))
        m = jnp.max(m, axis=-1, keepdims=True)

        def weighted(c, carry):
            l, acc = carry
            sl = pl.ds(pl.multiple_of(c * kc, kc), kc)
            p = jnp.exp(s_ref[:, sl] - m)
            for u in range(slabs):
                l = l + p[:, V7X_LANES * u:V7X_LANES * (u + 1)]
            acc = acc + jnp.dot(p.astype(BF16), v_ref[0, sl, :], preferred_element_type=F32)
            return l, acc

        l, acc = lax.fori_loop(0, nk, weighted,
                               (jnp.zeros((tq, V7X_LANES), F32), jnp.zeros((tq, ATT_HEAD_DIM), F32)))
        l = jnp.sum(l, axis=-1, keepdims=True)
        outs.append(acc / l)
    o = jnp.concatenate(outs, axis=1)
    o_ref[...] = (o * _silu(z_ref[...].astype(F32))).astype(BF16)


def _attn(qb, kbt, vb, zb, batch, seq_len):
    n = qb.shape[0]
    t = seq_len
    tq = ATT_Q_ROWS
    nq = t // tq
    width = ATT_GROUP * ATT_HEAD_DIM
    qspec = pl.BlockSpec((tq, width), lambda b, g, i: (b * nq + i, g))
    block_bytes = (3 * _nbytes((tq, width), BF16) + _nbytes((ATT_HEAD_DIM, t), BF16)
                   + _nbytes((t, V7X_LANES), BF16))
    scratch_bytes = _nbytes((tq, t), F32)
    return pl.pallas_call(
        _attn_body,
        grid=(batch, ATT_KV_HEADS, nq),
        in_specs=[qspec,
                  pl.BlockSpec((ATT_HEAD_DIM, t), lambda b, g, i: (g, b)),
                  pl.BlockSpec((1, t, ATT_HEAD_DIM), lambda b, g, i: (g, b, 0)),
                  qspec],
        out_specs=qspec,
        out_shape=jax.ShapeDtypeStruct((n, ATT_Q_DIM), BF16),
        scratch_shapes=[pltpu.VMEM((tq, t), F32)],
        compiler_params=pltpu.CompilerParams(
            dimension_semantics=("arbitrary", "arbitrary", "arbitrary"),
            vmem_limit_bytes=_vmem_limit(block_bytes, scratch_bytes)),
        name="attn",
    )(qb, kbt, vb, zb)


def _outproj_body(oa_ref, ob_ref, ma_ref, mb_ref, x_ref, wa_ref, wb_ref, wo_ref, fg_ref, o_ref, *, final):
    ya = jnp.dot(oa_ref[...], wa_ref[...], preferred_element_type=F32)
    yb = jnp.dot(ob_ref[...], wb_ref[...], preferred_element_type=F32)
    merged = (jax.nn.sigmoid(ma_ref[...].astype(F32)) * ya
              + jax.nn.sigmoid(mb_ref[...].astype(F32)) * yb)
    y = x_ref[...] + jnp.dot(merged.astype(BF16), wo_ref[...], preferred_element_type=F32)
    if final:
        ms = jnp.mean(y * y, axis=-1, keepdims=True)
        y = y * lax.rsqrt(ms + EPS) * fg_ref[...]
    o_ref[...] = y


def _outproj(oa, ob, ma, mb, x, lw, final_gain, final):
    n = x.shape[0]
    tm = OUTPROJ_ROWS
    row = lambda i: (i, 0)
    const = lambda i: (0, 0)
    act = pl.BlockSpec((tm, D_MODEL), row)
    wspec = pl.BlockSpec((D_MODEL, D_MODEL), const, pipeline_mode=pl.Buffered(1))
    block_bytes = 4 * _nbytes((tm, D_MODEL), BF16) + 2 * _nbytes((tm, D_MODEL), F32)
    weight_bytes = 3 * _nbytes((D_MODEL, D_MODEL), BF16)
    return pl.pallas_call(
        functools.partial(_outproj_body, final=final),
        grid=(n // tm,),
        in_specs=[act, act, act, act, act, wspec, wspec, wspec,
                  pl.BlockSpec((1, D_MODEL), const, pipeline_mode=pl.Buffered(1))],
        out_specs=act,
        out_shape=jax.ShapeDtypeStruct((n, D_MODEL), F32),
        compiler_params=pltpu.CompilerParams(
            dimension_semantics=("arbitrary",),
            vmem_limit_bytes=_vmem_limit(block_bytes, weight_bytes)),
        name="outproj",
    )(oa, ob, ma, mb, x, lw["w_a"], lw["w_b"], lw["w_o"], final_gain)


def _rope_tables(seq_len):
    rows = seq_len // GRID_W
    r = jnp.repeat(jnp.arange(rows, dtype=F32), GRID_W)
    c = jnp.tile(jnp.arange(GRID_W, dtype=F32), rows)
    nf = ROPE_AXIS_DIM // 2
    inv = ROPE_THETA ** (-jnp.arange(nf, dtype=F32) / nf)
    ang_r = r[:, None] * inv
    ang_c = c[:, None] * inv
    cos = jnp.concatenate([jnp.cos(ang_r), jnp.cos(ang_r), jnp.cos(ang_c), jnp.cos(ang_c)], axis=-1)
    sin = jnp.concatenate([-jnp.sin(ang_r), jnp.sin(ang_r), -jnp.sin(ang_c), jnp.sin(ang_c)], axis=-1)
    reps = V7X_LANES // ATT_HEAD_DIM
    return jnp.tile(cos, (1, reps)), jnp.tile(sin, (1, reps))


def _layer_weights(l, norm_g, w_in, w_gate_f, b_gate_f, w_gate_b, b_gate_b, gla_norm_g, q_norm_g, k_norm_g,
                   w_branch_a, w_branch_b, w_out):
    pts = np.cumsum((0,) + IN_SIZES)
    col = lambda i: w_in[l][:, pts[i]:pts[i + 1]]
    w = w_in[l]
    w_gate = jnp.zeros((D_MODEL, V7X_LANES), F32).at[:, :2 * GATE_RANK].set(w[:, pts[3]:pts[5]])
    w_gf = jnp.zeros((V7X_LANES, GLA_KEY_DIM), F32).at[:GATE_RANK].set(w_gate_f[l])
    w_gb = jnp.zeros((V7X_LANES, GLA_KEY_DIM), F32).at[GATE_RANK:2 * GATE_RANK].set(w_gate_b[l])
    return {
        "norm_g": norm_g[l].reshape(1, D_MODEL),
        "w_gla": jnp.concatenate([col(0), col(1), col(2), col(5)], axis=1).astype(BF16),
        "w_gate": w_gate.astype(BF16),
        "w_gf": w_gf.astype(BF16),
        "w_gb": w_gb.astype(BF16),
        "b_gf": b_gate_f[l].reshape(1, GLA_KEY_DIM),
        "b_gb": b_gate_b[l].reshape(1, GLA_KEY_DIM),
        "w_att": jnp.concatenate([col(6), col(7), col(8), col(9)], axis=1).astype(BF16),
        "w_m": jnp.concatenate([col(10), col(11)], axis=1).astype(BF16),
        "q_gain": jnp.tile(q_norm_g[l], ATT_Q_HEADS).reshape(1, ATT_Q_DIM),
        "k_gain": jnp.tile(k_norm_g[l], ATT_KV_HEADS).reshape(1, ATT_KV_DIM),
        "gla_gain": gla_norm_g[l].reshape(1, GLA_DV),
        "w_a": w_branch_a[l].astype(BF16),
        "w_b": w_branch_b[l].astype(BF16),
        "w_o": w_out[l].astype(BF16),
    }


def _trunk(x3, layers, final_gain, masks, bd):
    batch, seq_len, _ = x3.shape
    x = x3.reshape(batch * seq_len, D_MODEL)
    cos, sin = _rope_tables(seq_len)
    for l, lw in enumerate(layers):
        qa, ka, va, za, gf, gb, qb, kbt, vb, zb, ma, mb = _inproj(x, lw, cos, sin, bd, seq_len)
        oa = _gla(qa, ka, va, gf, gb, za, lw["gla_gain"], masks, batch, seq_len)
        ob = _attn(qb, kbt, vb, zb, batch, seq_len)
        x = _outproj(oa, ob, ma, mb, x, lw, final_gain, final=(l == len(layers) - 1))
    return x.reshape(batch, seq_len, D_MODEL)


def kernel(x_prompt, x_sample, norm_g, w_in, w_gate_f, b_gate_f, w_gate_b, b_gate_b, gla_norm_g, q_norm_g,
           k_norm_g, w_branch_a, w_branch_b, w_out, final_norm_g):
    layers = [_layer_weights(l, norm_g, w_in, w_gate_f, b_gate_f, w_gate_b, b_gate_b, gla_norm_g, q_norm_g,
                             k_norm_g, w_branch_a, w_branch_b, w_out) for l in range(DEPTH)]
    final_gain = final_norm_g.reshape(1, D_MODEL)
    masks = _gla_masks()
    head = np.arange(V7X_MXU_DIM) // ATT_HEAD_DIM
    bd = jnp.asarray((head[:, None] == head[None, :]).astype(np.float32), dtype=BF16)
    y_prompt = _trunk(x_prompt, layers, final_gain, masks, bd)
    y_sample = _trunk(x_sample, layers, final_gain, masks, bd)
    return (y_prompt, y_sample)
```

```python
import functools

import numpy as np
import jax
import jax.numpy as jnp
from jax import lax
from jax.experimental import pallas as pl
from jax.experimental.pallas import tpu as pltpu

F32, BF16 = jnp.float32, jnp.bfloat16

D_MODEL = 1024
DEPTH = 4
EPS = 1e-6
GRID_W = 64
GLA_HEADS = 4
GLA_KEY_DIM = D_MODEL // 2
GLA_VALUE_DIM = D_MODEL
GLA_DK = GLA_KEY_DIM // GLA_HEADS
GLA_DV = GLA_VALUE_DIM // GLA_HEADS
GATE_RANK = 16
GATE_NORMALIZER = 16.0
ATT_HEAD_DIM = 64
ATT_Q_HEADS = D_MODEL // ATT_HEAD_DIM
ATT_KV_HEADS = 4
ATT_GROUP = ATT_Q_HEADS // ATT_KV_HEADS
ATT_Q_DIM = ATT_Q_HEADS * ATT_HEAD_DIM
ATT_KV_DIM = ATT_KV_HEADS * ATT_HEAD_DIM
ROPE_AXIS_DIM = ATT_HEAD_DIM // 2
ROPE_THETA = 10000.0
IN_SIZES = (GLA_KEY_DIM, GLA_KEY_DIM, GLA_VALUE_DIM, GATE_RANK, GATE_RANK, GLA_VALUE_DIM,
            ATT_Q_DIM, ATT_KV_DIM, ATT_KV_DIM, ATT_Q_DIM, D_MODEL, D_MODEL)

V7X_LANES = 128
V7X_SUBLANES = 8
V7X_MXU_DIM = 256
V7X_VMEM_BYTES = 64 * 1024 * 1024
V7X_VMEM_BUDGET = V7X_VMEM_BYTES - 8 * 1024 * 1024

INPROJ_ROWS = 256
OUTPROJ_ROWS = 512
GLA_CHUNK = 64
GLA_LEVELS = (1, 2, 4, 8, 16, 32)
GLA_FINAL_ROWS = 256
ATT_Q_ROWS = 128
ATT_KEY_CHUNK = 512

_NT = (((1,), (1,)), ((), ()))
_TN = (((0,), (0,)), ((), ()))


def _vmem_limit(block_bytes, scratch_bytes=0):
    want = 2 * block_bytes + scratch_bytes + 16 * 1024 * 1024
    return int(min(want, V7X_VMEM_BUDGET))


def _nbytes(shape, dtype):
    return int(np.prod(shape)) * jnp.dtype(dtype).itemsize


def _log_sigmoid(z):
    return -(jnp.maximum(-z, 0.0) + jnp.log1p(jnp.exp(-jnp.abs(z))))


def _silu(z):
    return z * jax.nn.sigmoid(z)


def _inproj_body(x_ref, ng_ref, wgla_ref, wgate_ref, wgf_ref, wgb_ref, bgf_ref, bgb_ref,
                 watt_ref, wm_ref, qng_ref, kng_ref, cos_ref, sin_ref, bd_ref,
                 qa_ref, ka_ref, va_ref, za_ref, gf_ref, gb_ref,
                 qbt_ref, kb_ref, vbt_ref, zb_ref, ma_ref, mb_ref):
    x = x_ref[...]
    ms = jnp.mean(x * x, axis=-1, keepdims=True)
    h = (x * lax.rsqrt(ms + EPS) * ng_ref[...]).astype(BF16)

    def proj(w_ref, lo, hi):
        return jnp.dot(h, w_ref[:, lo:hi], preferred_element_type=F32)

    qa_ref[...] = (proj(wgla_ref, 0, 512) * (GLA_DK ** -0.5)).astype(BF16)
    ka_ref[...] = proj(wgla_ref, 512, 1024).astype(BF16)
    va_ref[...] = proj(wgla_ref, 1024, 2048).astype(BF16)
    za_ref[...] = proj(wgla_ref, 2048, 3072).astype(BF16)

    pg = proj(wgate_ref, 0, V7X_LANES).astype(BF16)
    zf = jnp.dot(pg, wgf_ref[...], preferred_element_type=F32) + bgf_ref[...]
    zb = jnp.dot(pg, wgb_ref[...], preferred_element_type=F32) + bgb_ref[...]
    gf_ref[...] = _log_sigmoid(zf) * (1.0 / GATE_NORMALIZER)
    gb_ref[...] = _log_sigmoid(zb) * (1.0 / GATE_NORMALIZER)

    cos = cos_ref[...]
    sin = sin_ref[...]
    bd = bd_ref[...]
    lane = lax.broadcasted_iota(jnp.int32, cos.shape, 1)
    first_half = (lane & (ROPE_AXIS_DIM // 2)) == 0

    def norm_rope(p, gain):
        sq = (p * p).astype(BF16)
        msq = jnp.dot(sq, bd, preferred_element_type=F32) * (1.0 / ATT_HEAD_DIM)
        r = lax.rsqrt(msq + EPS)
        xg = p * gain
        outs = []
        for s in range(2):
            xs = xg[:, V7X_LANES * s:V7X_LANES * (s + 1)]
            partner = jnp.where(first_half,
                                pltpu.roll(xs, V7X_LANES - ROPE_AXIS_DIM // 2, 1),
                                pltpu.roll(xs, ROPE_AXIS_DIM // 2, 1))
            outs.append(xs * cos + partner * sin)
        return jnp.concatenate(outs, axis=1) * r

    for c in range(ATT_Q_DIM // V7X_MXU_DIM):
        lo = V7X_MXU_DIM * c
        p = proj(watt_ref, lo, lo + V7X_MXU_DIM)
        y = norm_rope(p, qng_ref[:, lo:lo + V7X_MXU_DIM]) * (ATT_HEAD_DIM ** -0.5)
        qbt_ref[lo:lo + V7X_MXU_DIM, :] = jnp.transpose(y).astype(BF16)
    kk = norm_rope(proj(watt_ref, 1024, 1280), kng_ref[...]).astype(BF16)
    vt = jnp.transpose(proj(watt_ref, 1280, 1536)).astype(BF16)
    ones = jnp.ones((ATT_HEAD_DIM, vt.shape[1]), BF16)
    for g in range(ATT_KV_HEADS):
        kb_ref[g] = kk[:, ATT_HEAD_DIM * g:ATT_HEAD_DIM * (g + 1)]
        vbt_ref[2 * ATT_HEAD_DIM * g:2 * ATT_HEAD_DIM * g + ATT_HEAD_DIM, :] = vt[ATT_HEAD_DIM * g:ATT_HEAD_DIM * (g + 1), :]
        vbt_ref[2 * ATT_HEAD_DIM * g + ATT_HEAD_DIM:2 * ATT_HEAD_DIM * (g + 1), :] = ones
    zb_ref[...] = proj(watt_ref, 1536, 2560).astype(BF16)

    ma_ref[...] = proj(wm_ref, 0, 1024).astype(BF16)
    mb_ref[...] = proj(wm_ref, 1024, 2048).astype(BF16)


def _inproj(x, lw, cos, sin, bd, seq_len):
    n = x.shape[0]
    tm = INPROJ_ROWS
    pos_blocks = seq_len // tm
    row = lambda i: (i, 0)
    const = lambda i: (0, 0)

    def wspec(shape):
        return pl.BlockSpec(shape, const, pipeline_mode=pl.Buffered(1))

    in_specs = [
        pl.BlockSpec((tm, D_MODEL), row),
        wspec((1, D_MODEL)),
        wspec((D_MODEL, 3072)),
        wspec((D_MODEL, V7X_LANES)),
        wspec((V7X_LANES, GLA_KEY_DIM)),
        wspec((V7X_LANES, GLA_KEY_DIM)),
        wspec((1, GLA_KEY_DIM)),
        wspec((1, GLA_KEY_DIM)),
        wspec((D_MODEL, 2560)),
        wspec((D_MODEL, 2048)),
        wspec((1, ATT_Q_DIM)),
        wspec((1, ATT_KV_DIM)),
        pl.BlockSpec((tm, V7X_LANES), lambda i: (i % pos_blocks, 0)),
        pl.BlockSpec((tm, V7X_LANES), lambda i: (i % pos_blocks, 0)),
        wspec((V7X_MXU_DIM, V7X_MXU_DIM)),
    ]
    out_shape = [
        jax.ShapeDtypeStruct((n, GLA_KEY_DIM), BF16),
        jax.ShapeDtypeStruct((n, GLA_KEY_DIM), BF16),
        jax.ShapeDtypeStruct((n, GLA_VALUE_DIM), BF16),
        jax.ShapeDtypeStruct((n, GLA_VALUE_DIM), BF16),
        jax.ShapeDtypeStruct((n, GLA_KEY_DIM), F32),
        jax.ShapeDtypeStruct((n, GLA_KEY_DIM), F32),
        jax.ShapeDtypeStruct((ATT_Q_DIM, n), BF16),
        jax.ShapeDtypeStruct((ATT_KV_HEADS, n, ATT_HEAD_DIM), BF16),
        jax.ShapeDtypeStruct((2 * ATT_KV_DIM, n), BF16),
        jax.ShapeDtypeStruct((n, ATT_Q_DIM), BF16),
        jax.ShapeDtypeStruct((n, D_MODEL), BF16),
        jax.ShapeDtypeStruct((n, D_MODEL), BF16),
    ]
    out_specs = [
        pl.BlockSpec((tm, GLA_KEY_DIM), row),
        pl.BlockSpec((tm, GLA_KEY_DIM), row),
        pl.BlockSpec((tm, GLA_VALUE_DIM), row),
        pl.BlockSpec((tm, GLA_VALUE_DIM), row),
        pl.BlockSpec((tm, GLA_KEY_DIM), row),
        pl.BlockSpec((tm, GLA_KEY_DIM), row),
        pl.BlockSpec((ATT_Q_DIM, tm), lambda i: (0, i)),
        pl.BlockSpec((ATT_KV_HEADS, tm, ATT_HEAD_DIM), lambda i: (0, i, 0)),
        pl.BlockSpec((2 * ATT_KV_DIM, tm), lambda i: (0, i)),
        pl.BlockSpec((tm, ATT_Q_DIM), row),
        pl.BlockSpec((tm, D_MODEL), row),
        pl.BlockSpec((tm, D_MODEL), row),
    ]
    weight_bytes = _nbytes((D_MODEL, 3072 + V7X_LANES + 2560 + 2048), BF16)
    block_bytes = (_nbytes((tm, D_MODEL), F32) + _nbytes((tm, 7 * 1024 + 512), BF16)
                   + 2 * _nbytes((tm, GLA_KEY_DIM), F32) + 2 * _nbytes((tm, V7X_LANES), F32))
    return pl.pallas_call(
        _inproj_body,
        grid=(n // tm,),
        in_specs=in_specs,
        out_specs=out_specs,
        out_shape=out_shape,
        compiler_params=pltpu.CompilerParams(
            dimension_semantics=("arbitrary",),
            vmem_limit_bytes=_vmem_limit(block_bytes, weight_bytes)),
        name="inproj",
    )(x, lw["norm_g"], lw["w_gla"], lw["w_gate"], lw["w_gf"], lw["w_gb"], lw["b_gf"], lw["b_gb"],
      lw["w_att"], lw["w_m"], lw["q_gain"], lw["k_gain"], cos, sin, bd)


def _gla_boundary(beta, m, reverse):
    c, width = beta.shape
    two_m = 2 * m
    off = m if reverse else m - 1
    pieces = []
    if two_m >= V7X_SUBLANES:
        for v in range(c // V7X_SUBLANES):
            r = (V7X_SUBLANES * v // two_m) * two_m + off
            pieces.append(jnp.broadcast_to(beta[r:r + 1, :], (V7X_SUBLANES, width)))
    else:
        sub = lax.broadcasted_iota(jnp.int32, (V7X_SUBLANES, width), 0)
        per = V7X_SUBLANES // two_m
        for v in range(c // V7X_SUBLANES):
            rows = [jnp.broadcast_to(beta[V7X_SUBLANES * v + two_m * j + off:V7X_SUBLANES * v + two_m * j + off + 1, :],
                                     (V7X_SUBLANES, width)) for j in range(per)]
            acc = rows[-1]
            for j in range(per - 2, -1, -1):
                acc = jnp.where(sub < two_m * (j + 1), rows[j], acc)
            pieces.append(acc)
    return jnp.concatenate(pieces, axis=0)


def _gla_chunk(q, k, v, g, s_ref, mask_ref, mask_base, reverse):
    c = GLA_CHUNK
    qf = q.astype(F32)
    kf = k.astype(F32)
    row = lax.broadcasted_iota(jnp.int32, g.shape, 0)
    beta = g
    for sh in GLA_LEVELS:
        if reverse:
            beta = beta + jnp.where(row < c - sh, pltpu.roll(beta, c - sh, 0), 0.0)
        else:
            beta = beta + jnp.where(row >= sh, pltpu.roll(beta, sh, 0), 0.0)

    a = None
    nmask = 0
    if not reverse:
        a = mask_ref[mask_base] * lax.dot_general(q, k, _NT, preferred_element_type=F32)
        nmask = 1
    for m in GLA_LEVELS:
        if m == 1:
            parity = (row & 1) == (0 if reverse else 1)
            e = jnp.where(parity, g, 0.0)
        else:
            e = beta - _gla_boundary(beta, m, reverse)
        f = jnp.exp(-jnp.abs(e))
        am = lax.dot_general((qf * f).astype(BF16), (kf * f).astype(BF16), _NT, preferred_element_type=F32)
        am = mask_ref[mask_base + nmask] * am
        a = am if a is None else a + am
        nmask += 1

    o = jnp.dot(a.astype(BF16), v, preferred_element_type=F32)
    s = s_ref[...]
    o = o + jnp.dot((qf * jnp.exp(beta)).astype(BF16), s.astype(BF16), preferred_element_type=F32)
    tot = beta[0:1, :] if reverse else beta[c - 1:c, :]
    kd = (kf * jnp.exp(tot - beta)).astype(BF16)
    kv = lax.dot_general(kd, v, _TN, preferred_element_type=F32)
    decay_col = jnp.transpose(jnp.broadcast_to(jnp.exp(tot), (V7X_SUBLANES, GLA_DK)))[:, 0:1]
    s_ref[...] = decay_col * s + kv
    return o


def _gla_body(q_ref, k_ref, v_ref, gf_ref, gb_ref, za_ref, gn_ref, mask_ref, o_ref, acc_ref, sf_ref, sb_ref):
    t = q_ref.shape[0]
    c = GLA_CHUNK
    n = t // c
    acc_ref[...] = jnp.zeros_like(acc_ref)
    sf_ref[...] = jnp.zeros_like(sf_ref)
    sb_ref[...] = jnp.zeros_like(sb_ref)

    def step(i, carry):
        lo = pl.multiple_of(i * c, c)
        sl = pl.ds(lo, c)
        of = _gla_chunk(q_ref[sl, :], k_ref[sl, :], v_ref[sl, :], gf_ref[sl, :], sf_ref, mask_ref, 0, False)
        acc_ref[sl, :] += of
        hi = pl.multiple_of((n - 1 - i) * c, c)
        sh = pl.ds(hi, c)
        ob = _gla_chunk(q_ref[sh, :], k_ref[sh, :], v_ref[sh, :], gb_ref[sh, :], sb_ref, mask_ref,
                        len(GLA_LEVELS) + 1, True)
        acc_ref[sh, :] += ob
        return carry

    lax.fori_loop(0, n, step, 0)

    rows = GLA_FINAL_ROWS

    def fin(i, carry):
        sl = pl.ds(pl.multiple_of(i * rows, rows), rows)
        o = acc_ref[sl, :]
        ms = jnp.mean(o * o, axis=-1, keepdims=True)
        y = o * lax.rsqrt(ms + EPS) * gn_ref[...]
        o_ref[sl, :] = (y * _silu(za_ref[sl, :].astype(F32))).astype(BF16)
        return carry

    lax.fori_loop(0, t // rows, fin, 0)


def _gla_masks():
    c = GLA_CHUNK
    ri = np.arange(c)[:, None]
    ci = np.arange(c)[None, :]
    masks = [ri == ci]
    for m in GLA_LEVELS:
        same = (ri // (2 * m)) == (ci // (2 * m))
        masks.append(same & ((ri % (2 * m)) >= m) & ((ci % (2 * m)) < m))
    for m in GLA_LEVELS:
        same = (ri // (2 * m)) == (ci // (2 * m))
        masks.append(same & ((ri % (2 * m)) < m) & ((ci % (2 * m)) >= m))
    return jnp.asarray(np.stack(masks).astype(np.float32))


def _gla(qa, ka, va, gf, gb, za, gn, masks, batch, seq_len):
    n = qa.shape[0]
    t = seq_len
    kspec = pl.BlockSpec((t, GLA_DK), lambda b, h: (b, h))
    vspec = pl.BlockSpec((t, GLA_DV), lambda b, h: (b, h))
    nm = masks.shape[0]
    block_bytes = (2 * _nbytes((t, GLA_DK), BF16) + 3 * _nbytes((t, GLA_DV), BF16)
                   + 2 * _nbytes((t, GLA_DK), F32) + _nbytes(masks.shape, F32))
    scratch_bytes = _nbytes((t, GLA_DV), F32) + 2 * _nbytes((GLA_DK, GLA_DV), F32)
    return pl.pallas_call(
        _gla_body,
        grid=(batch, GLA_HEADS),
        in_specs=[kspec, kspec, vspec, kspec, kspec, vspec,
                  pl.BlockSpec((1, GLA_DV), lambda b, h: (0, 0)),
                  pl.BlockSpec((nm, GLA_CHUNK, GLA_CHUNK), lambda b, h: (0, 0, 0))],
        out_specs=vspec,
        out_shape=jax.ShapeDtypeStruct((n, GLA_VALUE_DIM), BF16),
        scratch_shapes=[pltpu.VMEM((t, GLA_DV), F32),
                        pltpu.VMEM((GLA_DK, GLA_DV), F32),
                        pltpu.VMEM((GLA_DK, GLA_DV), F32)],
        compiler_params=pltpu.CompilerParams(
            dimension_semantics=("arbitrary", "arbitrary"),
            vmem_limit_bytes=_vmem_limit(block_bytes, scratch_bytes)),
        name="gla",
    )(qa, ka, va, gf, gb, za, gn, masks)


def _attn_body(qt_ref, k_ref, vt_ref, z_ref, o_ref, s_ref, acc_ref):
    tq = qt_ref.shape[1]
    t = k_ref.shape[1]
    kc = ATT_KEY_CHUNK
    nk = t // kc
    qt = qt_ref[...]
    q4t = jnp.concatenate([qt[ATT_HEAD_DIM * j:ATT_HEAD_DIM * (j + 1), :] for j in range(ATT_GROUP)], axis=1)
    cols = ATT_GROUP * tq

    def scores(c, m, buf):
        sl = pl.ds(pl.multiple_of(c * kc, kc), kc)
        s = jnp.dot(k_ref[0, sl, :], q4t, preferred_element_type=F32)
        s_ref[buf] = s
        return jnp.maximum(m, jnp.max(s, axis=0, keepdims=True))

    def weighted(c, m_old, m_new, buf):
        sl = pl.ds(pl.multiple_of(c * kc, kc), kc)
        p = jnp.exp(s_ref[buf] - m_new).astype(BF16)
        acc_ref[...] = (jnp.exp(m_old - m_new) * acc_ref[...]
                        + jnp.dot(vt_ref[:, sl], p, preferred_element_type=F32))

    acc_ref[...] = jnp.zeros_like(acc_ref)
    m_init = jnp.full((1, cols), -jnp.inf, F32)
    m_first = scores(0, m_init, 0)

    def pair(i, carry):
        m_a, m_b = carry
        c = 2 * i
        m_c = scores(c + 1, m_b, 1)
        weighted(c, m_a, m_b, 0)
        m_d = scores(c + 2, m_c, 0)
        weighted(c + 1, m_b, m_c, 1)
        return m_c, m_d

    m_a, m_b = lax.fori_loop(0, nk // 2 - 1, pair, (m_init, m_first))
    m_c = scores(nk - 1, m_b, 1)
    weighted(nk - 2, m_a, m_b, 0)
    weighted(nk - 1, m_b, m_c, 1)
    acc = acc_ref[...]
    ot = acc[0:ATT_HEAD_DIM, :] / acc[ATT_HEAD_DIM:ATT_HEAD_DIM + 1, :]
    o = jnp.concatenate([jnp.transpose(ot[:, tq * j:tq * (j + 1)]) for j in range(ATT_GROUP)], axis=1)
    o_ref[...] = (o * _silu(z_ref[...].astype(F32))).astype(BF16)


def _attn(qbt, kb, vbt, zb, batch, seq_len):
    n = zb.shape[0]
    t = seq_len
    tq = ATT_Q_ROWS
    nq = t // tq
    width = ATT_GROUP * ATT_HEAD_DIM
    ospec = pl.BlockSpec((tq, width), lambda b, g, i: (b * nq + i, g))
    block_bytes = (3 * _nbytes((tq, width), BF16) + _nbytes((t, V7X_LANES), BF16)
                   + _nbytes((2 * ATT_HEAD_DIM, t), BF16))
    scratch_bytes = (_nbytes((2, ATT_KEY_CHUNK, ATT_GROUP * tq), F32)
                     + _nbytes((2 * ATT_HEAD_DIM, ATT_GROUP * tq), F32))
    return pl.pallas_call(
        _attn_body,
        grid=(batch, ATT_KV_HEADS, nq),
        in_specs=[pl.BlockSpec((width, tq), lambda b, g, i: (g, b * nq + i)),
                  pl.BlockSpec((1, t, ATT_HEAD_DIM), lambda b, g, i: (g, b, 0)),
                  pl.BlockSpec((2 * ATT_HEAD_DIM, t), lambda b, g, i: (g, b)),
                  ospec],
        out_specs=ospec,
        out_shape=jax.ShapeDtypeStruct((n, ATT_Q_DIM), BF16),
        scratch_shapes=[pltpu.VMEM((2, ATT_KEY_CHUNK, ATT_GROUP * tq), F32),
                        pltpu.VMEM((2 * ATT_HEAD_DIM, ATT_GROUP * tq), F32)],
        compiler_params=pltpu.CompilerParams(
            dimension_semantics=("arbitrary", "arbitrary", "arbitrary"),
            vmem_limit_bytes=_vmem_limit(block_bytes, scratch_bytes)),
        name="attn",
    )(qbt, kb, vbt, zb)


def _outproj_body(oa_ref, ob_ref, ma_ref, mb_ref, x_ref, wa_ref, wb_ref, wo_ref, fg_ref, o_ref, *, final):
    ya = jnp.dot(oa_ref[...], wa_ref[...], preferred_element_type=F32)
    yb = jnp.dot(ob_ref[...], wb_ref[...], preferred_element_type=F32)
    merged = (jax.nn.sigmoid(ma_ref[...].astype(F32)) * ya
              + jax.nn.sigmoid(mb_ref[...].astype(F32)) * yb)
    y = x_ref[...] + jnp.dot(merged.astype(BF16), wo_ref[...], preferred_element_type=F32)
    if final:
        ms = jnp.mean(y * y, axis=-1, keepdims=True)
        y = y * lax.rsqrt(ms + EPS) * fg_ref[...]
    o_ref[...] = y


def _outproj(oa, ob, ma, mb, x, lw, final_gain, final):
    n = x.shape[0]
    tm = OUTPROJ_ROWS
    row = lambda i: (i, 0)
    const = lambda i: (0, 0)
    act = pl.BlockSpec((tm, D_MODEL), row)
    wspec = pl.BlockSpec((D_MODEL, D_MODEL), const, pipeline_mode=pl.Buffered(1))
    block_bytes = 4 * _nbytes((tm, D_MODEL), BF16) + 2 * _nbytes((tm, D_MODEL), F32)
    weight_bytes = 3 * _nbytes((D_MODEL, D_MODEL), BF16)
    return pl.pallas_call(
        functools.partial(_outproj_body, final=final),
        grid=(n // tm,),
        in_specs=[act, act, act, act, act, wspec, wspec, wspec,
                  pl.BlockSpec((1, D_MODEL), const, pipeline_mode=pl.Buffered(1))],
        out_specs=act,
        out_shape=jax.ShapeDtypeStruct((n, D_MODEL), F32),
        compiler_params=pltpu.CompilerParams(
            dimension_semantics=("arbitrary",),
            vmem_limit_bytes=_vmem_limit(block_bytes, weight_bytes)),
        name="outproj",
    )(oa, ob, ma, mb, x, lw["w_a"], lw["w_b"], lw["w_o"], final_gain)


def _rope_tables(seq_len):
    rows = seq_len // GRID_W
    r = jnp.repeat(jnp.arange(rows, dtype=F32), GRID_W)
    c = jnp.tile(jnp.arange(GRID_W, dtype=F32), rows)
    nf = ROPE_AXIS_DIM // 2
    inv = ROPE_THETA ** (-jnp.arange(nf, dtype=F32) / nf)
    ang_r = r[:, None] * inv
    ang_c = c[:, None] * inv
    cos = jnp.concatenate([jnp.cos(ang_r), jnp.cos(ang_r), jnp.cos(ang_c), jnp.cos(ang_c)], axis=-1)
    sin = jnp.concatenate([-jnp.sin(ang_r), jnp.sin(ang_r), -jnp.sin(ang_c), jnp.sin(ang_c)], axis=-1)
    reps = V7X_LANES // ATT_HEAD_DIM
    return jnp.tile(cos, (1, reps)), jnp.tile(sin, (1, reps))


def _layer_weights(l, norm_g, w_in, w_gate_f, b_gate_f, w_gate_b, b_gate_b, gla_norm_g, q_norm_g, k_norm_g,
                   w_branch_a, w_branch_b, w_out):
    pts = np.cumsum((0,) + IN_SIZES)
    col = lambda i: w_in[l][:, pts[i]:pts[i + 1]]
    w = w_in[l]
    w_gate = jnp.zeros((D_MODEL, V7X_LANES), F32).at[:, :2 * GATE_RANK].set(w[:, pts[3]:pts[5]])
    w_gf = jnp.zeros((V7X_LANES, GLA_KEY_DIM), F32).at[:GATE_RANK].set(w_gate_f[l])
    w_gb = jnp.zeros((V7X_LANES, GLA_KEY_DIM), F32).at[GATE_RANK:2 * GATE_RANK].set(w_gate_b[l])
    return {
        "norm_g": norm_g[l].reshape(1, D_MODEL),
        "w_gla": jnp.concatenate([col(0), col(1), col(2), col(5)], axis=1).astype(BF16),
        "w_gate": w_gate.astype(BF16),
        "w_gf": w_gf.astype(BF16),
        "w_gb": w_gb.astype(BF16),
        "b_gf": b_gate_f[l].reshape(1, GLA_KEY_DIM),
        "b_gb": b_gate_b[l].reshape(1, GLA_KEY_DIM),
        "w_att": jnp.concatenate([col(6), col(7), col(8), col(9)], axis=1).astype(BF16),
        "w_m": jnp.concatenate([col(10), col(11)], axis=1).astype(BF16),
        "q_gain": jnp.tile(q_norm_g[l], ATT_Q_HEADS).reshape(1, ATT_Q_DIM),
        "k_gain": jnp.tile(k_norm_g[l], ATT_KV_HEADS).reshape(1, ATT_KV_DIM),
        "gla_gain": gla_norm_g[l].reshape(1, GLA_DV),
        "w_a": w_branch_a[l].astype(BF16),
        "w_b": w_branch_b[l].astype(BF16),
        "w_o": w_out[l].astype(BF16),
    }


def _trunk(x3, layers, final_gain, masks, bd):
    batch, seq_len, _ = x3.shape
    x = x3.reshape(batch * seq_len, D_MODEL)
    cos, sin = _rope_tables(seq_len)
    for l, lw in enumerate(layers):
        qa, ka, va, za, gf, gb, qbt, kb, vbt, zb, ma, mb = _inproj(x, lw, cos, sin, bd, seq_len)
        oa = _gla(qa, ka, va, gf, gb, za, lw["gla_gain"], masks, batch, seq_len)
        ob = _attn(qbt, kb, vbt, zb, batch, seq_len)
        x = _outproj(oa, ob, ma, mb, x, lw, final_gain, final=(l == len(layers) - 1))
    return x.reshape(batch, seq_len, D_MODEL)


def kernel(x_prompt, x_sample, norm_g, w_in, w_gate_f, b_gate_f, w_gate_b, b_gate_b, gla_norm_g, q_norm_g,
           k_norm_g, w_branch_a, w_branch_b, w_out, final_norm_g):
    layers = [_layer_weights(l, norm_g, w_in, w_gate_f, b_gate_f, w_gate_b, b_gate_b, gla_norm_g, q_norm_g,
                             k_norm_g, w_branch_a, w_branch_b, w_out) for l in range(DEPTH)]
    final_gain = final_norm_g.reshape(1, D_MODEL)
    masks = _gla_masks()
    head = np.arange(V7X_MXU_DIM) // ATT_HEAD_DIM
    bd = jnp.asarray((head[:, None] == head[None, :]).astype(np.float32), dtype=BF16)
    y_prompt = _trunk(x_prompt, layers, final_gain, masks, bd)
    y_sample = _trunk(x_sample, layers, final_gain, masks, bd)
    return (y_prompt, y_sample)
```

```python
import functools

import numpy as np
import jax
import jax.numpy as jnp
from jax import lax
from jax.experimental import pallas as pl
from jax.experimental.pallas import tpu as pltpu

F32, BF16 = jnp.float32, jnp.bfloat16

D_MODEL = 1024
DEPTH = 4
EPS = 1e-6
GRID_W = 64
GLA_HEADS = 4
GLA_KEY_DIM = D_MODEL // 2
GLA_VALUE_DIM = D_MODEL
GLA_DK = GLA_KEY_DIM // GLA_HEADS
GLA_DV = GLA_VALUE_DIM // GLA_HEADS
GATE_RANK = 16
GATE_NORMALIZER = 16.0
ATT_HEAD_DIM = 64
ATT_Q_HEADS = D_MODEL // ATT_HEAD_DIM
ATT_KV_HEADS = 4
ATT_GROUP = ATT_Q_HEADS // ATT_KV_HEADS
ATT_Q_DIM = ATT_Q_HEADS * ATT_HEAD_DIM
ATT_KV_DIM = ATT_KV_HEADS * ATT_HEAD_DIM
ROPE_AXIS_DIM = ATT_HEAD_DIM // 2
ROPE_THETA = 10000.0
IN_SIZES = (GLA_KEY_DIM, GLA_KEY_DIM, GLA_VALUE_DIM, GATE_RANK, GATE_RANK, GLA_VALUE_DIM,
            ATT_Q_DIM, ATT_KV_DIM, ATT_KV_DIM, ATT_Q_DIM, D_MODEL, D_MODEL)

V7X_LANES = 128
V7X_SUBLANES = 8
V7X_MXU_DIM = 256
V7X_VMEM_BYTES = 64 * 1024 * 1024
V7X_VMEM_BUDGET = V7X_VMEM_BYTES - 8 * 1024 * 1024

INPROJ_ROWS = 256
OUTPROJ_ROWS = 512
GLA_CHUNK = 64
GLA_LEVELS = (1, 2, 4, 8, 16, 32)
GLA_FINAL_ROWS = 256
ATT_Q_ROWS = 128
ATT_KEY_CHUNK = 512

_NT = (((1,), (1,)), ((), ()))
_TN = (((0,), (0,)), ((), ()))


def _vmem_limit(block_bytes, scratch_bytes=0):
    want = 2 * block_bytes + scratch_bytes + 16 * 1024 * 1024
    return int(min(want, V7X_VMEM_BUDGET))


def _nbytes(shape, dtype):
    return int(np.prod(shape)) * jnp.dtype(dtype).itemsize


def _log_sigmoid(z):
    return -(jnp.maximum(-z, 0.0) + jnp.log1p(jnp.exp(-jnp.abs(z))))


def _silu(z):
    return z * jax.nn.sigmoid(z)


def _inproj_body(x_ref, ng_ref, wgla_ref, wgate_ref, wgf_ref, wgb_ref, bgf_ref, bgb_ref,
                 watt_ref, wm_ref, qng_ref, kng_ref, cos_ref, sin_ref, bd_ref,
                 qa_ref, ka_ref, va_ref, za_ref, gf_ref, gb_ref,
                 qbt_ref, kb_ref, vbt_ref, zb_ref, ma_ref, mb_ref):
    x = x_ref[...]
    ms = jnp.mean(x * x, axis=-1, keepdims=True)
    h = (x * lax.rsqrt(ms + EPS) * ng_ref[...]).astype(BF16)

    def proj(w_ref, lo, hi):
        return jnp.dot(h, w_ref[:, lo:hi], preferred_element_type=F32)

    qa_ref[...] = (proj(wgla_ref, 0, 512) * (GLA_DK ** -0.5)).astype(BF16)
    ka_ref[...] = proj(wgla_ref, 512, 1024).astype(BF16)
    va_ref[...] = proj(wgla_ref, 1024, 2048).astype(BF16)
    za_ref[...] = proj(wgla_ref, 2048, 3072).astype(BF16)

    pg = proj(wgate_ref, 0, V7X_LANES).astype(BF16)
    zf = jnp.dot(pg, wgf_ref[...], preferred_element_type=F32) + bgf_ref[...]
    zb = jnp.dot(pg, wgb_ref[...], preferred_element_type=F32) + bgb_ref[...]
    gf_ref[...] = _log_sigmoid(zf) * (1.0 / GATE_NORMALIZER)
    gb_ref[...] = _log_sigmoid(zb) * (1.0 / GATE_NORMALIZER)

    cos = cos_ref[...]
    sin = sin_ref[...]
    bd = bd_ref[...]
    lane = lax.broadcasted_iota(jnp.int32, cos.shape, 1)
    first_half = (lane & (ROPE_AXIS_DIM // 2)) == 0

    def norm_rope(p, gain):
        sq = (p * p).astype(BF16)
        msq = jnp.dot(sq, bd, preferred_element_type=F32) * (1.0 / ATT_HEAD_DIM)
        r = lax.rsqrt(msq + EPS)
        xg = p * gain
        outs = []
        for s in range(2):
            xs = xg[:, V7X_LANES * s:V7X_LANES * (s + 1)]
            partner = jnp.where(first_half,
                                pltpu.roll(xs, V7X_LANES - ROPE_AXIS_DIM // 2, 1),
                                pltpu.roll(xs, ROPE_AXIS_DIM // 2, 1))
            outs.append(xs * cos + partner * sin)
        return jnp.concatenate(outs, axis=1) * r

    for c in range(ATT_Q_DIM // V7X_MXU_DIM):
        lo = V7X_MXU_DIM * c
        p = proj(watt_ref, lo, lo + V7X_MXU_DIM)
        y = norm_rope(p, qng_ref[:, lo:lo + V7X_MXU_DIM]) * (ATT_HEAD_DIM ** -0.5)
        qbt_ref[lo:lo + V7X_MXU_DIM, :] = jnp.transpose(y).astype(BF16)
    kk = norm_rope(proj(watt_ref, 1024, 1280), kng_ref[...]).astype(BF16)
    vt = jnp.transpose(proj(watt_ref, 1280, 1536)).astype(BF16)
    ones = jnp.ones((ATT_HEAD_DIM, vt.shape[1]), BF16)
    for g in range(ATT_KV_HEADS):
        kb_ref[g] = kk[:, ATT_HEAD_DIM * g:ATT_HEAD_DIM * (g + 1)]
        vbt_ref[2 * ATT_HEAD_DIM * g:2 * ATT_HEAD_DIM * g + ATT_HEAD_DIM, :] = vt[ATT_HEAD_DIM * g:ATT_HEAD_DIM * (g + 1), :]
        vbt_ref[2 * ATT_HEAD_DIM * g + ATT_HEAD_DIM:2 * ATT_HEAD_DIM * (g + 1), :] = ones
    zb_ref[...] = proj(watt_ref, 1536, 2560).astype(BF16)

    ma_ref[...] = proj(wm_ref, 0, 1024).astype(BF16)
    mb_ref[...] = proj(wm_ref, 1024, 2048).astype(BF16)


def _inproj(x, lw, cos, sin, bd, seq_len):
    n = x.shape[0]
    tm = INPROJ_ROWS
    pos_blocks = seq_len // tm
    row = lambda i: (i, 0)
    const = lambda i: (0, 0)

    def wspec(shape):
        return pl.BlockSpec(shape, const, pipeline_mode=pl.Buffered(1))

    in_specs = [
        pl.BlockSpec((tm, D_MODEL), row),
        wspec((1, D_MODEL)),
        wspec((D_MODEL, 3072)),
        wspec((D_MODEL, V7X_LANES)),
        wspec((V7X_LANES, GLA_KEY_DIM)),
        wspec((V7X_LANES, GLA_KEY_DIM)),
        wspec((1, GLA_KEY_DIM)),
        wspec((1, GLA_KEY_DIM)),
        wspec((D_MODEL, 2560)),
        wspec((D_MODEL, 2048)),
        wspec((1, ATT_Q_DIM)),
        wspec((1, ATT_KV_DIM)),
        pl.BlockSpec((tm, V7X_LANES), lambda i: (i % pos_blocks, 0)),
        pl.BlockSpec((tm, V7X_LANES), lambda i: (i % pos_blocks, 0)),
        wspec((V7X_MXU_DIM, V7X_MXU_DIM)),
    ]
    out_shape = [
        jax.ShapeDtypeStruct((n, GLA_KEY_DIM), BF16),
        jax.ShapeDtypeStruct((n, GLA_KEY_DIM), BF16),
        jax.ShapeDtypeStruct((n, GLA_VALUE_DIM), BF16),
        jax.ShapeDtypeStruct((n, GLA_VALUE_DIM), BF16),
        jax.ShapeDtypeStruct((n, GLA_KEY_DIM), F32),
        jax.ShapeDtypeStruct((n, GLA_KEY_DIM), F32),
        jax.ShapeDtypeStruct((ATT_Q_DIM, n), BF16),
        jax.ShapeDtypeStruct((ATT_KV_HEADS, n, ATT_HEAD_DIM), BF16),
        jax.ShapeDtypeStruct((2 * ATT_KV_DIM, n), BF16),
        jax.ShapeDtypeStruct((n, ATT_Q_DIM), BF16),
        jax.ShapeDtypeStruct((n, D_MODEL), BF16),
        jax.ShapeDtypeStruct((n, D_MODEL), BF16),
    ]
    out_specs = [
        pl.BlockSpec((tm, GLA_KEY_DIM), row),
        pl.BlockSpec((tm, GLA_KEY_DIM), row),
        pl.BlockSpec((tm, GLA_VALUE_DIM), row),
        pl.BlockSpec((tm, GLA_VALUE_DIM), row),
        pl.BlockSpec((tm, GLA_KEY_DIM), row),
        pl.BlockSpec((tm, GLA_KEY_DIM), row),
        pl.BlockSpec((ATT_Q_DIM, tm), lambda i: (0, i)),
        pl.BlockSpec((ATT_KV_HEADS, tm, ATT_HEAD_DIM), lambda i: (0, i, 0)),
        pl.BlockSpec((2 * ATT_KV_DIM, tm), lambda i: (0, i)),
        pl.BlockSpec((tm, ATT_Q_DIM), row),
        pl.BlockSpec((tm, D_MODEL), row),
        pl.BlockSpec((tm, D_MODEL), row),
    ]
    weight_bytes = _nbytes((D_MODEL, 3072 + V7X_LANES + 2560 + 2048), BF16)
    block_bytes = (_nbytes((tm, D_MODEL), F32) + _nbytes((tm, 7 * 1024 + 512), BF16)
                   + 2 * _nbytes((tm, GLA_KEY_DIM), F32) + 2 * _nbytes((tm, V7X_LANES), F32))
    return pl.pallas_call(
        _inproj_body,
        grid=(n // tm,),
        in_specs=in_specs,
        out_specs=out_specs,
        out_shape=out_shape,
        compiler_params=pltpu.CompilerParams(
            dimension_semantics=("arbitrary",),
            vmem_limit_bytes=_vmem_limit(block_bytes, weight_bytes)),
        name="inproj",
    )(x, lw["norm_g"], lw["w_gla"], lw["w_gate"], lw["w_gf"], lw["w_gb"], lw["b_gf"], lw["b_gb"],
      lw["w_att"], lw["w_m"], lw["q_gain"], lw["k_gain"], cos, sin, bd)


def _gla_boundary(beta, m, reverse):
    c, width = beta.shape
    two_m = 2 * m
    off = m if reverse else m - 1
    pieces = []
    if two_m >= V7X_SUBLANES:
        for v in range(c // V7X_SUBLANES):
            r = (V7X_SUBLANES * v // two_m) * two_m + off
            pieces.append(jnp.broadcast_to(beta[r:r + 1, :], (V7X_SUBLANES, width)))
    else:
        sub = lax.broadcasted_iota(jnp.int32, (V7X_SUBLANES, width), 0)
        per = V7X_SUBLANES // two_m
        for v in range(c // V7X_SUBLANES):
            rows = [jnp.broadcast_to(beta[V7X_SUBLANES * v + two_m * j + off:V7X_SUBLANES * v + two_m * j + off + 1, :],
                                     (V7X_SUBLANES, width)) for j in range(per)]
            acc = rows[-1]
            for j in range(per - 2, -1, -1):
                acc = jnp.where(sub < two_m * (j + 1), rows[j], acc)
            pieces.append(acc)
    return jnp.concatenate(pieces, axis=0)


def _gla_chunk(q, k, v, g, s_ref, mask_ref, sgn_ref, mask_base, sgn_base, reverse):
    c = GLA_CHUNK
    row = lax.broadcasted_iota(jnp.int32, g.shape, 0)
    beta = g
    for sh in GLA_LEVELS:
        if reverse:
            beta = beta + jnp.where(row < c - sh, pltpu.roll(beta, c - sh, 0), 0.0)
        else:
            beta = beta + jnp.where(row >= sh, pltpu.roll(beta, sh, 0), 0.0)

    a = None
    nmask = 0
    if not reverse:
        a = mask_ref[mask_base] * lax.dot_general(q, k, _NT, preferred_element_type=F32)
        nmask = 1
    for li, m in enumerate(GLA_LEVELS):
        if m == 1:
            parity = (row & 1) == (0 if reverse else 1)
            f = jnp.exp(jnp.where(parity, g, 0.0))
        else:
            f = jnp.exp2((beta - _gla_boundary(beta, m, reverse)) * sgn_ref[sgn_base + li - 1])
        fb = f.astype(BF16)
        am = lax.dot_general(q * fb, k * fb, _NT, preferred_element_type=F32)
        am = mask_ref[mask_base + nmask] * am
        a = am if a is None else a + am
        nmask += 1

    o = jnp.dot(a.astype(BF16), v, preferred_element_type=F32)
    s = s_ref[...]
    o = o + jnp.dot(q * jnp.exp(beta).astype(BF16), s.astype(BF16), preferred_element_type=F32)
    tot = beta[0:1, :] if reverse else beta[c - 1:c, :]
    kd = k * jnp.exp(tot - beta).astype(BF16)
    kv = lax.dot_general(kd, v, _TN, preferred_element_type=F32)
    decay_col = jnp.transpose(jnp.broadcast_to(jnp.exp(tot), (V7X_SUBLANES, GLA_DK)))[:, 0:1]
    s_ref[...] = decay_col * s + kv
    return o


def _gla_body(q_ref, k_ref, v_ref, gf_ref, gb_ref, za_ref, gn_ref, mask_ref, sgn_ref, o_ref,
              acc_ref, sf_ref, sb_ref):
    t = q_ref.shape[0]
    c = GLA_CHUNK
    n = t // c
    acc_ref[...] = jnp.zeros_like(acc_ref)
    sf_ref[...] = jnp.zeros_like(sf_ref)
    sb_ref[...] = jnp.zeros_like(sb_ref)

    def step(i, carry):
        lo = pl.multiple_of(i * c, c)
        sl = pl.ds(lo, c)
        of = _gla_chunk(q_ref[sl, :], k_ref[sl, :], v_ref[sl, :], gf_ref[sl, :], sf_ref, mask_ref, sgn_ref, 0, 0, False)
        acc_ref[sl, :] += of
        hi = pl.multiple_of((n - 1 - i) * c, c)
        sh = pl.ds(hi, c)
        ob = _gla_chunk(q_ref[sh, :], k_ref[sh, :], v_ref[sh, :], gb_ref[sh, :], sb_ref, mask_ref, sgn_ref,
                        len(GLA_LEVELS) + 1, len(GLA_LEVELS) - 1, True)
        acc_ref[sh, :] += ob
        return carry

    lax.fori_loop(0, n, step, 0, unroll=2)

    rows = GLA_FINAL_ROWS

    def fin(i, carry):
        sl = pl.ds(pl.multiple_of(i * rows, rows), rows)
        o = acc_ref[sl, :]
        ms = jnp.mean(o * o, axis=-1, keepdims=True)
        y = o * lax.rsqrt(ms + EPS) * gn_ref[...]
        o_ref[sl, :] = (y * _silu(za_ref[sl, :].astype(F32))).astype(BF16)
        return carry

    lax.fori_loop(0, t // rows, fin, 0)


def _gla_masks():
    c = GLA_CHUNK
    ri = np.arange(c)[:, None]
    ci = np.arange(c)[None, :]
    masks = [ri == ci]
    for m in GLA_LEVELS:
        same = (ri // (2 * m)) == (ci // (2 * m))
        masks.append(same & ((ri % (2 * m)) >= m) & ((ci % (2 * m)) < m))
    for m in GLA_LEVELS:
        same = (ri // (2 * m)) == (ci // (2 * m))
        masks.append(same & ((ri % (2 * m)) < m) & ((ci % (2 * m)) >= m))
    sgn = []
    row = np.arange(c)[:, None] * np.ones((1, GLA_DK), np.int64)
    for reverse in (False, True):
        for m in GLA_LEVELS[1:]:
            upper = (row % (2 * m)) >= m
            sgn.append(np.where(upper != reverse, np.log2(np.e), -np.log2(np.e)))
    return jnp.asarray(np.stack(masks).astype(np.float32)), jnp.asarray(np.stack(sgn).astype(np.float32))


def _gla(qa, ka, va, gf, gb, za, gn, masks, sgn, batch, seq_len):
    n = qa.shape[0]
    t = seq_len
    kspec = pl.BlockSpec((t, GLA_DK), lambda b, h: (b, h))
    vspec = pl.BlockSpec((t, GLA_DV), lambda b, h: (b, h))
    nm = masks.shape[0]
    block_bytes = (2 * _nbytes((t, GLA_DK), BF16) + 3 * _nbytes((t, GLA_DV), BF16)
                   + 2 * _nbytes((t, GLA_DK), F32) + _nbytes(masks.shape, F32) + _nbytes(sgn.shape, F32))
    scratch_bytes = _nbytes((t, GLA_DV), F32) + 2 * _nbytes((GLA_DK, GLA_DV), F32)
    return pl.pallas_call(
        _gla_body,
        grid=(batch, GLA_HEADS),
        in_specs=[kspec, kspec, vspec, kspec, kspec, vspec,
                  pl.BlockSpec((1, GLA_DV), lambda b, h: (0, 0)),
                  pl.BlockSpec((nm, GLA_CHUNK, GLA_CHUNK), lambda b, h: (0, 0, 0)),
                  pl.BlockSpec(sgn.shape, lambda b, h: (0, 0, 0))],
        out_specs=vspec,
        out_shape=jax.ShapeDtypeStruct((n, GLA_VALUE_DIM), BF16),
        scratch_shapes=[pltpu.VMEM((t, GLA_DV), F32),
                        pltpu.VMEM((GLA_DK, GLA_DV), F32),
                        pltpu.VMEM((GLA_DK, GLA_DV), F32)],
        compiler_params=pltpu.CompilerParams(
            dimension_semantics=("arbitrary", "arbitrary"),
            vmem_limit_bytes=_vmem_limit(block_bytes, scratch_bytes)),
        name="gla",
    )(qa, ka, va, gf, gb, za, gn, masks, sgn)


def _attn_body(qt_ref, k_ref, vt_ref, z_ref, o_ref, s_ref, acc_ref):
    t = k_ref.shape[1]
    tq = ATT_Q_ROWS
    nq = t // tq
    kc = ATT_KEY_CHUNK
    nk = t // kc
    cols = ATT_GROUP * tq

    def load_q(i):
        qt = qt_ref[:, pl.ds(pl.multiple_of(i * tq, tq), tq)]
        return jnp.concatenate([qt[ATT_HEAD_DIM * j:ATT_HEAD_DIM * (j + 1), :] for j in range(ATT_GROUP)], axis=1)

    def scores(q4t, c, m, buf):
        sl = pl.ds(pl.multiple_of(c * kc, kc), kc)
        s = jnp.dot(k_ref[0, sl, :], q4t, preferred_element_type=F32)
        s_ref[buf] = s
        return jnp.maximum(m, jnp.max(s, axis=0, keepdims=True))

    def weighted(c, m_old, m_new, buf):
        sl = pl.ds(pl.multiple_of(c * kc, kc), kc)
        p = jnp.exp(s_ref[buf] - m_new).astype(BF16)
        acc_ref[...] = (jnp.exp(m_old - m_new) * acc_ref[...]
                        + jnp.dot(vt_ref[:, sl], p, preferred_element_type=F32))

    m_init = jnp.full((1, cols), -jnp.inf, F32)

    def tile(i, m_first):
        q4t = load_q(i)
        acc_ref[...] = jnp.zeros_like(acc_ref)

        def pair(j, carry):
            m_a, m_b = carry
            c = 2 * j
            m_c = scores(q4t, c + 1, m_b, 1)
            weighted(c, m_a, m_b, 0)
            m_d = scores(q4t, c + 2, m_c, 0)
            weighted(c + 1, m_b, m_c, 1)
            return m_c, m_d

        m_a, m_b = lax.fori_loop(0, nk // 2 - 1, pair, (m_init, m_first))
        m_c = scores(q4t, nk - 1, m_b, 1)
        weighted(nk - 2, m_a, m_b, 0)
        m_next = scores(load_q(jnp.minimum(i + 1, nq - 1)), 0, m_init, 0)
        weighted(nk - 1, m_b, m_c, 1)
        acc = acc_ref[...]
        ot = acc[0:ATT_HEAD_DIM, :] / acc[ATT_HEAD_DIM:ATT_HEAD_DIM + 1, :]
        o = jnp.concatenate([jnp.transpose(ot[:, tq * j:tq * (j + 1)]) for j in range(ATT_GROUP)], axis=1)
        rows = pl.ds(pl.multiple_of(i * tq, tq), tq)
        o_ref[rows, :] = (o * _silu(z_ref[rows, :].astype(F32))).astype(BF16)
        return m_next

    lax.fori_loop(0, nq, tile, scores(load_q(0), 0, m_init, 0))


def _attn(qbt, kb, vbt, zb, batch, seq_len):
    n = zb.shape[0]
    t = seq_len
    tq = ATT_Q_ROWS
    width = ATT_GROUP * ATT_HEAD_DIM
    ospec = pl.BlockSpec((t, width), lambda b, g: (b, g))
    block_bytes = (3 * _nbytes((t, width), BF16) + _nbytes((t, V7X_LANES), BF16)
                   + _nbytes((2 * ATT_HEAD_DIM, t), BF16))
    scratch_bytes = (_nbytes((2, ATT_KEY_CHUNK, ATT_GROUP * tq), F32)
                     + _nbytes((2 * ATT_HEAD_DIM, ATT_GROUP * tq), F32))
    return pl.pallas_call(
        _attn_body,
        grid=(batch, ATT_KV_HEADS),
        in_specs=[pl.BlockSpec((width, t), lambda b, g: (g, b)),
                  pl.BlockSpec((1, t, ATT_HEAD_DIM), lambda b, g: (g, b, 0)),
                  pl.BlockSpec((2 * ATT_HEAD_DIM, t), lambda b, g: (g, b)),
                  ospec],
        out_specs=ospec,
        out_shape=jax.ShapeDtypeStruct((n, ATT_Q_DIM), BF16),
        scratch_shapes=[pltpu.VMEM((2, ATT_KEY_CHUNK, ATT_GROUP * tq), F32),
                        pltpu.VMEM((2 * ATT_HEAD_DIM, ATT_GROUP * tq), F32)],
        compiler_params=pltpu.CompilerParams(
            dimension_semantics=("arbitrary", "arbitrary"),
            vmem_limit_bytes=_vmem_limit(block_bytes, scratch_bytes)),
        name="attn",
    )(qbt, kb, vbt, zb)


def _outproj_body(oa_ref, ob_ref, ma_ref, mb_ref, x_ref, wa_ref, wb_ref, wo_ref, fg_ref, o_ref, *, final):
    ya = jnp.dot(oa_ref[...], wa_ref[...], preferred_element_type=F32)
    yb = jnp.dot(ob_ref[...], wb_ref[...], preferred_element_type=F32)
    merged = (jax.nn.sigmoid(ma_ref[...].astype(F32)) * ya
              + jax.nn.sigmoid(mb_ref[...].astype(F32)) * yb)
    y = x_ref[...] + jnp.dot(merged.astype(BF16), wo_ref[...], preferred_element_type=F32)
    if final:
        ms = jnp.mean(y * y, axis=-1, keepdims=True)
        y = y * lax.rsqrt(ms + EPS) * fg_ref[...]
    o_ref[...] = y


def _outproj(oa, ob, ma, mb, x, lw, final_gain, final):
    n = x.shape[0]
    tm = OUTPROJ_ROWS
    row = lambda i: (i, 0)
    const = lambda i: (0, 0)
    act = pl.BlockSpec((tm, D_MODEL), row)
    wspec = pl.BlockSpec((D_MODEL, D_MODEL), const, pipeline_mode=pl.Buffered(1))
    block_bytes = 4 * _nbytes((tm, D_MODEL), BF16) + 2 * _nbytes((tm, D_MODEL), F32)
    weight_bytes = 3 * _nbytes((D_MODEL, D_MODEL), BF16)
    return pl.pallas_call(
        functools.partial(_outproj_body, final=final),
        grid=(n // tm,),
        in_specs=[act, act, act, act, act, wspec, wspec, wspec,
                  pl.BlockSpec((1, D_MODEL), const, pipeline_mode=pl.Buffered(1))],
        out_specs=act,
        out_shape=jax.ShapeDtypeStruct((n, D_MODEL), F32),
        compiler_params=pltpu.CompilerParams(
            dimension_semantics=("arbitrary",),
            vmem_limit_bytes=_vmem_limit(block_bytes, weight_bytes)),
        name="outproj",
    )(oa, ob, ma, mb, x, lw["w_a"], lw["w_b"], lw["w_o"], final_gain)


def _rope_tables(seq_len):
    rows = seq_len // GRID_W
    r = jnp.repeat(jnp.arange(rows, dtype=F32), GRID_W)
    c = jnp.tile(jnp.arange(GRID_W, dtype=F32), rows)
    nf = ROPE_AXIS_DIM // 2
    inv = ROPE_THETA ** (-jnp.arange(nf, dtype=F32) / nf)
    ang_r = r[:, None] * inv
    ang_c = c[:, None] * inv
    cos = jnp.concatenate([jnp.cos(ang_r), jnp.cos(ang_r), jnp.cos(ang_c), jnp.cos(ang_c)], axis=-1)
    sin = jnp.concatenate([-jnp.sin(ang_r), jnp.sin(ang_r), -jnp.sin(ang_c), jnp.sin(ang_c)], axis=-1)
    reps = V7X_LANES // ATT_HEAD_DIM
    return jnp.tile(cos, (1, reps)), jnp.tile(sin, (1, reps))


def _layer_weights(l, norm_g, w_in, w_gate_f, b_gate_f, w_gate_b, b_gate_b, gla_norm_g, q_norm_g, k_norm_g,
                   w_branch_a, w_branch_b, w_out):
    pts = np.cumsum((0,) + IN_SIZES)
    col = lambda i: w_in[l][:, pts[i]:pts[i + 1]]
    w = w_in[l]
    w_gate = jnp.zeros((D_MODEL, V7X_LANES), F32).at[:, :2 * GATE_RANK].set(w[:, pts[3]:pts[5]])
    w_gf = jnp.zeros((V7X_LANES, GLA_KEY_DIM), F32).at[:GATE_RANK].set(w_gate_f[l])
    w_gb = jnp.zeros((V7X_LANES, GLA_KEY_DIM), F32).at[GATE_RANK:2 * GATE_RANK].set(w_gate_b[l])
    return {
        "norm_g": norm_g[l].reshape(1, D_MODEL),
        "w_gla": jnp.concatenate([col(0), col(1), col(2), col(5)], axis=1).astype(BF16),
        "w_gate": w_gate.astype(BF16),
        "w_gf": w_gf.astype(BF16),
        "w_gb": w_gb.astype(BF16),
        "b_gf": b_gate_f[l].reshape(1, GLA_KEY_DIM),
        "b_gb": b_gate_b[l].reshape(1, GLA_KEY_DIM),
        "w_att": jnp.concatenate([col(6), col(7), col(8), col(9)], axis=1).astype(BF16),
        "w_m": jnp.concatenate([col(10), col(11)], axis=1).astype(BF16),
        "q_gain": jnp.tile(q_norm_g[l], ATT_Q_HEADS).reshape(1, ATT_Q_DIM),
        "k_gain": jnp.tile(k_norm_g[l], ATT_KV_HEADS).reshape(1, ATT_KV_DIM),
        "gla_gain": gla_norm_g[l].reshape(1, GLA_DV),
        "w_a": w_branch_a[l].astype(BF16),
        "w_b": w_branch_b[l].astype(BF16),
        "w_o": w_out[l].astype(BF16),
    }


def _trunk(x3, layers, final_gain, masks, sgn, bd):
    batch, seq_len, _ = x3.shape
    x = x3.reshape(batch * seq_len, D_MODEL)
    cos, sin = _rope_tables(seq_len)
    for l, lw in enumerate(layers):
        qa, ka, va, za, gf, gb, qbt, kb, vbt, zb, ma, mb = _inproj(x, lw, cos, sin, bd, seq_len)
        oa = _gla(qa, ka, va, gf, gb, za, lw["gla_gain"], masks, sgn, batch, seq_len)
        ob = _attn(qbt, kb, vbt, zb, batch, seq_len)
        x = _outproj(oa, ob, ma, mb, x, lw, final_gain, final=(l == len(layers) - 1))
    return x.reshape(batch, seq_len, D_MODEL)


def kernel(x_prompt, x_sample, norm_g, w_in, w_gate_f, b_gate_f, w_gate_b, b_gate_b, gla_norm_g, q_norm_g,
           k_norm_g, w_branch_a, w_branch_b, w_out, final_norm_g):
    layers = [_layer_weights(l, norm_g, w_in, w_gate_f, b_gate_f, w_gate_b, b_gate_b, gla_norm_g, q_norm_g,
                             k_norm_g, w_branch_a, w_branch_b, w_out) for l in range(DEPTH)]
    final_gain = final_norm_g.reshape(1, D_MODEL)
    masks, sgn = _gla_masks()
    head = np.arange(V7X_MXU_DIM) // ATT_HEAD_DIM
    bd = jnp.asarray((head[:, None] == head[None, :]).astype(np.float32), dtype=BF16)
    y_prompt = _trunk(x_prompt, layers, final_gain, masks, sgn, bd)
    y_sample = _trunk(x_sample, layers, final_gain, masks, sgn, bd)
    return (y_prompt, y_sample)
```

```python
import functools

import numpy as np
import jax
import jax.numpy as jnp
from jax import lax
from jax.experimental import pallas as pl
from jax.experimental.pallas import tpu as pltpu

F32, BF16 = jnp.float32, jnp.bfloat16

D_MODEL = 1024
DEPTH = 4
EPS = 1e-6
GRID_W = 64
GLA_HEADS = 4
GLA_KEY_DIM = D_MODEL // 2
GLA_VALUE_DIM = D_MODEL
GLA_DK = GLA_KEY_DIM // GLA_HEADS
GLA_DV = GLA_VALUE_DIM // GLA_HEADS
GATE_RANK = 16
GATE_NORMALIZER = 16.0
ATT_HEAD_DIM = 64
ATT_Q_HEADS = D_MODEL // ATT_HEAD_DIM
ATT_KV_HEADS = 4
ATT_GROUP = ATT_Q_HEADS // ATT_KV_HEADS
ATT_Q_DIM = ATT_Q_HEADS * ATT_HEAD_DIM
ATT_KV_DIM = ATT_KV_HEADS * ATT_HEAD_DIM
ROPE_AXIS_DIM = ATT_HEAD_DIM // 2
ROPE_THETA = 10000.0
IN_SIZES = (GLA_KEY_DIM, GLA_KEY_DIM, GLA_VALUE_DIM, GATE_RANK, GATE_RANK, GLA_VALUE_DIM,
            ATT_Q_DIM, ATT_KV_DIM, ATT_KV_DIM, ATT_Q_DIM, D_MODEL, D_MODEL)

V7X_LANES = 128
V7X_SUBLANES = 8
V7X_MXU_DIM = 256
V7X_VMEM_BYTES = 64 * 1024 * 1024
V7X_VMEM_BUDGET = V7X_VMEM_BYTES - 8 * 1024 * 1024

INPROJ_ROWS = 512
OUTPROJ_ROWS = 512
GLA_CHUNK = 64
GLA_LEVELS = (1, 2, 4, 8, 16, 32)
GLA_FINAL_ROWS = 256
ATT_Q_ROWS = 256
ATT_KEY_CHUNK = 512

_NT = (((1,), (1,)), ((), ()))
_TN = (((0,), (0,)), ((), ()))


def _vmem_limit(block_bytes, scratch_bytes=0):
    want = 2 * block_bytes + scratch_bytes + 16 * 1024 * 1024
    return int(min(want, V7X_VMEM_BUDGET))


def _nbytes(shape, dtype):
    return int(np.prod(shape)) * jnp.dtype(dtype).itemsize


def _log_sigmoid(z):
    return -(jnp.maximum(-z, 0.0) + jnp.log1p(jnp.exp(-jnp.abs(z))))


def _silu(z):
    return z * jax.nn.sigmoid(z)


def _inproj_body(x_ref, ng_ref, wgla_ref, wgate_ref, wgf_ref, wgb_ref, bgf_ref, bgb_ref,
                 watt_ref, wm_ref, qng_ref, kng_ref, cos_ref, sin_ref, bd_ref,
                 qa_ref, ka_ref, va_ref, za_ref, gf_ref, gb_ref,
                 qbt_ref, kb_ref, vbt_ref, zb_ref, ma_ref, mb_ref):
    x = x_ref[...]
    ms = jnp.mean(x * x, axis=-1, keepdims=True)
    h = (x * lax.rsqrt(ms + EPS) * ng_ref[...]).astype(BF16)

    def proj(w_ref, lo, hi):
        return jnp.dot(h, w_ref[:, lo:hi], preferred_element_type=F32)

    qa_ref[...] = (proj(wgla_ref, 0, 512) * (GLA_DK ** -0.5)).astype(BF16)
    ka_ref[...] = proj(wgla_ref, 512, 1024).astype(BF16)
    va_ref[...] = proj(wgla_ref, 1024, 2048).astype(BF16)
    za_ref[...] = proj(wgla_ref, 2048, 3072).astype(BF16)

    pg = proj(wgate_ref, 0, V7X_LANES).astype(BF16)
    zf = jnp.dot(pg, wgf_ref[...], preferred_element_type=F32) + bgf_ref[...]
    zb = jnp.dot(pg, wgb_ref[...], preferred_element_type=F32) + bgb_ref[...]
    gf_ref[...] = _log_sigmoid(zf) * (1.0 / GATE_NORMALIZER)
    gb_ref[...] = _log_sigmoid(zb) * (1.0 / GATE_NORMALIZER)

    cos = cos_ref[...]
    sin = sin_ref[...]
    bd = bd_ref[...]
    lane = lax.broadcasted_iota(jnp.int32, cos.shape, 1)
    first_half = (lane & (ROPE_AXIS_DIM // 2)) == 0

    def norm_rope(p, gain):
        sq = (p * p).astype(BF16)
        msq = jnp.dot(sq, bd, preferred_element_type=F32) * (1.0 / ATT_HEAD_DIM)
        r = lax.rsqrt(msq + EPS)
        xg = p * gain
        outs = []
        for s in range(2):
            xs = xg[:, V7X_LANES * s:V7X_LANES * (s + 1)]
            partner = jnp.where(first_half,
                                pltpu.roll(xs, V7X_LANES - ROPE_AXIS_DIM // 2, 1),
                                pltpu.roll(xs, ROPE_AXIS_DIM // 2, 1))
            outs.append(xs * cos + partner * sin)
        return jnp.concatenate(outs, axis=1) * r

    for c in range(ATT_Q_DIM // V7X_MXU_DIM):
        lo = V7X_MXU_DIM * c
        p = proj(watt_ref, lo, lo + V7X_MXU_DIM)
        y = norm_rope(p, qng_ref[:, lo:lo + V7X_MXU_DIM]) * (ATT_HEAD_DIM ** -0.5)
        qbt_ref[lo:lo + V7X_MXU_DIM, :] = jnp.transpose(y).astype(BF16)
    kk = norm_rope(proj(watt_ref, 1024, 1280), kng_ref[...]).astype(BF16)
    vt = jnp.transpose(proj(watt_ref, 1280, 1536)).astype(BF16)
    ones = jnp.ones((ATT_HEAD_DIM, vt.shape[1]), BF16)
    for g in range(ATT_KV_HEADS):
        kb_ref[g] = kk[:, ATT_HEAD_DIM * g:ATT_HEAD_DIM * (g + 1)]
        vbt_ref[2 * ATT_HEAD_DIM * g:2 * ATT_HEAD_DIM * g + ATT_HEAD_DIM, :] = vt[ATT_HEAD_DIM * g:ATT_HEAD_DIM * (g + 1), :]
        vbt_ref[2 * ATT_HEAD_DIM * g + ATT_HEAD_DIM:2 * ATT_HEAD_DIM * (g + 1), :] = ones
    zb_ref[...] = proj(watt_ref, 1536, 2560).astype(BF16)

    ma_ref[...] = proj(wm_ref, 0, 1024).astype(BF16)
    mb_ref[...] = proj(wm_ref, 1024, 2048).astype(BF16)


def _inproj(x, lw, cos, sin, bd, seq_len):
    n = x.shape[0]
    tm = INPROJ_ROWS
    pos_blocks = seq_len // tm
    row = lambda i: (i, 0)
    const = lambda i: (0, 0)

    def wspec(shape):
        return pl.BlockSpec(shape, const, pipeline_mode=pl.Buffered(1))

    in_specs = [
        pl.BlockSpec((tm, D_MODEL), row),
        wspec((1, D_MODEL)),
        wspec((D_MODEL, 3072)),
        wspec((D_MODEL, V7X_LANES)),
        wspec((V7X_LANES, GLA_KEY_DIM)),
        wspec((V7X_LANES, GLA_KEY_DIM)),
        wspec((1, GLA_KEY_DIM)),
        wspec((1, GLA_KEY_DIM)),
        wspec((D_MODEL, 2560)),
        wspec((D_MODEL, 2048)),
        wspec((1, ATT_Q_DIM)),
        wspec((1, ATT_KV_DIM)),
        pl.BlockSpec((tm, V7X_LANES), lambda i: (i % pos_blocks, 0)),
        pl.BlockSpec((tm, V7X_LANES), lambda i: (i % pos_blocks, 0)),
        wspec((V7X_MXU_DIM, V7X_MXU_DIM)),
    ]
    out_shape = [
        jax.ShapeDtypeStruct((n, GLA_KEY_DIM), BF16),
        jax.ShapeDtypeStruct((n, GLA_KEY_DIM), BF16),
        jax.ShapeDtypeStruct((n, GLA_VALUE_DIM), BF16),
        jax.ShapeDtypeStruct((n, GLA_VALUE_DIM), BF16),
        jax.ShapeDtypeStruct((n, GLA_KEY_DIM), F32),
        jax.ShapeDtypeStruct((n, GLA_KEY_DIM), F32),
        jax.ShapeDtypeStruct((ATT_Q_DIM, n), BF16),
        jax.ShapeDtypeStruct((ATT_KV_HEADS, n, ATT_HEAD_DIM), BF16),
        jax.ShapeDtypeStruct((2 * ATT_KV_DIM, n), BF16),
        jax.ShapeDtypeStruct((n, ATT_Q_DIM), BF16),
        jax.ShapeDtypeStruct((n, D_MODEL), BF16),
        jax.ShapeDtypeStruct((n, D_MODEL), BF16),
    ]
    out_specs = [
        pl.BlockSpec((tm, GLA_KEY_DIM), row),
        pl.BlockSpec((tm, GLA_KEY_DIM), row),
        pl.BlockSpec((tm, GLA_VALUE_DIM), row),
        pl.BlockSpec((tm, GLA_VALUE_DIM), row),
        pl.BlockSpec((tm, GLA_KEY_DIM), row),
        pl.BlockSpec((tm, GLA_KEY_DIM), row),
        pl.BlockSpec((ATT_Q_DIM, tm), lambda i: (0, i)),
        pl.BlockSpec((ATT_KV_HEADS, tm, ATT_HEAD_DIM), lambda i: (0, i, 0)),
        pl.BlockSpec((2 * ATT_KV_DIM, tm), lambda i: (0, i)),
        pl.BlockSpec((tm, ATT_Q_DIM), row),
        pl.BlockSpec((tm, D_MODEL), row),
        pl.BlockSpec((tm, D_MODEL), row),
    ]
    weight_bytes = _nbytes((D_MODEL, 3072 + V7X_LANES + 2560 + 2048), BF16)
    block_bytes = (_nbytes((tm, D_MODEL), F32) + _nbytes((tm, 7 * 1024 + 512), BF16)
                   + 2 * _nbytes((tm, GLA_KEY_DIM), F32) + 2 * _nbytes((tm, V7X_LANES), F32))
    return pl.pallas_call(
        _inproj_body,
        grid=(n // tm,),
        in_specs=in_specs,
        out_specs=out_specs,
        out_shape=out_shape,
        compiler_params=pltpu.CompilerParams(
            dimension_semantics=("arbitrary",),
            vmem_limit_bytes=_vmem_limit(block_bytes, weight_bytes)),
        name="inproj",
    )(x, lw["norm_g"], lw["w_gla"], lw["w_gate"], lw["w_gf"], lw["w_gb"], lw["b_gf"], lw["b_gb"],
      lw["w_att"], lw["w_m"], lw["q_gain"], lw["k_gain"], cos, sin, bd)


def _gla_boundary(beta, m, reverse):
    c, width = beta.shape
    two_m = 2 * m
    off = m if reverse else m - 1
    pieces = []
    if two_m >= V7X_SUBLANES:
        for v in range(c // V7X_SUBLANES):
            r = (V7X_SUBLANES * v // two_m) * two_m + off
            pieces.append(jnp.broadcast_to(beta[r:r + 1, :], (V7X_SUBLANES, width)))
    else:
        sub = lax.broadcasted_iota(jnp.int32, (V7X_SUBLANES, width), 0)
        per = V7X_SUBLANES // two_m
        for v in range(c // V7X_SUBLANES):
            rows = [jnp.broadcast_to(beta[V7X_SUBLANES * v + two_m * j + off:V7X_SUBLANES * v + two_m * j + off + 1, :],
                                     (V7X_SUBLANES, width)) for j in range(per)]
            acc = rows[-1]
            for j in range(per - 2, -1, -1):
                acc = jnp.where(sub < two_m * (j + 1), rows[j], acc)
            pieces.append(acc)
    return jnp.concatenate(pieces, axis=0)


def _gla_chunk(q, k, v, g, s_ref, mask_ref, sgn_ref, mask_base, sgn_base, reverse):
    c = GLA_CHUNK
    row = lax.broadcasted_iota(jnp.int32, g.shape, 0)
    beta = g
    for sh in GLA_LEVELS:
        if reverse:
            beta = beta + jnp.where(row < c - sh, pltpu.roll(beta, c - sh, 0), 0.0)
        else:
            beta = beta + jnp.where(row >= sh, pltpu.roll(beta, sh, 0), 0.0)

    a = None
    nmask = 0
    if not reverse:
        a = mask_ref[mask_base] * lax.dot_general(q, k, _NT, preferred_element_type=F32)
        nmask = 1
    for li, m in enumerate(GLA_LEVELS):
        if m == 1:
            parity = (row & 1) == (0 if reverse else 1)
            f = jnp.exp(jnp.where(parity, g, 0.0))
        else:
            f = jnp.exp2((beta - _gla_boundary(beta, m, reverse)) * sgn_ref[sgn_base + li - 1])
        fb = f.astype(BF16)
        am = lax.dot_general(q * fb, k * fb, _NT, preferred_element_type=F32)
        am = mask_ref[mask_base + nmask] * am
        a = am if a is None else a + am
        nmask += 1

    o = jnp.dot(a.astype(BF16), v, preferred_element_type=F32)
    s = s_ref[...]
    o = o + jnp.dot(q * jnp.exp(beta).astype(BF16), s.astype(BF16), preferred_element_type=F32)
    tot = beta[0:1, :] if reverse else beta[c - 1:c, :]
    kd = k * jnp.exp(tot - beta).astype(BF16)
    kv = lax.dot_general(kd, v, _TN, preferred_element_type=F32)
    decay_col = jnp.transpose(jnp.broadcast_to(jnp.exp(tot), (V7X_SUBLANES, GLA_DK)))[:, 0:1]
    s_ref[...] = decay_col * s + kv
    return o


def _gla_body(q_ref, k_ref, v_ref, gf_ref, gb_ref, za_ref, gn_ref, mask_ref, sgn_ref, o_ref,
              acc_ref, sf_ref, sb_ref):
    t = q_ref.shape[0]
    c = GLA_CHUNK
    n = t // c
    acc_ref[...] = jnp.zeros_like(acc_ref)
    sf_ref[...] = jnp.zeros_like(sf_ref)
    sb_ref[...] = jnp.zeros_like(sb_ref)

    def step(i, carry):
        lo = pl.multiple_of(i * c, c)
        sl = pl.ds(lo, c)
        of = _gla_chunk(q_ref[sl, :], k_ref[sl, :], v_ref[sl, :], gf_ref[sl, :], sf_ref, mask_ref, sgn_ref, 0, 0, False)
        acc_ref[sl, :] += of
        hi = pl.multiple_of((n - 1 - i) * c, c)
        sh = pl.ds(hi, c)
        ob = _gla_chunk(q_ref[sh, :], k_ref[sh, :], v_ref[sh, :], gb_ref[sh, :], sb_ref, mask_ref, sgn_ref,
                        len(GLA_LEVELS) + 1, len(GLA_LEVELS) - 1, True)
        acc_ref[sh, :] += ob
        return carry

    lax.fori_loop(0, n, step, 0, unroll=2)

    rows = GLA_FINAL_ROWS

    def fin(i, carry):
        sl = pl.ds(pl.multiple_of(i * rows, rows), rows)
        o = acc_ref[sl, :]
        ms = jnp.mean(o * o, axis=-1, keepdims=True)
        y = o * lax.rsqrt(ms + EPS) * gn_ref[...]
        o_ref[sl, :] = (y * _silu(za_ref[sl, :].astype(F32))).astype(BF16)
        return carry

    lax.fori_loop(0, t // rows, fin, 0)


def _gla_masks():
    c = GLA_CHUNK
    ri = np.arange(c)[:, None]
    ci = np.arange(c)[None, :]
    masks = [ri == ci]
    for m in GLA_LEVELS:
        same = (ri // (2 * m)) == (ci // (2 * m))
        masks.append(same & ((ri % (2 * m)) >= m) & ((ci % (2 * m)) < m))
    for m in GLA_LEVELS:
        same = (ri // (2 * m)) == (ci // (2 * m))
        masks.append(same & ((ri % (2 * m)) < m) & ((ci % (2 * m)) >= m))
    sgn = []
    row = np.arange(c)[:, None] * np.ones((1, GLA_DK), np.int64)
    for reverse in (False, True):
        for m in GLA_LEVELS[1:]:
            upper = (row % (2 * m)) >= m
            sgn.append(np.where(upper != reverse, np.log2(np.e), -np.log2(np.e)))
    return jnp.asarray(np.stack(masks).astype(np.float32)), jnp.asarray(np.stack(sgn).astype(np.float32))


def _gla(qa, ka, va, gf, gb, za, gn, masks, sgn, batch, seq_len):
    n = qa.shape[0]
    t = seq_len
    kspec = pl.BlockSpec((t, GLA_DK), lambda b, h: (b, h))
    vspec = pl.BlockSpec((t, GLA_DV), lambda b, h: (b, h))
    nm = masks.shape[0]
    block_bytes = (2 * _nbytes((t, GLA_DK), BF16) + 3 * _nbytes((t, GLA_DV), BF16)
                   + 2 * _nbytes((t, GLA_DK), F32) + _nbytes(masks.shape, F32) + _nbytes(sgn.shape, F32))
    scratch_bytes = _nbytes((t, GLA_DV), F32) + 2 * _nbytes((GLA_DK, GLA_DV), F32)
    return pl.pallas_call(
        _gla_body,
        grid=(batch, GLA_HEADS),
        in_specs=[kspec, kspec, vspec, kspec, kspec, vspec,
                  pl.BlockSpec((1, GLA_DV), lambda b, h: (0, 0)),
                  pl.BlockSpec((nm, GLA_CHUNK, GLA_CHUNK), lambda b, h: (0, 0, 0)),
                  pl.BlockSpec(sgn.shape, lambda b, h: (0, 0, 0))],
        out_specs=vspec,
        out_shape=jax.ShapeDtypeStruct((n, GLA_VALUE_DIM), BF16),
        scratch_shapes=[pltpu.VMEM((t, GLA_DV), F32),
                        pltpu.VMEM((GLA_DK, GLA_DV), F32),
                        pltpu.VMEM((GLA_DK, GLA_DV), F32)],
        compiler_params=pltpu.CompilerParams(
            dimension_semantics=("arbitrary", "arbitrary"),
            vmem_limit_bytes=_vmem_limit(block_bytes, scratch_bytes)),
        name="gla",
    )(qa, ka, va, gf, gb, za, gn, masks, sgn)


def _attn_body(qt_ref, k_ref, vt_ref, z_ref, o_ref, s_ref, acc_ref):
    t = k_ref.shape[1]
    tq = ATT_Q_ROWS
    nq = t // tq
    kc = ATT_KEY_CHUNK
    nk = t // kc
    cols = ATT_GROUP * tq

    def load_q(i):
        qt = qt_ref[:, pl.ds(pl.multiple_of(i * tq, tq), tq)]
        return jnp.concatenate([qt[ATT_HEAD_DIM * j:ATT_HEAD_DIM * (j + 1), :] for j in range(ATT_GROUP)], axis=1)

    def scores(q4t, c, m, buf):
        sl = pl.ds(pl.multiple_of(c * kc, kc), kc)
        s = jnp.dot(k_ref[0, sl, :], q4t, preferred_element_type=F32)
        s_ref[buf] = s
        return jnp.maximum(m, jnp.max(s, axis=0, keepdims=True))

    def weighted(c, m_old, m_new, buf):
        sl = pl.ds(pl.multiple_of(c * kc, kc), kc)
        p = jnp.exp(s_ref[buf] - m_new).astype(BF16)
        acc_ref[...] = (jnp.exp(m_old - m_new) * acc_ref[...]
                        + jnp.dot(vt_ref[:, sl], p, preferred_element_type=F32))

    m_init = jnp.full((1, cols), -jnp.inf, F32)

    def tile(i, m_first):
        q4t = load_q(i)
        acc_ref[...] = jnp.zeros_like(acc_ref)

        def pair(j, carry):
            m_a, m_b = carry
            c = 2 * j
            m_c = scores(q4t, c + 1, m_b, 1)
            weighted(c, m_a, m_b, 0)
            m_d = scores(q4t, c + 2, m_c, 0)
            weighted(c + 1, m_b, m_c, 1)
            return m_c, m_d

        m_a, m_b = lax.fori_loop(0, nk // 2 - 1, pair, (m_init, m_first))
        m_c = scores(q4t, nk - 1, m_b, 1)
        weighted(nk - 2, m_a, m_b, 0)
        m_next = scores(load_q(jnp.minimum(i + 1, nq - 1)), 0, m_init, 0)
        weighted(nk - 1, m_b, m_c, 1)
        acc = acc_ref[...]
        ot = acc[0:ATT_HEAD_DIM, :] / acc[ATT_HEAD_DIM:ATT_HEAD_DIM + 1, :]
        o = jnp.concatenate([jnp.transpose(ot[:, tq * j:tq * (j + 1)]) for j in range(ATT_GROUP)], axis=1)
        rows = pl.ds(pl.multiple_of(i * tq, tq), tq)
        o_ref[rows, :] = (o * _silu(z_ref[rows, :].astype(F32))).astype(BF16)
        return m_next

    lax.fori_loop(0, nq, tile, scores(load_q(0), 0, m_init, 0))


def _attn(qbt, kb, vbt, zb, batch, seq_len):
    n = zb.shape[0]
    t = seq_len
    tq = ATT_Q_ROWS
    width = ATT_GROUP * ATT_HEAD_DIM
    ospec = pl.BlockSpec((t, width), lambda b, g: (b, g))
    block_bytes = (3 * _nbytes((t, width), BF16) + _nbytes((t, V7X_LANES), BF16)
                   + _nbytes((2 * ATT_HEAD_DIM, t), BF16))
    scratch_bytes = (_nbytes((2, ATT_KEY_CHUNK, ATT_GROUP * tq), F32)
                     + _nbytes((2 * ATT_HEAD_DIM, ATT_GROUP * tq), F32))
    return pl.pallas_call(
        _attn_body,
        grid=(batch, ATT_KV_HEADS),
        in_specs=[pl.BlockSpec((width, t), lambda b, g: (g, b)),
                  pl.BlockSpec((1, t, ATT_HEAD_DIM), lambda b, g: (g, b, 0)),
                  pl.BlockSpec((2 * ATT_HEAD_DIM, t), lambda b, g: (g, b)),
                  ospec],
        out_specs=ospec,
        out_shape=jax.ShapeDtypeStruct((n, ATT_Q_DIM), BF16),
        scratch_shapes=[pltpu.VMEM((2, ATT_KEY_CHUNK, ATT_GROUP * tq), F32),
                        pltpu.VMEM((2 * ATT_HEAD_DIM, ATT_GROUP * tq), F32)],
        compiler_params=pltpu.CompilerParams(
            dimension_semantics=("arbitrary", "arbitrary"),
            vmem_limit_bytes=_vmem_limit(block_bytes, scratch_bytes)),
        name="attn",
    )(qbt, kb, vbt, zb)


def _outproj_body(oa_ref, ob_ref, ma_ref, mb_ref, x_ref, wa_ref, wb_ref, wo_ref, fg_ref, o_ref, *, final):
    ya = jnp.dot(oa_ref[...], wa_ref[...], preferred_element_type=F32)
    yb = jnp.dot(ob_ref[...], wb_ref[...], preferred_element_type=F32)
    merged = (jax.nn.sigmoid(ma_ref[...].astype(F32)) * ya
              + jax.nn.sigmoid(mb_ref[...].astype(F32)) * yb)
    y = x_ref[...] + jnp.dot(merged.astype(BF16), wo_ref[...], preferred_element_type=F32)
    if final:
        ms = jnp.mean(y * y, axis=-1, keepdims=True)
        y = y * lax.rsqrt(ms + EPS) * fg_ref[...]
    o_ref[...] = y


def _outproj(oa, ob, ma, mb, x, lw, final_gain, final):
    n = x.shape[0]
    tm = OUTPROJ_ROWS
    row = lambda i: (i, 0)
    const = lambda i: (0, 0)
    act = pl.BlockSpec((tm, D_MODEL), row)
    wspec = pl.BlockSpec((D_MODEL, D_MODEL), const, pipeline_mode=pl.Buffered(1))
    block_bytes = 4 * _nbytes((tm, D_MODEL), BF16) + 2 * _nbytes((tm, D_MODEL), F32)
    weight_bytes = 3 * _nbytes((D_MODEL, D_MODEL), BF16)
    return pl.pallas_call(
        functools.partial(_outproj_body, final=final),
        grid=(n // tm,),
        in_specs=[act, act, act, act, act, wspec, wspec, wspec,
                  pl.BlockSpec((1, D_MODEL), const, pipeline_mode=pl.Buffered(1))],
        out_specs=act,
        out_shape=jax.ShapeDtypeStruct((n, D_MODEL), F32),
        compiler_params=pltpu.CompilerParams(
            dimension_semantics=("arbitrary",),
            vmem_limit_bytes=_vmem_limit(block_bytes, weight_bytes)),
        name="outproj",
    )(oa, ob, ma, mb, x, lw["w_a"], lw["w_b"], lw["w_o"], final_gain)


def _rope_tables(seq_len):
    rows = seq_len // GRID_W
    r = jnp.repeat(jnp.arange(rows, dtype=F32), GRID_W)
    c = jnp.tile(jnp.arange(GRID_W, dtype=F32), rows)
    nf = ROPE_AXIS_DIM // 2
    inv = ROPE_THETA ** (-jnp.arange(nf, dtype=F32) / nf)
    ang_r = r[:, None] * inv
    ang_c = c[:, None] * inv
    cos = jnp.concatenate([jnp.cos(ang_r), jnp.cos(ang_r), jnp.cos(ang_c), jnp.cos(ang_c)], axis=-1)
    sin = jnp.concatenate([-jnp.sin(ang_r), jnp.sin(ang_r), -jnp.sin(ang_c), jnp.sin(ang_c)], axis=-1)
    reps = V7X_LANES // ATT_HEAD_DIM
    return jnp.tile(cos, (1, reps)), jnp.tile(sin, (1, reps))


def _layer_weights(l, norm_g, w_in, w_gate_f, b_gate_f, w_gate_b, b_gate_b, gla_norm_g, q_norm_g, k_norm_g,
                   w_branch_a, w_branch_b, w_out):
    pts = np.cumsum((0,) + IN_SIZES)
    col = lambda i: w_in[l][:, pts[i]:pts[i + 1]]
    w = w_in[l]
    w_gate = jnp.zeros((D_MODEL, V7X_LANES), F32).at[:, :2 * GATE_RANK].set(w[:, pts[3]:pts[5]])
    w_gf = jnp.zeros((V7X_LANES, GLA_KEY_DIM), F32).at[:GATE_RANK].set(w_gate_f[l])
    w_gb = jnp.zeros((V7X_LANES, GLA_KEY_DIM), F32).at[GATE_RANK:2 * GATE_RANK].set(w_gate_b[l])
    return {
        "norm_g": norm_g[l].reshape(1, D_MODEL),
        "w_gla": jnp.concatenate([col(0), col(1), col(2), col(5)], axis=1).astype(BF16),
        "w_gate": w_gate.astype(BF16),
        "w_gf": w_gf.astype(BF16),
        "w_gb": w_gb.astype(BF16),
        "b_gf": b_gate_f[l].reshape(1, GLA_KEY_DIM),
        "b_gb": b_gate_b[l].reshape(1, GLA_KEY_DIM),
        "w_att": jnp.concatenate([col(6), col(7), col(8), col(9)], axis=1).astype(BF16),
        "w_m": jnp.concatenate([col(10), col(11)], axis=1).astype(BF16),
        "q_gain": jnp.tile(q_norm_g[l], ATT_Q_HEADS).reshape(1, ATT_Q_DIM),
        "k_gain": jnp.tile(k_norm_g[l], ATT_KV_HEADS).reshape(1, ATT_KV_DIM),
        "gla_gain": gla_norm_g[l].reshape(1, GLA_DV),
        "w_a": w_branch_a[l].astype(BF16),
        "w_b": w_branch_b[l].astype(BF16),
        "w_o": w_out[l].astype(BF16),
    }


def _trunk(x3, layers, final_gain, masks, sgn, bd):
    batch, seq_len, _ = x3.shape
    x = x3.reshape(batch * seq_len, D_MODEL)
    cos, sin = _rope_tables(seq_len)
    for l, lw in enumerate(layers):
        qa, ka, va, za, gf, gb, qbt, kb, vbt, zb, ma, mb = _inproj(x, lw, cos, sin, bd, seq_len)
        oa = _gla(qa, ka, va, gf, gb, za, lw["gla_gain"], masks, sgn, batch, seq_len)
        ob = _attn(qbt, kb, vbt, zb, batch, seq_len)
        x = _outproj(oa, ob, ma, mb, x, lw, final_gain, final=(l == len(layers) - 1))
    return x.reshape(batch, seq_len, D_MODEL)


def kernel(x_prompt, x_sample, norm_g, w_in, w_gate_f, b_gate_f, w_gate_b, b_gate_b, gla_norm_g, q_norm_g,
           k_norm_g, w_branch_a, w_branch_b, w_out, final_norm_g):
    layers = [_layer_weights(l, norm_g, w_in, w_gate_f, b_gate_f, w_gate_b, b_gate_b, gla_norm_g, q_norm_g,
                             k_norm_g, w_branch_a, w_branch_b, w_out) for l in range(DEPTH)]
    final_gain = final_norm_g.reshape(1, D_MODEL)
    masks, sgn = _gla_masks()
    head = np.arange(V7X_MXU_DIM) // ATT_HEAD_DIM
    bd = jnp.asarray((head[:, None] == head[None, :]).astype(np.float32), dtype=BF16)
    y_prompt = _trunk(x_prompt, layers, final_gain, masks, sgn, bd)
    y_sample = _trunk(x_sample, layers, final_gain, masks, sgn, bd)
    return (y_prompt, y_sample)
```

```python
import functools

import numpy as np
import jax
import jax.numpy as jnp
from jax import lax
from jax.experimental import pallas as pl
from jax.experimental.pallas import tpu as pltpu

F32, BF16 = jnp.float32, jnp.bfloat16

D_MODEL = 1024
DEPTH = 4
EPS = 1e-6
GRID_W = 64
GLA_HEADS = 4
GLA_KEY_DIM = D_MODEL // 2
GLA_VALUE_DIM = D_MODEL
GLA_DK = GLA_KEY_DIM // GLA_HEADS
GLA_DV = GLA_VALUE_DIM // GLA_HEADS
GATE_RANK = 16
GATE_NORMALIZER = 16.0
ATT_HEAD_DIM = 64
ATT_Q_HEADS = D_MODEL // ATT_HEAD_DIM
ATT_KV_HEADS = 4
ATT_GROUP = ATT_Q_HEADS // ATT_KV_HEADS
ATT_Q_DIM = ATT_Q_HEADS * ATT_HEAD_DIM
ATT_KV_DIM = ATT_KV_HEADS * ATT_HEAD_DIM
ROPE_AXIS_DIM = ATT_HEAD_DIM // 2
ROPE_THETA = 10000.0
IN_SIZES = (GLA_KEY_DIM, GLA_KEY_DIM, GLA_VALUE_DIM, GATE_RANK, GATE_RANK, GLA_VALUE_DIM,
            ATT_Q_DIM, ATT_KV_DIM, ATT_KV_DIM, ATT_Q_DIM, D_MODEL, D_MODEL)

V7X_LANES = 128
V7X_SUBLANES = 8
V7X_MXU_DIM = 256
V7X_VMEM_BYTES = 64 * 1024 * 1024
V7X_VMEM_BUDGET = V7X_VMEM_BYTES - 8 * 1024 * 1024

INPROJ_ROWS = 512
OUTPROJ_ROWS = 512
GLA_CHUNK = 64
GLA_LEVELS = (1, 2, 4, 8, 16, 32)
GLA_CHUNKS_PER_STEP = 8
GLA_FINAL_ROWS = 256
GLA_BOUNDED_DECAY = 40.0
GLA_MASK_DIAG = 0
GLA_MASK_TREE_FWD = 1
GLA_MASK_TREE_BWD = GLA_MASK_TREE_FWD + len(GLA_LEVELS)
GLA_MASK_TRI_FWD = GLA_MASK_TREE_BWD + len(GLA_LEVELS)
GLA_MASK_TRI_BWD = GLA_MASK_TRI_FWD + 1
ATT_Q_ROWS = 256
ATT_KEY_CHUNK = 512

_NT = (((1,), (1,)), ((), ()))
_TN = (((0,), (0,)), ((), ()))


def _vmem_limit(block_bytes, scratch_bytes=0):
    want = 2 * block_bytes + scratch_bytes + 16 * 1024 * 1024
    return int(min(want, V7X_VMEM_BUDGET))


def _nbytes(shape, dtype):
    return int(np.prod(shape)) * jnp.dtype(dtype).itemsize


def _log_sigmoid(z):
    return -(jnp.maximum(-z, 0.0) + jnp.log1p(jnp.exp(-jnp.abs(z))))


def _silu(z):
    return z * jax.nn.sigmoid(z)


def _inproj_body(x_ref, ng_ref, wgla_ref, wgate_ref, wgf_ref, wgb_ref, bgf_ref, bgb_ref,
                 watt_ref, wm_ref, qng_ref, kng_ref, cos_ref, sin_ref, bd_ref,
                 qa_ref, ka_ref, va_ref, za_ref, gf_ref, gb_ref,
                 qbt_ref, kb_ref, vbt_ref, zb_ref, ma_ref, mb_ref):
    x = x_ref[...]
    ms = jnp.mean(x * x, axis=-1, keepdims=True)
    h = (x * lax.rsqrt(ms + EPS) * ng_ref[...]).astype(BF16)

    def proj(w_ref, lo, hi):
        return jnp.dot(h, w_ref[:, lo:hi], preferred_element_type=F32)

    qa_ref[...] = (proj(wgla_ref, 0, 512) * (GLA_DK ** -0.5)).astype(BF16)
    ka_ref[...] = proj(wgla_ref, 512, 1024).astype(BF16)
    va_ref[...] = proj(wgla_ref, 1024, 2048).astype(BF16)
    za_ref[...] = proj(wgla_ref, 2048, 3072).astype(BF16)

    pg = proj(wgate_ref, 0, V7X_LANES).astype(BF16)
    zf = jnp.dot(pg, wgf_ref[...], preferred_element_type=F32) + bgf_ref[...]
    zb = jnp.dot(pg, wgb_ref[...], preferred_element_type=F32) + bgb_ref[...]
    gf_ref[...] = _log_sigmoid(zf) * (1.0 / GATE_NORMALIZER)
    gb_ref[...] = _log_sigmoid(zb) * (1.0 / GATE_NORMALIZER)

    cos = cos_ref[...]
    sin = sin_ref[...]
    bd = bd_ref[...]
    lane = lax.broadcasted_iota(jnp.int32, cos.shape, 1)
    first_half = (lane & (ROPE_AXIS_DIM // 2)) == 0

    def norm_rope(p, gain):
        sq = (p * p).astype(BF16)
        msq = jnp.dot(sq, bd, preferred_element_type=F32) * (1.0 / ATT_HEAD_DIM)
        r = lax.rsqrt(msq + EPS)
        xg = p * gain
        outs = []
        for s in range(2):
            xs = xg[:, V7X_LANES * s:V7X_LANES * (s + 1)]
            partner = jnp.where(first_half,
                                pltpu.roll(xs, V7X_LANES - ROPE_AXIS_DIM // 2, 1),
                                pltpu.roll(xs, ROPE_AXIS_DIM // 2, 1))
            outs.append(xs * cos + partner * sin)
        return jnp.concatenate(outs, axis=1) * r

    for c in range(ATT_Q_DIM // V7X_MXU_DIM):
        lo = V7X_MXU_DIM * c
        p = proj(watt_ref, lo, lo + V7X_MXU_DIM)
        y = norm_rope(p, qng_ref[:, lo:lo + V7X_MXU_DIM]) * (ATT_HEAD_DIM ** -0.5)
        qbt_ref[lo:lo + V7X_MXU_DIM, :] = jnp.transpose(y).astype(BF16)
    kk = norm_rope(proj(watt_ref, 1024, 1280), kng_ref[...]).astype(BF16)
    vt = jnp.transpose(proj(watt_ref, 1280, 1536)).astype(BF16)
    ones = jnp.ones((ATT_HEAD_DIM, vt.shape[1]), BF16)
    for g in range(ATT_KV_HEADS):
        kb_ref[g] = kk[:, ATT_HEAD_DIM * g:ATT_HEAD_DIM * (g + 1)]
        vbt_ref[2 * ATT_HEAD_DIM * g:2 * ATT_HEAD_DIM * g + ATT_HEAD_DIM, :] = vt[ATT_HEAD_DIM * g:ATT_HEAD_DIM * (g + 1), :]
        vbt_ref[2 * ATT_HEAD_DIM * g + ATT_HEAD_DIM:2 * ATT_HEAD_DIM * (g + 1), :] = ones
    zb_ref[...] = proj(watt_ref, 1536, 2560).astype(BF16)

    ma_ref[...] = proj(wm_ref, 0, 1024).astype(BF16)
    mb_ref[...] = proj(wm_ref, 1024, 2048).astype(BF16)


def _inproj(x, lw, cos, sin, bd, seq_len):
    n = x.shape[0]
    tm = INPROJ_ROWS
    pos_blocks = seq_len // tm
    row = lambda i: (i, 0)
    const = lambda i: (0, 0)

    def wspec(shape):
        return pl.BlockSpec(shape, const, pipeline_mode=pl.Buffered(1))

    in_specs = [
        pl.BlockSpec((tm, D_MODEL), row),
        wspec((1, D_MODEL)),
        wspec((D_MODEL, 3072)),
        wspec((D_MODEL, V7X_LANES)),
        wspec((V7X_LANES, GLA_KEY_DIM)),
        wspec((V7X_LANES, GLA_KEY_DIM)),
        wspec((1, GLA_KEY_DIM)),
        wspec((1, GLA_KEY_DIM)),
        wspec((D_MODEL, 2560)),
        wspec((D_MODEL, 2048)),
        wspec((1, ATT_Q_DIM)),
        wspec((1, ATT_KV_DIM)),
        pl.BlockSpec((tm, V7X_LANES), lambda i: (i % pos_blocks, 0)),
        pl.BlockSpec((tm, V7X_LANES), lambda i: (i % pos_blocks, 0)),
        wspec((V7X_MXU_DIM, V7X_MXU_DIM)),
    ]
    out_shape = [
        jax.ShapeDtypeStruct((n, GLA_KEY_DIM), BF16),
        jax.ShapeDtypeStruct((n, GLA_KEY_DIM), BF16),
        jax.ShapeDtypeStruct((n, GLA_VALUE_DIM), BF16),
        jax.ShapeDtypeStruct((n, GLA_VALUE_DIM), BF16),
        jax.ShapeDtypeStruct((n, GLA_KEY_DIM), F32),
        jax.ShapeDtypeStruct((n, GLA_KEY_DIM), F32),
        jax.ShapeDtypeStruct((ATT_Q_DIM, n), BF16),
        jax.ShapeDtypeStruct((ATT_KV_HEADS, n, ATT_HEAD_DIM), BF16),
        jax.ShapeDtypeStruct((2 * ATT_KV_DIM, n), BF16),
        jax.ShapeDtypeStruct((n, ATT_Q_DIM), BF16),
        jax.ShapeDtypeStruct((n, D_MODEL), BF16),
        jax.ShapeDtypeStruct((n, D_MODEL), BF16),
    ]
    out_specs = [
        pl.BlockSpec((tm, GLA_KEY_DIM), row),
        pl.BlockSpec((tm, GLA_KEY_DIM), row),
        pl.BlockSpec((tm, GLA_VALUE_DIM), row),
        pl.BlockSpec((tm, GLA_VALUE_DIM), row),
        pl.BlockSpec((tm, GLA_KEY_DIM), row),
        pl.BlockSpec((tm, GLA_KEY_DIM), row),
        pl.BlockSpec((ATT_Q_DIM, tm), lambda i: (0, i)),
        pl.BlockSpec((ATT_KV_HEADS, tm, ATT_HEAD_DIM), lambda i: (0, i, 0)),
        pl.BlockSpec((2 * ATT_KV_DIM, tm), lambda i: (0, i)),
        pl.BlockSpec((tm, ATT_Q_DIM), row),
        pl.BlockSpec((tm, D_MODEL), row),
        pl.BlockSpec((tm, D_MODEL), row),
    ]
    weight_bytes = _nbytes((D_MODEL, 3072 + V7X_LANES + 2560 + 2048), BF16)
    block_bytes = (_nbytes((tm, D_MODEL), F32) + _nbytes((tm, 7 * 1024 + 512), BF16)
                   + 2 * _nbytes((tm, GLA_KEY_DIM), F32) + 2 * _nbytes((tm, V7X_LANES), F32))
    return pl.pallas_call(
        _inproj_body,
        grid=(n // tm,),
        in_specs=in_specs,
        out_specs=out_specs,
        out_shape=out_shape,
        compiler_params=pltpu.CompilerParams(
            dimension_semantics=("arbitrary",),
            vmem_limit_bytes=_vmem_limit(block_bytes, weight_bytes)),
        name="inproj",
    )(x, lw["norm_g"], lw["w_gla"], lw["w_gate"], lw["w_gf"], lw["w_gb"], lw["b_gf"], lw["b_gb"],
      lw["w_att"], lw["w_m"], lw["q_gain"], lw["k_gain"], cos, sin, bd)


def _gla_boundary(beta, m, reverse):
    c, width = beta.shape
    two_m = 2 * m
    off = m if reverse else m - 1
    pieces = []
    if two_m >= V7X_SUBLANES:
        for v in range(c // V7X_SUBLANES):
            r = (V7X_SUBLANES * v // two_m) * two_m + off
            pieces.append(jnp.broadcast_to(beta[r:r + 1, :], (V7X_SUBLANES, width)))
    else:
        sub = lax.broadcasted_iota(jnp.int32, (V7X_SUBLANES, width), 0)
        per = V7X_SUBLANES // two_m
        for v in range(c // V7X_SUBLANES):
            rows = [jnp.broadcast_to(beta[V7X_SUBLANES * v + two_m * j + off:V7X_SUBLANES * v + two_m * j + off + 1, :],
                                     (V7X_SUBLANES, width)) for j in range(per)]
            acc = rows[-1]
            for j in range(per - 2, -1, -1):
                acc = jnp.where(sub < two_m * (j + 1), rows[j], acc)
            pieces.append(acc)
    return jnp.concatenate(pieces, axis=0)


def _gla_tree_scores(q, k, g, beta, mask_ref, sgn_ref, reverse):
    row = lax.broadcasted_iota(jnp.int32, g.shape, 0)
    nlev = len(GLA_LEVELS)
    mask_base = GLA_MASK_TREE_BWD if reverse else GLA_MASK_TREE_FWD
    sgn_base = (nlev - 1) if reverse else 0
    a = None
    if not reverse:
        a = mask_ref[GLA_MASK_DIAG] * lax.dot_general(q, k, _NT, preferred_element_type=F32)
    for li, m in enumerate(GLA_LEVELS):
        if m == 1:
            parity = (row & 1) == (0 if reverse else 1)
            f = jnp.exp(jnp.where(parity, g, 0.0))
        else:
            f = jnp.exp2((beta - _gla_boundary(beta, m, reverse)) * sgn_ref[sgn_base + li - 1])
        fb = f.astype(BF16)
        am = mask_ref[mask_base + li] * lax.dot_general(q * fb, k * fb, _NT, preferred_element_type=F32)
        a = am if a is None else a + am
    return a


def _gla_chunk_local(q, k, v, g, mask_ref, sgn_ref, reverse, bounded):
    c = GLA_CHUNK
    row = lax.broadcasted_iota(jnp.int32, g.shape, 0)
    beta = g
    for sh in GLA_LEVELS:
        if reverse:
            beta = beta + jnp.where(row < c - sh, pltpu.roll(beta, c - sh, 0), 0.0)
        else:
            beta = beta + jnp.where(row >= sh, pltpu.roll(beta, sh, 0), 0.0)
    qd = q * jnp.exp(beta).astype(BF16)
    tot = beta[0:1, :] if reverse else beta[c - 1:c, :]
    kd = k * jnp.exp(tot - beta).astype(BF16)
    decay_col = jnp.transpose(jnp.broadcast_to(jnp.exp(tot), (V7X_SUBLANES, GLA_DK)))[:, 0:1]
    if bounded:
        kg = k * jnp.exp(-beta).astype(BF16)
        a = (mask_ref[GLA_MASK_TRI_BWD if reverse else GLA_MASK_TRI_FWD]
             * lax.dot_general(qd, kg, _NT, preferred_element_type=F32))
    else:
        a = _gla_tree_scores(q, k, g, beta, mask_ref, sgn_ref, reverse)
    kv = lax.dot_general(kd, v, _TN, preferred_element_type=F32)
    return qd, a.astype(BF16), kv, decay_col


def _gla_chunk_state(local, v, s_ref):
    qd, a, kv, decay_col = local
    s = s_ref[...]
    o = jnp.dot(qd, s.astype(BF16), preferred_element_type=F32)
    s_ref[...] = decay_col * s + kv
    return o + jnp.dot(a, v, preferred_element_type=F32)


def _gla_body(q_ref, k_ref, v_ref, gf_ref, gb_ref, za_ref, gn_ref, mask_ref, sgn_ref, o_ref,
              acc_ref, sf_ref, sb_ref):
    t = q_ref.shape[0]
    c = GLA_CHUNK
    n = t // c
    acc_ref[...] = jnp.zeros_like(acc_ref)
    sf_ref[...] = jnp.zeros_like(sf_ref)
    sb_ref[...] = jnp.zeros_like(sb_ref)

    chunk_tot = jnp.minimum(jnp.sum(gf_ref[...].reshape(n, c, GLA_DK), axis=1),
                            jnp.sum(gb_ref[...].reshape(n, c, GLA_DK), axis=1))
    bounded = jnp.min(chunk_tot) >= -GLA_BOUNDED_DECAY

    def scan(is_bounded):
        def step(i, carry):
            work = []
            for u in range(GLA_CHUNKS_PER_STEP):
                j = i * GLA_CHUNKS_PER_STEP + u
                for reverse, g_ref, s_ref in ((False, gf_ref, sf_ref), (True, gb_ref, sb_ref)):
                    sl = pl.ds(pl.multiple_of(((n - 1 - j) if reverse else j) * c, c), c)
                    local = _gla_chunk_local(q_ref[sl, :], k_ref[sl, :], v_ref[sl, :], g_ref[sl, :],
                                             mask_ref, sgn_ref, reverse, is_bounded)
                    work.append((sl, local, s_ref))
            for sl, local, s_ref in work:
                acc_ref[sl, :] += _gla_chunk_state(local, v_ref[sl, :], s_ref)
            return carry

        lax.fori_loop(0, n // GLA_CHUNKS_PER_STEP, step, 0)

    @pl.when(bounded)
    def _():
        scan(True)

    @pl.when(jnp.logical_not(bounded))
    def _():
        scan(False)

    rows = GLA_FINAL_ROWS

    def fin(i, carry):
        sl = pl.ds(pl.multiple_of(i * rows, rows), rows)
        o = acc_ref[sl, :]
        ms = jnp.mean(o * o, axis=-1, keepdims=True)
        y = o * lax.rsqrt(ms + EPS) * gn_ref[...]
        o_ref[sl, :] = (y * _silu(za_ref[sl, :].astype(F32))).astype(BF16)
        return carry

    lax.fori_loop(0, t // rows, fin, 0)


def _gla_masks():
    c = GLA_CHUNK
    ri = np.arange(c)[:, None]
    ci = np.arange(c)[None, :]
    masks = [ri == ci]
    for m in GLA_LEVELS:
        same = (ri // (2 * m)) == (ci // (2 * m))
        masks.append(same & ((ri % (2 * m)) >= m) & ((ci % (2 * m)) < m))
    for m in GLA_LEVELS:
        same = (ri // (2 * m)) == (ci // (2 * m))
        masks.append(same & ((ri % (2 * m)) < m) & ((ci % (2 * m)) >= m))
    masks.append(ri >= ci)
    masks.append(ri < ci)
    sgn = []
    row = np.arange(c)[:, None] * np.ones((1, GLA_DK), np.int64)
    for reverse in (False, True):
        for m in GLA_LEVELS[1:]:
            upper = (row % (2 * m)) >= m
            sgn.append(np.where(upper != reverse, np.log2(np.e), -np.log2(np.e)))
    return jnp.asarray(np.stack(masks).astype(np.float32)), jnp.asarray(np.stack(sgn).astype(np.float32))


def _gla(qa, ka, va, gf, gb, za, gn, masks, sgn, batch, seq_len):
    n = qa.shape[0]
    t = seq_len
    assert t % (GLA_CHUNK * GLA_CHUNKS_PER_STEP) == 0 and t % GLA_FINAL_ROWS == 0, t
    kspec = pl.BlockSpec((t, GLA_DK), lambda b, h: (b, h))
    vspec = pl.BlockSpec((t, GLA_DV), lambda b, h: (b, h))
    nm = masks.shape[0]
    block_bytes = (2 * _nbytes((t, GLA_DK), BF16) + 3 * _nbytes((t, GLA_DV), BF16)
                   + 2 * _nbytes((t, GLA_DK), F32) + _nbytes(masks.shape, F32) + _nbytes(sgn.shape, F32))
    scratch_bytes = _nbytes((t, GLA_DV), F32) + 2 * _nbytes((GLA_DK, GLA_DV), F32)
    return pl.pallas_call(
        _gla_body,
        grid=(batch, GLA_HEADS),
        in_specs=[kspec, kspec, vspec, kspec, kspec, vspec,
                  pl.BlockSpec((1, GLA_DV), lambda b, h: (0, 0)),
                  pl.BlockSpec((nm, GLA_CHUNK, GLA_CHUNK), lambda b, h: (0, 0, 0)),
                  pl.BlockSpec(sgn.shape, lambda b, h: (0, 0, 0))],
        out_specs=vspec,
        out_shape=jax.ShapeDtypeStruct((n, GLA_VALUE_DIM), BF16),
        scratch_shapes=[pltpu.VMEM((t, GLA_DV), F32),
                        pltpu.VMEM((GLA_DK, GLA_DV), F32),
                        pltpu.VMEM((GLA_DK, GLA_DV), F32)],
        compiler_params=pltpu.CompilerParams(
            dimension_semantics=("arbitrary", "arbitrary"),
            vmem_limit_bytes=_vmem_limit(block_bytes, scratch_bytes)),
        name="gla",
    )(qa, ka, va, gf, gb, za, gn, masks, sgn)


def _attn_body(qt_ref, k_ref, vt_ref, z_ref, o_ref, s_ref, acc_ref):
    t = k_ref.shape[1]
    tq = ATT_Q_ROWS
    nq = t // tq
    kc = ATT_KEY_CHUNK
    nk = t // kc
    cols = ATT_GROUP * tq

    def load_q(i):
        qt = qt_ref[:, pl.ds(pl.multiple_of(i * tq, tq), tq)]
        return jnp.concatenate([qt[ATT_HEAD_DIM * j:ATT_HEAD_DIM * (j + 1), :] for j in range(ATT_GROUP)], axis=1)

    def scores(q4t, c, m, buf):
        sl = pl.ds(pl.multiple_of(c * kc, kc), kc)
        s = jnp.dot(k_ref[0, sl, :], q4t, preferred_element_type=F32)
        s_ref[buf] = s
        return jnp.maximum(m, jnp.max(s, axis=0, keepdims=True))

    def weighted(c, m_old, m_new, buf):
        sl = pl.ds(pl.multiple_of(c * kc, kc), kc)
        p = jnp.exp(s_ref[buf] - m_new).astype(BF16)
        acc_ref[...] = (jnp.exp(m_old - m_new) * acc_ref[...]
                        + jnp.dot(vt_ref[:, sl], p, preferred_element_type=F32))

    m_init = jnp.full((1, cols), -jnp.inf, F32)

    def tile(i, m_first):
        q4t = load_q(i)
        acc_ref[...] = jnp.zeros_like(acc_ref)

        def pair(j, carry):
            m_a, m_b = carry
            c = 2 * j
            m_c = scores(q4t, c + 1, m_b, 1)
            weighted(c, m_a, m_b, 0)
            m_d = scores(q4t, c + 2, m_c, 0)
            weighted(c + 1, m_b, m_c, 1)
            return m_c, m_d

        m_a, m_b = lax.fori_loop(0, nk // 2 - 1, pair, (m_init, m_first))
        m_c = scores(q4t, nk - 1, m_b, 1)
        weighted(nk - 2, m_a, m_b, 0)
        m_next = scores(load_q(jnp.minimum(i + 1, nq - 1)), 0, m_init, 0)
        weighted(nk - 1, m_b, m_c, 1)
        acc = acc_ref[...]
        ot = acc[0:ATT_HEAD_DIM, :] / acc[ATT_HEAD_DIM:ATT_HEAD_DIM + 1, :]
        o = jnp.concatenate([jnp.transpose(ot[:, tq * j:tq * (j + 1)]) for j in range(ATT_GROUP)], axis=1)
        rows = pl.ds(pl.multiple_of(i * tq, tq), tq)
        o_ref[rows, :] = (o * _silu(z_ref[rows, :].astype(F32))).astype(BF16)
        return m_next

    lax.fori_loop(0, nq, tile, scores(load_q(0), 0, m_init, 0))


def _attn(qbt, kb, vbt, zb, batch, seq_len):
    n = zb.shape[0]
    t = seq_len
    tq = ATT_Q_ROWS
    width = ATT_GROUP * ATT_HEAD_DIM
    ospec = pl.BlockSpec((t, width), lambda b, g: (b, g))
    block_bytes = (3 * _nbytes((t, width), BF16) + _nbytes((t, V7X_LANES), BF16)
                   + _nbytes((2 * ATT_HEAD_DIM, t), BF16))
    scratch_bytes = (_nbytes((2, ATT_KEY_CHUNK, ATT_GROUP * tq), F32)
                     + _nbytes((2 * ATT_HEAD_DIM, ATT_GROUP * tq), F32))
    return pl.pallas_call(
        _attn_body,
        grid=(batch, ATT_KV_HEADS),
        in_specs=[pl.BlockSpec((width, t), lambda b, g: (g, b)),
                  pl.BlockSpec((1, t, ATT_HEAD_DIM), lambda b, g: (g, b, 0)),
                  pl.BlockSpec((2 * ATT_HEAD_DIM, t), lambda b, g: (g, b)),
                  ospec],
        out_specs=ospec,
        out_shape=jax.ShapeDtypeStruct((n, ATT_Q_DIM), BF16),
        scratch_shapes=[pltpu.VMEM((2, ATT_KEY_CHUNK, ATT_GROUP * tq), F32),
                        pltpu.VMEM((2 * ATT_HEAD_DIM, ATT_GROUP * tq), F32)],
        compiler_params=pltpu.CompilerParams(
            dimension_semantics=("arbitrary", "arbitrary"),
            vmem_limit_bytes=_vmem_limit(block_bytes, scratch_bytes)),
        name="attn",
    )(qbt, kb, vbt, zb)


def _outproj_body(oa_ref, ob_ref, ma_ref, mb_ref, x_ref, wa_ref, wb_ref, wo_ref, fg_ref, o_ref, *, final):
    ya = jnp.dot(oa_ref[...], wa_ref[...], preferred_element_type=F32)
    yb = jnp.dot(ob_ref[...], wb_ref[...], preferred_element_type=F32)
    merged = (jax.nn.sigmoid(ma_ref[...].astype(F32)) * ya
              + jax.nn.sigmoid(mb_ref[...].astype(F32)) * yb)
    y = x_ref[...] + jnp.dot(merged.astype(BF16), wo_ref[...], preferred_element_type=F32)
    if final:
        ms = jnp.mean(y * y, axis=-1, keepdims=True)
        y = y * lax.rsqrt(ms + EPS) * fg_ref[...]
    o_ref[...] = y


def _outproj(oa, ob, ma, mb, x, lw, final_gain, final):
    n = x.shape[0]
    tm = OUTPROJ_ROWS
    row = lambda i: (i, 0)
    const = lambda i: (0, 0)
    act = pl.BlockSpec((tm, D_MODEL), row)
    wspec = pl.BlockSpec((D_MODEL, D_MODEL), const, pipeline_mode=pl.Buffered(1))
    block_bytes = 4 * _nbytes((tm, D_MODEL), BF16) + 2 * _nbytes((tm, D_MODEL), F32)
    weight_bytes = 3 * _nbytes((D_MODEL, D_MODEL), BF16)
    return pl.pallas_call(
        functools.partial(_outproj_body, final=final),
        grid=(n // tm,),
        in_specs=[act, act, act, act, act, wspec, wspec, wspec,
                  pl.BlockSpec((1, D_MODEL), const, pipeline_mode=pl.Buffered(1))],
        out_specs=act,
        out_shape=jax.ShapeDtypeStruct((n, D_MODEL), F32),
        compiler_params=pltpu.CompilerParams(
            dimension_semantics=("arbitrary",),
            vmem_limit_bytes=_vmem_limit(block_bytes, weight_bytes)),
        name="outproj",
    )(oa, ob, ma, mb, x, lw["w_a"], lw["w_b"], lw["w_o"], final_gain)


def _rope_tables(seq_len):
    rows = seq_len // GRID_W
    r = jnp.repeat(jnp.arange(rows, dtype=F32), GRID_W)
    c = jnp.tile(jnp.arange(GRID_W, dtype=F32), rows)
    nf = ROPE_AXIS_DIM // 2
    inv = ROPE_THETA ** (-jnp.arange(nf, dtype=F32) / nf)
    ang_r = r[:, None] * inv
    ang_c = c[:, None] * inv
    cos = jnp.concatenate([jnp.cos(ang_r), jnp.cos(ang_r), jnp.cos(ang_c), jnp.cos(ang_c)], axis=-1)
    sin = jnp.concatenate([-jnp.sin(ang_r), jnp.sin(ang_r), -jnp.sin(ang_c), jnp.sin(ang_c)], axis=-1)
    reps = V7X_LANES // ATT_HEAD_DIM
    return jnp.tile(cos, (1, reps)), jnp.tile(sin, (1, reps))


def _layer_weights(l, norm_g, w_in, w_gate_f, b_gate_f, w_gate_b, b_gate_b, gla_norm_g, q_norm_g, k_norm_g,
                   w_branch_a, w_branch_b, w_out):
    pts = np.cumsum((0,) + IN_SIZES)
    col = lambda i: w_in[l][:, pts[i]:pts[i + 1]]
    w = w_in[l]
    w_gate = jnp.zeros((D_MODEL, V7X_LANES), F32).at[:, :2 * GATE_RANK].set(w[:, pts[3]:pts[5]])
    w_gf = jnp.zeros((V7X_LANES, GLA_KEY_DIM), F32).at[:GATE_RANK].set(w_gate_f[l])
    w_gb = jnp.zeros((V7X_LANES, GLA_KEY_DIM), F32).at[GATE_RANK:2 * GATE_RANK].set(w_gate_b[l])
    return {
        "norm_g": norm_g[l].reshape(1, D_MODEL),
        "w_gla": jnp.concatenate([col(0), col(1), col(2), col(5)], axis=1).astype(BF16),
        "w_gate": w_gate.astype(BF16),
        "w_gf": w_gf.astype(BF16),
        "w_gb": w_gb.astype(BF16),
        "b_gf": b_gate_f[l].reshape(1, GLA_KEY_DIM),
        "b_gb": b_gate_b[l].reshape(1, GLA_KEY_DIM),
        "w_att": jnp.concatenate([col(6), col(7), col(8), col(9)], axis=1).astype(BF16),
        "w_m": jnp.concatenate([col(10), col(11)], axis=1).astype(BF16),
        "q_gain": jnp.tile(q_norm_g[l], ATT_Q_HEADS).reshape(1, ATT_Q_DIM),
        "k_gain": jnp.tile(k_norm_g[l], ATT_KV_HEADS).reshape(1, ATT_KV_DIM),
        "gla_gain": gla_norm_g[l].reshape(1, GLA_DV),
        "w_a": w_branch_a[l].astype(BF16),
        "w_b": w_branch_b[l].astype(BF16),
        "w_o": w_out[l].astype(BF16),
    }


def _trunk(x3, layers, final_gain, masks, sgn, bd):
    batch, seq_len, _ = x3.shape
    x = x3.reshape(batch * seq_len, D_MODEL)
    cos, sin = _rope_tables(seq_len)
    for l, lw in enumerate(layers):
        qa, ka, va, za, gf, gb, qbt, kb, vbt, zb, ma, mb = _inproj(x, lw, cos, sin, bd, seq_len)
        oa = _gla(qa, ka, va, gf, gb, za, lw["gla_gain"], masks, sgn, batch, seq_len)
        ob = _attn(qbt, kb, vbt, zb, batch, seq_len)
        x = _outproj(oa, ob, ma, mb, x, lw, final_gain, final=(l == len(layers) - 1))
    return x.reshape(batch, seq_len, D_MODEL)


def kernel(x_prompt, x_sample, norm_g, w_in, w_gate_f, b_gate_f, w_gate_b, b_gate_b, gla_norm_g, q_norm_g,
           k_norm_g, w_branch_a, w_branch_b, w_out, final_norm_g):
    layers = [_layer_weights(l, norm_g, w_in, w_gate_f, b_gate_f, w_gate_b, b_gate_b, gla_norm_g, q_norm_g,
                             k_norm_g, w_branch_a, w_branch_b, w_out) for l in range(DEPTH)]
    final_gain = final_norm_g.reshape(1, D_MODEL)
    masks, sgn = _gla_masks()
    head = np.arange(V7X_MXU_DIM) // ATT_HEAD_DIM
    bd = jnp.asarray((head[:, None] == head[None, :]).astype(np.float32), dtype=BF16)
    y_prompt = _trunk(x_prompt, layers, final_gain, masks, sgn, bd)
    y_sample = _trunk(x_sample, layers, final_gain, masks, sgn, bd)
    return (y_prompt, y_sample)
```

```python
import functools

import numpy as np
import jax
import jax.numpy as jnp
from jax import lax
from jax.experimental import pallas as pl
from jax.experimental.pallas import tpu as pltpu

F32, BF16 = jnp.float32, jnp.bfloat16

D_MODEL = 1024
DEPTH = 4
EPS = 1e-6
GRID_W = 64
GLA_HEADS = 4
GLA_KEY_DIM = D_MODEL // 2
GLA_VALUE_DIM = D_MODEL
GLA_DK = GLA_KEY_DIM // GLA_HEADS
GLA_DV = GLA_VALUE_DIM // GLA_HEADS
GATE_RANK = 16
GATE_NORMALIZER = 16.0
ATT_HEAD_DIM = 64
ATT_Q_HEADS = D_MODEL // ATT_HEAD_DIM
ATT_KV_HEADS = 4
ATT_GROUP = ATT_Q_HEADS // ATT_KV_HEADS
ATT_Q_DIM = ATT_Q_HEADS * ATT_HEAD_DIM
ATT_KV_DIM = ATT_KV_HEADS * ATT_HEAD_DIM
ROPE_AXIS_DIM = ATT_HEAD_DIM // 2
ROPE_THETA = 10000.0
IN_SIZES = (GLA_KEY_DIM, GLA_KEY_DIM, GLA_VALUE_DIM, GATE_RANK, GATE_RANK, GLA_VALUE_DIM,
            ATT_Q_DIM, ATT_KV_DIM, ATT_KV_DIM, ATT_Q_DIM, D_MODEL, D_MODEL)

V7X_LANES = 128
V7X_SUBLANES = 8
V7X_MXU_DIM = 256
V7X_VMEM_BYTES = 64 * 1024 * 1024
V7X_VMEM_BUDGET = V7X_VMEM_BYTES - 8 * 1024 * 1024

INPROJ_ROWS = 512
OUTPROJ_ROWS = 512
GLA_CHUNK = 64
GLA_LEVELS = (1, 2, 4, 8, 16, 32)
GLA_CHUNKS_PER_STEP = 8
GLA_FINAL_ROWS = 256
GLA_BOUNDED_DECAY = 40.0
GLA_MASK_DIAG = 0
GLA_MASK_TREE_FWD = 1
GLA_MASK_TREE_BWD = GLA_MASK_TREE_FWD + len(GLA_LEVELS)
GLA_MASK_TRI_FWD = GLA_MASK_TREE_BWD + len(GLA_LEVELS)
GLA_MASK_TRI_BWD = GLA_MASK_TRI_FWD + 1
ATT_Q_ROWS = 256
ATT_KEY_CHUNK = 512

_NT = (((1,), (1,)), ((), ()))
_TN = (((0,), (0,)), ((), ()))


def _vmem_limit(block_bytes, scratch_bytes=0):
    want = 2 * block_bytes + scratch_bytes + 16 * 1024 * 1024
    return int(min(want, V7X_VMEM_BUDGET))


def _nbytes(shape, dtype):
    return int(np.prod(shape)) * jnp.dtype(dtype).itemsize


def _log_sigmoid(z):
    return jnp.minimum(z, 0.0) - jnp.log(1.0 + jnp.exp(-jnp.abs(z)))


def _silu(z):
    return z * jax.nn.sigmoid(z)


def _inproj_body(x_ref, ng_ref, wgla_ref, wgate_ref, wgf_ref, wgb_ref, bgf_ref, bgb_ref,
                 watt_ref, wm_ref, qng_ref, kng_ref, cos_ref, sin_ref, bd_ref,
                 qa_ref, ka_ref, va_ref, za_ref, gf_ref, gb_ref,
                 qbt_ref, kb_ref, vbt_ref, zb_ref, ma_ref, mb_ref):
    x = x_ref[...]
    ms = jnp.mean(x * x, axis=-1, keepdims=True)
    h = (x * lax.rsqrt(ms + EPS) * ng_ref[...]).astype(BF16)

    def proj(w_ref, lo, hi):
        return jnp.dot(h, w_ref[:, lo:hi], preferred_element_type=F32)

    pg = proj(wgate_ref, 0, V7X_LANES).astype(BF16)
    gate_rows = pg.shape[0] // 4

    def decay_gates(part):
        rows = slice(gate_rows * part, gate_rows * (part + 1))
        zf = jnp.dot(pg[rows], wgf_ref[...], preferred_element_type=F32) + bgf_ref[...]
        zb = jnp.dot(pg[rows], wgb_ref[...], preferred_element_type=F32) + bgb_ref[...]
        gf_ref[rows, :] = _log_sigmoid(zf) * (1.0 / GATE_NORMALIZER)
        gb_ref[rows, :] = _log_sigmoid(zb) * (1.0 / GATE_NORMALIZER)

    cos = cos_ref[...]
    sin = sin_ref[...]
    bd = bd_ref[...]
    lane = lax.broadcasted_iota(jnp.int32, cos.shape, 1)
    first_half = (lane & (ROPE_AXIS_DIM // 2)) == 0

    def norm_rope(p, gain):
        sq = (p * p).astype(BF16)
        msq = jnp.dot(sq, bd, preferred_element_type=F32) * (1.0 / ATT_HEAD_DIM)
        r = lax.rsqrt(msq + EPS)
        xg = p * gain
        outs = []
        for s in range(2):
            xs = xg[:, V7X_LANES * s:V7X_LANES * (s + 1)]
            partner = jnp.where(first_half,
                                pltpu.roll(xs, V7X_LANES - ROPE_AXIS_DIM // 2, 1),
                                pltpu.roll(xs, ROPE_AXIS_DIM // 2, 1))
            outs.append(xs * cos + partner * sin)
        return jnp.concatenate(outs, axis=1) * r

    def attn_q(c):
        lo = V7X_MXU_DIM * c
        p = proj(watt_ref, lo, lo + V7X_MXU_DIM)
        y = norm_rope(p, qng_ref[:, lo:lo + V7X_MXU_DIM]) * (ATT_HEAD_DIM ** -0.5)
        qbt_ref[lo:lo + V7X_MXU_DIM, :] = jnp.transpose(y).astype(BF16)

    def attn_kv():
        kk = norm_rope(proj(watt_ref, 1024, 1280), kng_ref[...]).astype(BF16)
        vt = jnp.transpose(proj(watt_ref, 1280, 1536)).astype(BF16)
        ones = jnp.ones((ATT_HEAD_DIM, vt.shape[1]), BF16)
        for g in range(ATT_KV_HEADS):
            kb_ref[g] = kk[:, ATT_HEAD_DIM * g:ATT_HEAD_DIM * (g + 1)]
            vbt_ref[2 * ATT_HEAD_DIM * g:2 * ATT_HEAD_DIM * g + ATT_HEAD_DIM, :] = (
                vt[ATT_HEAD_DIM * g:ATT_HEAD_DIM * (g + 1), :])
            vbt_ref[2 * ATT_HEAD_DIM * g + ATT_HEAD_DIM:2 * ATT_HEAD_DIM * (g + 1), :] = ones

    qa_ref[...] = (proj(wgla_ref, 0, 512) * (GLA_DK ** -0.5)).astype(BF16)
    attn_kv()
    ka_ref[...] = proj(wgla_ref, 512, 1024).astype(BF16)
    attn_q(0)
    va_ref[...] = proj(wgla_ref, 1024, 2048).astype(BF16)
    decay_gates(0)
    za_ref[...] = proj(wgla_ref, 2048, 3072).astype(BF16)
    attn_q(1)
    decay_gates(1)
    zb_ref[...] = proj(watt_ref, 1536, 2560).astype(BF16)
    attn_q(2)
    decay_gates(2)
    ma_ref[...] = proj(wm_ref, 0, 1024).astype(BF16)
    attn_q(3)
    decay_gates(3)
    mb_ref[...] = proj(wm_ref, 1024, 2048).astype(BF16)


def _inproj(x, lw, cos, sin, bd, seq_len):
    n = x.shape[0]
    tm = INPROJ_ROWS
    pos_blocks = seq_len // tm
    row = lambda i: (i, 0)
    const = lambda i: (0, 0)

    def wspec(shape):
        return pl.BlockSpec(shape, const, pipeline_mode=pl.Buffered(1))

    in_specs = [
        pl.BlockSpec((tm, D_MODEL), row),
        wspec((1, D_MODEL)),
        wspec((D_MODEL, 3072)),
        wspec((D_MODEL, V7X_LANES)),
        wspec((V7X_LANES, GLA_KEY_DIM)),
        wspec((V7X_LANES, GLA_KEY_DIM)),
        wspec((1, GLA_KEY_DIM)),
        wspec((1, GLA_KEY_DIM)),
        wspec((D_MODEL, 2560)),
        wspec((D_MODEL, 2048)),
        wspec((1, ATT_Q_DIM)),
        wspec((1, ATT_KV_DIM)),
        pl.BlockSpec((tm, V7X_LANES), lambda i: (i % pos_blocks, 0)),
        pl.BlockSpec((tm, V7X_LANES), lambda i: (i % pos_blocks, 0)),
        wspec((V7X_MXU_DIM, V7X_MXU_DIM)),
    ]
    out_shape = [
        jax.ShapeDtypeStruct((n, GLA_KEY_DIM), BF16),
        jax.ShapeDtypeStruct((n, GLA_KEY_DIM), BF16),
        jax.ShapeDtypeStruct((n, GLA_VALUE_DIM), BF16),
        jax.ShapeDtypeStruct((n, GLA_VALUE_DIM), BF16),
        jax.ShapeDtypeStruct((n, GLA_KEY_DIM), F32),
        jax.ShapeDtypeStruct((n, GLA_KEY_DIM), F32),
        jax.ShapeDtypeStruct((ATT_Q_DIM, n), BF16),
        jax.ShapeDtypeStruct((ATT_KV_HEADS, n, ATT_HEAD_DIM), BF16),
        jax.ShapeDtypeStruct((2 * ATT_KV_DIM, n), BF16),
        jax.ShapeDtypeStruct((n, ATT_Q_DIM), BF16),
        jax.ShapeDtypeStruct((n, D_MODEL), BF16),
        jax.ShapeDtypeStruct((n, D_MODEL), BF16),
    ]
    out_specs = [
        pl.BlockSpec((tm, GLA_KEY_DIM), row),
        pl.BlockSpec((tm, GLA_KEY_DIM), row),
        pl.BlockSpec((tm, GLA_VALUE_DIM), row),
        pl.BlockSpec((tm, GLA_VALUE_DIM), row),
        pl.BlockSpec((tm, GLA_KEY_DIM), row),
        pl.BlockSpec((tm, GLA_KEY_DIM), row),
        pl.BlockSpec((ATT_Q_DIM, tm), lambda i: (0, i)),
        pl.BlockSpec((ATT_KV_HEADS, tm, ATT_HEAD_DIM), lambda i: (0, i, 0)),
        pl.BlockSpec((2 * ATT_KV_DIM, tm), lambda i: (0, i)),
        pl.BlockSpec((tm, ATT_Q_DIM), row),
        pl.BlockSpec((tm, D_MODEL), row),
        pl.BlockSpec((tm, D_MODEL), row),
    ]
    weight_bytes = _nbytes((D_MODEL, 3072 + V7X_LANES + 2560 + 2048), BF16)
    block_bytes = (_nbytes((tm, D_MODEL), F32) + _nbytes((tm, 7 * 1024 + 512), BF16)
                   + 2 * _nbytes((tm, GLA_KEY_DIM), F32) + 2 * _nbytes((tm, V7X_LANES), F32))
    return pl.pallas_call(
        _inproj_body,
        grid=(n // tm,),
        in_specs=in_specs,
        out_specs=out_specs,
        out_shape=out_shape,
        compiler_params=pltpu.CompilerParams(
            dimension_semantics=("arbitrary",),
            vmem_limit_bytes=_vmem_limit(block_bytes, weight_bytes)),
        name="inproj",
    )(x, lw["norm_g"], lw["w_gla"], lw["w_gate"], lw["w_gf"], lw["w_gb"], lw["b_gf"], lw["b_gb"],
      lw["w_att"], lw["w_m"], lw["q_gain"], lw["k_gain"], cos, sin, bd)


def _gla_boundary(beta, m, reverse):
    c, width = beta.shape
    two_m = 2 * m
    off = m if reverse else m - 1
    pieces = []
    if two_m >= V7X_SUBLANES:
        for v in range(c // V7X_SUBLANES):
            r = (V7X_SUBLANES * v // two_m) * two_m + off
            pieces.append(jnp.broadcast_to(beta[r:r + 1, :], (V7X_SUBLANES, width)))
    else:
        sub = lax.broadcasted_iota(jnp.int32, (V7X_SUBLANES, width), 0)
        per = V7X_SUBLANES // two_m
        for v in range(c // V7X_SUBLANES):
            rows = [jnp.broadcast_to(beta[V7X_SUBLANES * v + two_m * j + off:V7X_SUBLANES * v + two_m * j + off + 1, :],
                                     (V7X_SUBLANES, width)) for j in range(per)]
            acc = rows[-1]
            for j in range(per - 2, -1, -1):
                acc = jnp.where(sub < two_m * (j + 1), rows[j], acc)
            pieces.append(acc)
    return jnp.concatenate(pieces, axis=0)


def _gla_tree_scores(q, k, g, beta, mask_ref, sgn_ref, reverse):
    row = lax.broadcasted_iota(jnp.int32, g.shape, 0)
    nlev = len(GLA_LEVELS)
    mask_base = GLA_MASK_TREE_BWD if reverse else GLA_MASK_TREE_FWD
    sgn_base = (nlev - 1) if reverse else 0
    a = None
    if not reverse:
        a = mask_ref[GLA_MASK_DIAG] * lax.dot_general(q, k, _NT, preferred_element_type=F32)
    for li, m in enumerate(GLA_LEVELS):
        if m == 1:
            parity = (row & 1) == (0 if reverse else 1)
            f = jnp.exp(jnp.where(parity, g, 0.0))
        else:
            f = jnp.exp2((beta - _gla_boundary(beta, m, reverse)) * sgn_ref[sgn_base + li - 1])
        fb = f.astype(BF16)
        am = mask_ref[mask_base + li] * lax.dot_general(q * fb, k * fb, _NT, preferred_element_type=F32)
        a = am if a is None else a + am
    return a


def _gla_chunk_local(q, k, v, g, mask_ref, sgn_ref, reverse, bounded):
    c = GLA_CHUNK
    row = lax.broadcasted_iota(jnp.int32, g.shape, 0)
    beta = g
    for sh in GLA_LEVELS:
        if reverse:
            beta = beta + jnp.where(row < c - sh, pltpu.roll(beta, c - sh, 0), 0.0)
        else:
            beta = beta + jnp.where(row >= sh, pltpu.roll(beta, sh, 0), 0.0)
    qd = q * jnp.exp(beta).astype(BF16)
    tot = beta[0:1, :] if reverse else beta[c - 1:c, :]
    kd = k * jnp.exp(tot - beta).astype(BF16)
    decay_col = jnp.transpose(jnp.broadcast_to(jnp.exp(tot), (V7X_SUBLANES, GLA_DK)))[:, 0:1]
    if bounded:
        kg = k * jnp.exp(-beta).astype(BF16)
        a = (mask_ref[GLA_MASK_TRI_BWD if reverse else GLA_MASK_TRI_FWD]
             * lax.dot_general(qd, kg, _NT, preferred_element_type=F32))
    else:
        a = _gla_tree_scores(q, k, g, beta, mask_ref, sgn_ref, reverse)
    kv = lax.dot_general(kd, v, _TN, preferred_element_type=F32)
    return qd, a.astype(BF16), kv, decay_col


def _gla_chunk_state(local, v, s_ref):
    qd, a, kv, decay_col = local
    s = s_ref[...]
    o = jnp.dot(qd, s.astype(BF16), preferred_element_type=F32)
    s_ref[...] = decay_col * s + kv
    return o + jnp.dot(a, v, preferred_element_type=F32)


def _gla_body(q_ref, k_ref, v_ref, gf_ref, gb_ref, za_ref, gn_ref, mask_ref, sgn_ref, o_ref,
              acc_ref, sf_ref, sb_ref):
    t = q_ref.shape[0]
    c = GLA_CHUNK
    n = t // c
    acc_ref[...] = jnp.zeros_like(acc_ref)
    sf_ref[...] = jnp.zeros_like(sf_ref)
    sb_ref[...] = jnp.zeros_like(sb_ref)

    chunk_tot = jnp.minimum(jnp.sum(gf_ref[...].reshape(n, c, GLA_DK), axis=1),
                            jnp.sum(gb_ref[...].reshape(n, c, GLA_DK), axis=1))
    bounded = jnp.min(chunk_tot) >= -GLA_BOUNDED_DECAY

    def scan(is_bounded):
        def step(i, carry):
            work = []
            for u in range(GLA_CHUNKS_PER_STEP):
                j = i * GLA_CHUNKS_PER_STEP + u
                for reverse, g_ref, s_ref in ((False, gf_ref, sf_ref), (True, gb_ref, sb_ref)):
                    sl = pl.ds(pl.multiple_of(((n - 1 - j) if reverse else j) * c, c), c)
                    local = _gla_chunk_local(q_ref[sl, :], k_ref[sl, :], v_ref[sl, :], g_ref[sl, :],
                                             mask_ref, sgn_ref, reverse, is_bounded)
                    work.append((sl, local, s_ref))
            for sl, local, s_ref in work:
                acc_ref[sl, :] += _gla_chunk_state(local, v_ref[sl, :], s_ref)
            return carry

        lax.fori_loop(0, n // GLA_CHUNKS_PER_STEP, step, 0)

    @pl.when(bounded)
    def _():
        scan(True)

    @pl.when(jnp.logical_not(bounded))
    def _():
        scan(False)

    rows = GLA_FINAL_ROWS

    def fin(i, carry):
        sl = pl.ds(pl.multiple_of(i * rows, rows), rows)
        o = acc_ref[sl, :]
        ms = jnp.mean(o * o, axis=-1, keepdims=True)
        y = o * lax.rsqrt(ms + EPS) * gn_ref[...]
        o_ref[sl, :] = (y * _silu(za_ref[sl, :].astype(F32))).astype(BF16)
        return carry

    lax.fori_loop(0, t // rows, fin, 0)


def _gla_masks():
    c = GLA_CHUNK
    ri = np.arange(c)[:, None]
    ci = np.arange(c)[None, :]
    masks = [ri == ci]
    for m in GLA_LEVELS:
        same = (ri // (2 * m)) == (ci // (2 * m))
        masks.append(same & ((ri % (2 * m)) >= m) & ((ci % (2 * m)) < m))
    for m in GLA_LEVELS:
        same = (ri // (2 * m)) == (ci // (2 * m))
        masks.append(same & ((ri % (2 * m)) < m) & ((ci % (2 * m)) >= m))
    masks.append(ri >= ci)
    masks.append(ri < ci)
    sgn = []
    row = np.arange(c)[:, None] * np.ones((1, GLA_DK), np.int64)
    for reverse in (False, True):
        for m in GLA_LEVELS[1:]:
            upper = (row % (2 * m)) >= m
            sgn.append(np.where(upper != reverse, np.log2(np.e), -np.log2(np.e)))
    return jnp.asarray(np.stack(masks).astype(np.float32)), jnp.asarray(np.stack(sgn).astype(np.float32))


def _gla(qa, ka, va, gf, gb, za, gn, masks, sgn, batch, seq_len):
    n = qa.shape[0]
    t = seq_len
    assert t % (GLA_CHUNK * GLA_CHUNKS_PER_STEP) == 0 and t % GLA_FINAL_ROWS == 0, t
    kspec = pl.BlockSpec((t, GLA_DK), lambda b, h: (b, h))
    vspec = pl.BlockSpec((t, GLA_DV), lambda b, h: (b, h))
    nm = masks.shape[0]
    block_bytes = (2 * _nbytes((t, GLA_DK), BF16) + 3 * _nbytes((t, GLA_DV), BF16)
                   + 2 * _nbytes((t, GLA_DK), F32) + _nbytes(masks.shape, F32) + _nbytes(sgn.shape, F32))
    scratch_bytes = _nbytes((t, GLA_DV), F32) + 2 * _nbytes((GLA_DK, GLA_DV), F32)
    return pl.pallas_call(
        _gla_body,
        grid=(batch, GLA_HEADS),
        in_specs=[kspec, kspec, vspec, kspec, kspec, vspec,
                  pl.BlockSpec((1, GLA_DV), lambda b, h: (0, 0)),
                  pl.BlockSpec((nm, GLA_CHUNK, GLA_CHUNK), lambda b, h: (0, 0, 0)),
                  pl.BlockSpec(sgn.shape, lambda b, h: (0, 0, 0))],
        out_specs=vspec,
        out_shape=jax.ShapeDtypeStruct((n, GLA_VALUE_DIM), BF16),
        scratch_shapes=[pltpu.VMEM((t, GLA_DV), F32),
                        pltpu.VMEM((GLA_DK, GLA_DV), F32),
                        pltpu.VMEM((GLA_DK, GLA_DV), F32)],
        compiler_params=pltpu.CompilerParams(
            dimension_semantics=("arbitrary", "arbitrary"),
            vmem_limit_bytes=_vmem_limit(block_bytes, scratch_bytes)),
        name="gla",
    )(qa, ka, va, gf, gb, za, gn, masks, sgn)


def _attn_body(qt_ref, k_ref, vt_ref, z_ref, o_ref, s_ref, acc_ref):
    t = k_ref.shape[1]
    tq = ATT_Q_ROWS
    nq = t // tq
    kc = ATT_KEY_CHUNK
    nk = t // kc
    cols = ATT_GROUP * tq

    def load_q(i):
        qt = qt_ref[:, pl.ds(pl.multiple_of(i * tq, tq), tq)]
        return jnp.concatenate([qt[ATT_HEAD_DIM * j:ATT_HEAD_DIM * (j + 1), :] for j in range(ATT_GROUP)], axis=1)

    def scores(q4t, c, m, buf):
        sl = pl.ds(pl.multiple_of(c * kc, kc), kc)
        s = jnp.dot(k_ref[0, sl, :], q4t, preferred_element_type=F32)
        s_ref[buf] = s
        return jnp.maximum(m, jnp.max(s, axis=0, keepdims=True))

    def weighted(c, m_old, m_new, buf):
        sl = pl.ds(pl.multiple_of(c * kc, kc), kc)
        p = jnp.exp(s_ref[buf] - m_new).astype(BF16)
        acc_ref[...] = (jnp.exp(m_old - m_new) * acc_ref[...]
                        + jnp.dot(vt_ref[:, sl], p, preferred_element_type=F32))

    m_init = jnp.full((1, cols), -jnp.inf, F32)

    def tile(i, m_first):
        q4t = load_q(i)
        acc_ref[...] = jnp.zeros_like(acc_ref)

        def pair(j, carry):
            m_a, m_b = carry
            c = 2 * j
            m_c = scores(q4t, c + 1, m_b, 1)
            weighted(c, m_a, m_b, 0)
            m_d = scores(q4t, c + 2, m_c, 0)
            weighted(c + 1, m_b, m_c, 1)
            return m_c, m_d

        m_a, m_b = lax.fori_loop(0, nk // 2 - 1, pair, (m_init, m_first), unroll=True)
        m_c = scores(q4t, nk - 1, m_b, 1)
        weighted(nk - 2, m_a, m_b, 0)
        m_next = scores(load_q(jnp.minimum(i + 1, nq - 1)), 0, m_init, 0)
        weighted(nk - 1, m_b, m_c, 1)
        acc = acc_ref[...]
        ot = acc[0:ATT_HEAD_DIM, :] / acc[ATT_HEAD_DIM:ATT_HEAD_DIM + 1, :]
        o = jnp.concatenate([jnp.transpose(ot[:, tq * j:tq * (j + 1)]) for j in range(ATT_GROUP)], axis=1)
        rows = pl.ds(pl.multiple_of(i * tq, tq), tq)
        o_ref[rows, :] = (o * _silu(z_ref[rows, :].astype(F32))).astype(BF16)
        return m_next

    lax.fori_loop(0, nq, tile, scores(load_q(0), 0, m_init, 0))


def _attn(qbt, kb, vbt, zb, batch, seq_len):
    n = zb.shape[0]
    t = seq_len
    tq = ATT_Q_ROWS
    width = ATT_GROUP * ATT_HEAD_DIM
    ospec = pl.BlockSpec((t, width), lambda b, g: (b, g))
    block_bytes = (3 * _nbytes((t, width), BF16) + _nbytes((t, V7X_LANES), BF16)
                   + _nbytes((2 * ATT_HEAD_DIM, t), BF16))
    scratch_bytes = (_nbytes((2, ATT_KEY_CHUNK, ATT_GROUP * tq), F32)
                     + _nbytes((2 * ATT_HEAD_DIM, ATT_GROUP * tq), F32))
    return pl.pallas_call(
        _attn_body,
        grid=(batch, ATT_KV_HEADS),
        in_specs=[pl.BlockSpec((width, t), lambda b, g: (g, b)),
                  pl.BlockSpec((1, t, ATT_HEAD_DIM), lambda b, g: (g, b, 0)),
                  pl.BlockSpec((2 * ATT_HEAD_DIM, t), lambda b, g: (g, b)),
                  ospec],
        out_specs=ospec,
        out_shape=jax.ShapeDtypeStruct((n, ATT_Q_DIM), BF16),
        scratch_shapes=[pltpu.VMEM((2, ATT_KEY_CHUNK, ATT_GROUP * tq), F32),
                        pltpu.VMEM((2 * ATT_HEAD_DIM, ATT_GROUP * tq), F32)],
        compiler_params=pltpu.CompilerParams(
            dimension_semantics=("arbitrary", "arbitrary"),
            vmem_limit_bytes=_vmem_limit(block_bytes, scratch_bytes)),
        name="attn",
    )(qbt, kb, vbt, zb)


def _outproj_body(oa_ref, ob_ref, ma_ref, mb_ref, x_ref, wa_ref, wb_ref, wo_ref, fg_ref, o_ref, *, final):
    ya = jnp.dot(oa_ref[...], wa_ref[...], preferred_element_type=F32)
    yb = jnp.dot(ob_ref[...], wb_ref[...], preferred_element_type=F32)
    merged = (jax.nn.sigmoid(ma_ref[...].astype(F32)) * ya
              + jax.nn.sigmoid(mb_ref[...].astype(F32)) * yb)
    y = x_ref[...] + jnp.dot(merged.astype(BF16), wo_ref[...], preferred_element_type=F32)
    if final:
        ms = jnp.mean(y * y, axis=-1, keepdims=True)
        y = y * lax.rsqrt(ms + EPS) * fg_ref[...]
    o_ref[...] = y


def _outproj(oa, ob, ma, mb, x, lw, final_gain, final):
    n = x.shape[0]
    tm = OUTPROJ_ROWS
    row = lambda i: (i, 0)
    const = lambda i: (0, 0)
    act = pl.BlockSpec((tm, D_MODEL), row)
    wspec = pl.BlockSpec((D_MODEL, D_MODEL), const, pipeline_mode=pl.Buffered(1))
    block_bytes = 4 * _nbytes((tm, D_MODEL), BF16) + 2 * _nbytes((tm, D_MODEL), F32)
    weight_bytes = 3 * _nbytes((D_MODEL, D_MODEL), BF16)
    return pl.pallas_call(
        functools.partial(_outproj_body, final=final),
        grid=(n // tm,),
        in_specs=[act, act, act, act, act, wspec, wspec, wspec,
                  pl.BlockSpec((1, D_MODEL), const, pipeline_mode=pl.Buffered(1))],
        out_specs=act,
        out_shape=jax.ShapeDtypeStruct((n, D_MODEL), F32),
        compiler_params=pltpu.CompilerParams(
            dimension_semantics=("arbitrary",),
            vmem_limit_bytes=_vmem_limit(block_bytes, weight_bytes)),
        name="outproj",
    )(oa, ob, ma, mb, x, lw["w_a"], lw["w_b"], lw["w_o"], final_gain)


def _rope_tables(seq_len):
    rows = seq_len // GRID_W
    r = jnp.repeat(jnp.arange(rows, dtype=F32), GRID_W)
    c = jnp.tile(jnp.arange(GRID_W, dtype=F32), rows)
    nf = ROPE_AXIS_DIM // 2
    inv = ROPE_THETA ** (-jnp.arange(nf, dtype=F32) / nf)
    ang_r = r[:, None] * inv
    ang_c = c[:, None] * inv
    cos = jnp.concatenate([jnp.cos(ang_r), jnp.cos(ang_r), jnp.cos(ang_c), jnp.cos(ang_c)], axis=-1)
    sin = jnp.concatenate([-jnp.sin(ang_r), jnp.sin(ang_r), -jnp.sin(ang_c), jnp.sin(ang_c)], axis=-1)
    reps = V7X_LANES // ATT_HEAD_DIM
    return jnp.tile(cos, (1, reps)), jnp.tile(sin, (1, reps))


def _layer_weights(l, norm_g, w_in, w_gate_f, b_gate_f, w_gate_b, b_gate_b, gla_norm_g, q_norm_g, k_norm_g,
                   w_branch_a, w_branch_b, w_out):
    pts = np.cumsum((0,) + IN_SIZES)
    col = lambda i: w_in[l][:, pts[i]:pts[i + 1]]
    w = w_in[l]
    w_gate = jnp.zeros((D_MODEL, V7X_LANES), F32).at[:, :2 * GATE_RANK].set(w[:, pts[3]:pts[5]])
    w_gf = jnp.zeros((V7X_LANES, GLA_KEY_DIM), F32).at[:GATE_RANK].set(w_gate_f[l])
    w_gb = jnp.zeros((V7X_LANES, GLA_KEY_DIM), F32).at[GATE_RANK:2 * GATE_RANK].set(w_gate_b[l])
    return {
        "norm_g": norm_g[l].reshape(1, D_MODEL),
        "w_gla": jnp.concatenate([col(0), col(1), col(2), col(5)], axis=1).astype(BF16),
        "w_gate": w_gate.astype(BF16),
        "w_gf": w_gf.astype(BF16),
        "w_gb": w_gb.astype(BF16),
        "b_gf": b_gate_f[l].reshape(1, GLA_KEY_DIM),
        "b_gb": b_gate_b[l].reshape(1, GLA_KEY_DIM),
        "w_att": jnp.concatenate([col(6), col(7), col(8), col(9)], axis=1).astype(BF16),
        "w_m": jnp.concatenate([col(10), col(11)], axis=1).astype(BF16),
        "q_gain": jnp.tile(q_norm_g[l], ATT_Q_HEADS).reshape(1, ATT_Q_DIM),
        "k_gain": jnp.tile(k_norm_g[l], ATT_KV_HEADS).reshape(1, ATT_KV_DIM),
        "gla_gain": gla_norm_g[l].reshape(1, GLA_DV),
        "w_a": w_branch_a[l].astype(BF16),
        "w_b": w_branch_b[l].astype(BF16),
        "w_o": w_out[l].astype(BF16),
    }


def _trunk(x3, layers, final_gain, masks, sgn, bd):
    batch, seq_len, _ = x3.shape
    x = x3.reshape(batch * seq_len, D_MODEL)
    cos, sin = _rope_tables(seq_len)
    for l, lw in enumerate(layers):
        qa, ka, va, za, gf, gb, qbt, kb, vbt, zb, ma, mb = _inproj(x, lw, cos, sin, bd, seq_len)
        oa = _gla(qa, ka, va, gf, gb, za, lw["gla_gain"], masks, sgn, batch, seq_len)
        ob = _attn(qbt, kb, vbt, zb, batch, seq_len)
        x = _outproj(oa, ob, ma, mb, x, lw, final_gain, final=(l == len(layers) - 1))
    return x.reshape(batch, seq_len, D_MODEL)


def kernel(x_prompt, x_sample, norm_g, w_in, w_gate_f, b_gate_f, w_gate_b, b_gate_b, gla_norm_g, q_norm_g,
           k_norm_g, w_branch_a, w_branch_b, w_out, final_norm_g):
    layers = [_layer_weights(l, norm_g, w_in, w_gate_f, b_gate_f, w_gate_b, b_gate_b, gla_norm_g, q_norm_g,
                             k_norm_g, w_branch_a, w_branch_b, w_out) for l in range(DEPTH)]
    final_gain = final_norm_g.reshape(1, D_MODEL)
    masks, sgn = _gla_masks()
    head = np.arange(V7X_MXU_DIM) // ATT_HEAD_DIM
    bd = jnp.asarray((head[:, None] == head[None, :]).astype(np.float32), dtype=BF16)
    y_prompt = _trunk(x_prompt, layers, final_gain, masks, sgn, bd)
    y_sample = _trunk(x_sample, layers, final_gain, masks, sgn, bd)
    return (y_prompt, y_sample)
```

```python
import functools

import numpy as np
import jax
import jax.numpy as jnp
from jax import lax
from jax.experimental import pallas as pl
from jax.experimental.pallas import tpu as pltpu

F32, BF16 = jnp.float32, jnp.bfloat16

D_MODEL = 1024
DEPTH = 4
EPS = 1e-6
GRID_W = 64
GLA_HEADS = 4
GLA_KEY_DIM = D_MODEL // 2
GLA_VALUE_DIM = D_MODEL
GLA_DK = GLA_KEY_DIM // GLA_HEADS
GLA_DV = GLA_VALUE_DIM // GLA_HEADS
GATE_RANK = 16
GATE_NORMALIZER = 16.0
ATT_HEAD_DIM = 64
ATT_Q_HEADS = D_MODEL // ATT_HEAD_DIM
ATT_KV_HEADS = 4
ATT_GROUP = ATT_Q_HEADS // ATT_KV_HEADS
ATT_Q_DIM = ATT_Q_HEADS * ATT_HEAD_DIM
ATT_KV_DIM = ATT_KV_HEADS * ATT_HEAD_DIM
ROPE_AXIS_DIM = ATT_HEAD_DIM // 2
ROPE_THETA = 10000.0
IN_SIZES = (GLA_KEY_DIM, GLA_KEY_DIM, GLA_VALUE_DIM, GATE_RANK, GATE_RANK, GLA_VALUE_DIM,
            ATT_Q_DIM, ATT_KV_DIM, ATT_KV_DIM, ATT_Q_DIM, D_MODEL, D_MODEL)

V7X_LANES = 128
V7X_SUBLANES = 8
V7X_MXU_DIM = 256
V7X_VMEM_BYTES = 64 * 1024 * 1024
V7X_VMEM_BUDGET = V7X_VMEM_BYTES - 8 * 1024 * 1024

INPROJ_ROWS = 512
OUTPROJ_ROWS = 512
GLA_CHUNK = 64
GLA_LEVELS = (1, 2, 4, 8, 16, 32)
GLA_CHUNKS_PER_STEP = 8
GLA_FINAL_ROWS = 256
GLA_BOUNDED_DECAY = 40.0
GLA_MASK_DIAG = 0
GLA_MASK_TREE_FWD = 1
GLA_MASK_TREE_BWD = GLA_MASK_TREE_FWD + len(GLA_LEVELS)
GLA_MASK_TRI_FWD = GLA_MASK_TREE_BWD + len(GLA_LEVELS)
GLA_MASK_TRI_BWD = GLA_MASK_TRI_FWD + 1
ATT_Q_SCALE = ATT_HEAD_DIM ** -0.5 * float(np.log2(np.e))
ATT_Q_ROWS = 256
ATT_KEY_CHUNK = 512

_NT = (((1,), (1,)), ((), ()))
_TN = (((0,), (0,)), ((), ()))


def _vmem_limit(block_bytes, scratch_bytes=0):
    want = 2 * block_bytes + scratch_bytes + 16 * 1024 * 1024
    return int(min(want, V7X_VMEM_BUDGET))


def _nbytes(shape, dtype):
    return int(np.prod(shape)) * jnp.dtype(dtype).itemsize


def _log_sigmoid(z):
    return jnp.minimum(z, 0.0) - jnp.log(1.0 + jnp.exp(-jnp.abs(z)))


def _silu(z):
    return z * jax.nn.sigmoid(z)


def _inproj_body(x_ref, ng_ref, wgla_ref, wgate_ref, wgf_ref, wgb_ref, bgf_ref, bgb_ref,
                 watt_ref, wm_ref, qng_ref, kng_ref, cos_ref, sin_ref, bd_ref,
                 qa_ref, ka_ref, va_ref, za_ref, gf_ref, gb_ref,
                 qbt_ref, kb_ref, vbt_ref, zb_ref, ma_ref, mb_ref):
    x = x_ref[...]
    ms = jnp.mean(x * x, axis=-1, keepdims=True)
    h = (x * lax.rsqrt(ms + EPS) * ng_ref[...]).astype(BF16)

    def proj(w_ref, lo, hi):
        return jnp.dot(h, w_ref[:, lo:hi], preferred_element_type=F32)

    pg = proj(wgate_ref, 0, V7X_LANES).astype(BF16)
    gate_rows = pg.shape[0] // 4

    def decay_gates(part):
        rows = slice(gate_rows * part, gate_rows * (part + 1))
        zf = jnp.dot(pg[rows], wgf_ref[...], preferred_element_type=F32) + bgf_ref[...]
        zb = jnp.dot(pg[rows], wgb_ref[...], preferred_element_type=F32) + bgb_ref[...]
        gf_ref[rows, :] = _log_sigmoid(zf) * (1.0 / GATE_NORMALIZER)
        gb_ref[rows, :] = _log_sigmoid(zb) * (1.0 / GATE_NORMALIZER)

    cos = cos_ref[...]
    sin = sin_ref[...]
    bd = bd_ref[...]
    lane = lax.broadcasted_iota(jnp.int32, cos.shape, 1)
    first_half = (lane & (ROPE_AXIS_DIM // 2)) == 0

    def norm_rope(p, gain):
        sq = (p * p).astype(BF16)
        msq = jnp.dot(sq, bd, preferred_element_type=F32)
        r = lax.rsqrt(msq + EPS)
        xg = p * gain
        outs = []
        for s in range(2):
            xs = xg[:, V7X_LANES * s:V7X_LANES * (s + 1)]
            partner = jnp.where(first_half,
                                pltpu.roll(xs, V7X_LANES - ROPE_AXIS_DIM // 2, 1),
                                pltpu.roll(xs, ROPE_AXIS_DIM // 2, 1))
            outs.append(xs * cos + partner * sin)
        return jnp.concatenate(outs, axis=1) * r

    def attn_q(c):
        lo = V7X_MXU_DIM * c
        p = proj(watt_ref, lo, lo + V7X_MXU_DIM)
        y = norm_rope(p, qng_ref[:, lo:lo + V7X_MXU_DIM])
        qbt_ref[lo:lo + V7X_MXU_DIM, :] = jnp.transpose(y).astype(BF16)

    def attn_kv():
        kk = norm_rope(proj(watt_ref, 1024, 1280), kng_ref[...]).astype(BF16)
        vt = jnp.transpose(proj(watt_ref, 1280, 1536)).astype(BF16)
        ones = jnp.ones((ATT_HEAD_DIM, vt.shape[1]), BF16)
        for g in range(ATT_KV_HEADS):
            kb_ref[g] = kk[:, ATT_HEAD_DIM * g:ATT_HEAD_DIM * (g + 1)]
            vbt_ref[2 * ATT_HEAD_DIM * g:2 * ATT_HEAD_DIM * g + ATT_HEAD_DIM, :] = (
                vt[ATT_HEAD_DIM * g:ATT_HEAD_DIM * (g + 1), :])
            vbt_ref[2 * ATT_HEAD_DIM * g + ATT_HEAD_DIM:2 * ATT_HEAD_DIM * (g + 1), :] = ones

    qa_ref[...] = (proj(wgla_ref, 0, 512) * (GLA_DK ** -0.5)).astype(BF16)
    attn_kv()
    ka_ref[...] = proj(wgla_ref, 512, 1024).astype(BF16)
    attn_q(0)
    va_ref[...] = proj(wgla_ref, 1024, 2048).astype(BF16)
    decay_gates(0)
    za_ref[...] = proj(wgla_ref, 2048, 3072).astype(BF16)
    attn_q(1)
    decay_gates(1)
    zb_ref[...] = proj(watt_ref, 1536, 2560).astype(BF16)
    attn_q(2)
    decay_gates(2)
    ma_ref[...] = proj(wm_ref, 0, 1024).astype(BF16)
    attn_q(3)
    decay_gates(3)
    mb_ref[...] = proj(wm_ref, 1024, 2048).astype(BF16)


def _inproj(x, lw, cos, sin, bd, seq_len):
    n = x.shape[0]
    tm = INPROJ_ROWS
    pos_blocks = seq_len // tm
    row = lambda i: (i, 0)
    const = lambda i: (0, 0)

    def wspec(shape):
        return pl.BlockSpec(shape, const, pipeline_mode=pl.Buffered(1))

    in_specs = [
        pl.BlockSpec((tm, D_MODEL), row),
        wspec((1, D_MODEL)),
        wspec((D_MODEL, 3072)),
        wspec((D_MODEL, V7X_LANES)),
        wspec((V7X_LANES, GLA_KEY_DIM)),
        wspec((V7X_LANES, GLA_KEY_DIM)),
        wspec((1, GLA_KEY_DIM)),
        wspec((1, GLA_KEY_DIM)),
        wspec((D_MODEL, 2560)),
        wspec((D_MODEL, 2048)),
        wspec((1, ATT_Q_DIM)),
        wspec((1, ATT_KV_DIM)),
        pl.BlockSpec((tm, V7X_LANES), lambda i: (i % pos_blocks, 0)),
        pl.BlockSpec((tm, V7X_LANES), lambda i: (i % pos_blocks, 0)),
        wspec((V7X_MXU_DIM, V7X_MXU_DIM)),
    ]
    out_shape = [
        jax.ShapeDtypeStruct((n, GLA_KEY_DIM), BF16),
        jax.ShapeDtypeStruct((n, GLA_KEY_DIM), BF16),
        jax.ShapeDtypeStruct((n, GLA_VALUE_DIM), BF16),
        jax.ShapeDtypeStruct((n, GLA_VALUE_DIM), BF16),
        jax.ShapeDtypeStruct((n, GLA_KEY_DIM), F32),
        jax.ShapeDtypeStruct((n, GLA_KEY_DIM), F32),
        jax.ShapeDtypeStruct((ATT_Q_DIM, n), BF16),
        jax.ShapeDtypeStruct((ATT_KV_HEADS, n, ATT_HEAD_DIM), BF16),
        jax.ShapeDtypeStruct((2 * ATT_KV_DIM, n), BF16),
        jax.ShapeDtypeStruct((n, ATT_Q_DIM), BF16),
        jax.ShapeDtypeStruct((n, D_MODEL), BF16),
        jax.ShapeDtypeStruct((n, D_MODEL), BF16),
    ]
    out_specs = [
        pl.BlockSpec((tm, GLA_KEY_DIM), row),
        pl.BlockSpec((tm, GLA_KEY_DIM), row),
        pl.BlockSpec((tm, GLA_VALUE_DIM), row),
        pl.BlockSpec((tm, GLA_VALUE_DIM), row),
        pl.BlockSpec((tm, GLA_KEY_DIM), row),
        pl.BlockSpec((tm, GLA_KEY_DIM), row),
        pl.BlockSpec((ATT_Q_DIM, tm), lambda i: (0, i)),
        pl.BlockSpec((ATT_KV_HEADS, tm, ATT_HEAD_DIM), lambda i: (0, i, 0)),
        pl.BlockSpec((2 * ATT_KV_DIM, tm), lambda i: (0, i)),
        pl.BlockSpec((tm, ATT_Q_DIM), row),
        pl.BlockSpec((tm, D_MODEL), row),
        pl.BlockSpec((tm, D_MODEL), row),
    ]
    weight_bytes = _nbytes((D_MODEL, 3072 + V7X_LANES + 2560 + 2048), BF16)
    block_bytes = (_nbytes((tm, D_MODEL), F32) + _nbytes((tm, 7 * 1024 + 512), BF16)
                   + 2 * _nbytes((tm, GLA_KEY_DIM), F32) + 2 * _nbytes((tm, V7X_LANES), F32))
    return pl.pallas_call(
        _inproj_body,
        grid=(n // tm,),
        in_specs=in_specs,
        out_specs=out_specs,
        out_shape=out_shape,
        compiler_params=pltpu.CompilerParams(
            dimension_semantics=("arbitrary",),
            vmem_limit_bytes=_vmem_limit(block_bytes, weight_bytes)),
        name="inproj",
    )(x, lw["norm_g"], lw["w_gla"], lw["w_gate"], lw["w_gf"], lw["w_gb"], lw["b_gf"], lw["b_gb"],
      lw["w_att"], lw["w_m"], lw["q_gain"], lw["k_gain"], cos, sin, bd)


def _gla_boundary(beta, m, reverse):
    c, width = beta.shape
    two_m = 2 * m
    off = m if reverse else m - 1
    pieces = []
    if two_m >= V7X_SUBLANES:
        for v in range(c // V7X_SUBLANES):
            r = (V7X_SUBLANES * v // two_m) * two_m + off
            pieces.append(jnp.broadcast_to(beta[r:r + 1, :], (V7X_SUBLANES, width)))
    else:
        sub = lax.broadcasted_iota(jnp.int32, (V7X_SUBLANES, width), 0)
        per = V7X_SUBLANES // two_m
        for v in range(c // V7X_SUBLANES):
            rows = [jnp.broadcast_to(beta[V7X_SUBLANES * v + two_m * j + off:V7X_SUBLANES * v + two_m * j + off + 1, :],
                                     (V7X_SUBLANES, width)) for j in range(per)]
            acc = rows[-1]
            for j in range(per - 2, -1, -1):
                acc = jnp.where(sub < two_m * (j + 1), rows[j], acc)
            pieces.append(acc)
    return jnp.concatenate(pieces, axis=0)


def _gla_tree_scores(q, k, g, beta, mask_ref, sgn_ref, reverse):
    row = lax.broadcasted_iota(jnp.int32, g.shape, 0)
    nlev = len(GLA_LEVELS)
    mask_base = GLA_MASK_TREE_BWD if reverse else GLA_MASK_TREE_FWD
    sgn_base = (nlev - 1) if reverse else 0
    a = None
    if not reverse:
        a = mask_ref[GLA_MASK_DIAG] * lax.dot_general(q, k, _NT, preferred_element_type=F32)
    for li, m in enumerate(GLA_LEVELS):
        if m == 1:
            parity = (row & 1) == (0 if reverse else 1)
            f = jnp.exp(jnp.where(parity, g, 0.0))
        else:
            f = jnp.exp2((beta - _gla_boundary(beta, m, reverse)) * sgn_ref[sgn_base + li - 1])
        fb = f.astype(BF16)
        am = mask_ref[mask_base + li] * lax.dot_general(q * fb, k * fb, _NT, preferred_element_type=F32)
        a = am if a is None else a + am
    return a


def _gla_chunk_local(q, k, v, g, mask_ref, sgn_ref, reverse, bounded):
    c = GLA_CHUNK
    row = lax.broadcasted_iota(jnp.int32, g.shape, 0)
    beta = g
    for sh in GLA_LEVELS:
        if reverse:
            beta = beta + jnp.where(row < c - sh, pltpu.roll(beta, c - sh, 0), 0.0)
        else:
            beta = beta + jnp.where(row >= sh, pltpu.roll(beta, sh, 0), 0.0)
    qd = q * jnp.exp(beta).astype(BF16)
    tot = beta[0:1, :] if reverse else beta[c - 1:c, :]
    kd = k * jnp.exp(tot - beta).astype(BF16)
    decay_col = jnp.transpose(jnp.broadcast_to(jnp.exp(tot), (V7X_SUBLANES, GLA_DK)))[:, 0:1]
    if bounded:
        kg = k * jnp.exp(-beta).astype(BF16)
        a = (mask_ref[GLA_MASK_TRI_BWD if reverse else GLA_MASK_TRI_FWD]
             * lax.dot_general(qd, kg, _NT, preferred_element_type=F32))
    else:
        a = _gla_tree_scores(q, k, g, beta, mask_ref, sgn_ref, reverse)
    kv = lax.dot_general(kd, v, _TN, preferred_element_type=F32)
    return qd, a.astype(BF16), kv, decay_col


def _gla_chunk_state(local, v, s_ref):
    qd, a, kv, decay_col = local
    s = s_ref[...]
    o = jnp.dot(qd, s.astype(BF16), preferred_element_type=F32)
    s_ref[...] = decay_col * s + kv
    return o + jnp.dot(a, v, preferred_element_type=F32)


def _gla_body(q_ref, k_ref, v_ref, gf_ref, gb_ref, za_ref, gn_ref, mask_ref, sgn_ref, o_ref,
              acc_ref, sf_ref, sb_ref):
    t = q_ref.shape[0]
    c = GLA_CHUNK
    n = t // c
    acc_ref[...] = jnp.zeros_like(acc_ref)
    sf_ref[...] = jnp.zeros_like(sf_ref)
    sb_ref[...] = jnp.zeros_like(sb_ref)

    chunk_tot = jnp.minimum(jnp.sum(gf_ref[...].reshape(n, c, GLA_DK), axis=1),
                            jnp.sum(gb_ref[...].reshape(n, c, GLA_DK), axis=1))
    bounded = jnp.min(chunk_tot) >= -GLA_BOUNDED_DECAY

    def scan(is_bounded):
        def step(i, carry):
            work = []
            for u in range(GLA_CHUNKS_PER_STEP):
                j = i * GLA_CHUNKS_PER_STEP + u
                for reverse, g_ref, s_ref in ((False, gf_ref, sf_ref), (True, gb_ref, sb_ref)):
                    sl = pl.ds(pl.multiple_of(((n - 1 - j) if reverse else j) * c, c), c)
                    local = _gla_chunk_local(q_ref[sl, :], k_ref[sl, :], v_ref[sl, :], g_ref[sl, :],
                                             mask_ref, sgn_ref, reverse, is_bounded)
                    work.append((sl, local, s_ref))
            for sl, local, s_ref in work:
                acc_ref[sl, :] += _gla_chunk_state(local, v_ref[sl, :], s_ref)
            return carry

        lax.fori_loop(0, n // GLA_CHUNKS_PER_STEP, step, 0)

    @pl.when(bounded)
    def _():
        scan(True)

    @pl.when(jnp.logical_not(bounded))
    def _():
        scan(False)

    rows = GLA_FINAL_ROWS

    def fin(i, carry):
        sl = pl.ds(pl.multiple_of(i * rows, rows), rows)
        o = acc_ref[sl, :]
        ms = jnp.mean(o * o, axis=-1, keepdims=True)
        y = o * lax.rsqrt(ms + EPS) * gn_ref[...]
        o_ref[sl, :] = (y * _silu(za_ref[sl, :].astype(F32))).astype(BF16)
        return carry

    lax.fori_loop(0, t // rows, fin, 0)


def _gla_masks():
    c = GLA_CHUNK
    ri = np.arange(c)[:, None]
    ci = np.arange(c)[None, :]
    masks = [ri == ci]
    for m in GLA_LEVELS:
        same = (ri // (2 * m)) == (ci // (2 * m))
        masks.append(same & ((ri % (2 * m)) >= m) & ((ci % (2 * m)) < m))
    for m in GLA_LEVELS:
        same = (ri // (2 * m)) == (ci // (2 * m))
        masks.append(same & ((ri % (2 * m)) < m) & ((ci % (2 * m)) >= m))
    masks.append(ri >= ci)
    masks.append(ri < ci)
    sgn = []
    row = np.arange(c)[:, None] * np.ones((1, GLA_DK), np.int64)
    for reverse in (False, True):
        for m in GLA_LEVELS[1:]:
            upper = (row % (2 * m)) >= m
            sgn.append(np.where(upper != reverse, np.log2(np.e), -np.log2(np.e)))
    return jnp.asarray(np.stack(masks).astype(np.float32)), jnp.asarray(np.stack(sgn).astype(np.float32))


def _gla(qa, ka, va, gf, gb, za, gn, masks, sgn, batch, seq_len):
    n = qa.shape[0]
    t = seq_len
    assert t % (GLA_CHUNK * GLA_CHUNKS_PER_STEP) == 0 and t % GLA_FINAL_ROWS == 0, t
    kspec = pl.BlockSpec((t, GLA_DK), lambda b, h: (b, h))
    vspec = pl.BlockSpec((t, GLA_DV), lambda b, h: (b, h))
    nm = masks.shape[0]
    block_bytes = (2 * _nbytes((t, GLA_DK), BF16) + 3 * _nbytes((t, GLA_DV), BF16)
                   + 2 * _nbytes((t, GLA_DK), F32) + _nbytes(masks.shape, F32) + _nbytes(sgn.shape, F32))
    scratch_bytes = _nbytes((t, GLA_DV), F32) + 2 * _nbytes((GLA_DK, GLA_DV), F32)
    return pl.pallas_call(
        _gla_body,
        grid=(batch, GLA_HEADS),
        in_specs=[kspec, kspec, vspec, kspec, kspec, vspec,
                  pl.BlockSpec((1, GLA_DV), lambda b, h: (0, 0)),
                  pl.BlockSpec((nm, GLA_CHUNK, GLA_CHUNK), lambda b, h: (0, 0, 0)),
                  pl.BlockSpec(sgn.shape, lambda b, h: (0, 0, 0))],
        out_specs=vspec,
        out_shape=jax.ShapeDtypeStruct((n, GLA_VALUE_DIM), BF16),
        scratch_shapes=[pltpu.VMEM((t, GLA_DV), F32),
                        pltpu.VMEM((GLA_DK, GLA_DV), F32),
                        pltpu.VMEM((GLA_DK, GLA_DV), F32)],
        compiler_params=pltpu.CompilerParams(
            dimension_semantics=("arbitrary", "arbitrary"),
            vmem_limit_bytes=_vmem_limit(block_bytes, scratch_bytes)),
        name="gla",
    )(qa, ka, va, gf, gb, za, gn, masks, sgn)


def _attn_body(qt_ref, k_ref, vt_ref, z_ref, o_ref, s_ref, acc_ref):
    t = k_ref.shape[1]
    tq = ATT_Q_ROWS
    nq = t // tq
    kc = ATT_KEY_CHUNK
    nk = t // kc
    cols = ATT_GROUP * tq

    def load_q(i):
        qt = qt_ref[:, pl.ds(pl.multiple_of(i * tq, tq), tq)]
        return jnp.concatenate([qt[ATT_HEAD_DIM * j:ATT_HEAD_DIM * (j + 1), :] for j in range(ATT_GROUP)], axis=1)

    def scores(q4t, c, m, buf):
        sl = pl.ds(pl.multiple_of(c * kc, kc), kc)
        s = jnp.dot(k_ref[0, sl, :], q4t, preferred_element_type=F32)
        s_ref[buf] = s
        return jnp.maximum(m, jnp.max(s, axis=0, keepdims=True))

    def weighted(c, m_old, m_new, buf):
        sl = pl.ds(pl.multiple_of(c * kc, kc), kc)
        p = jnp.exp2(s_ref[buf] - m_new).astype(BF16)
        acc_ref[...] = (jnp.exp2(m_old - m_new) * acc_ref[...]
                        + jnp.dot(vt_ref[:, sl], p, preferred_element_type=F32))

    m_init = jnp.full((1, cols), -jnp.inf, F32)

    def tile(i, m_first):
        q4t = load_q(i)
        acc_ref[...] = jnp.zeros_like(acc_ref)

        def pair(j, carry):
            m_a, m_b = carry
            c = 2 * j
            m_c = scores(q4t, c + 1, m_b, 1)
            weighted(c, m_a, m_b, 0)
            m_d = scores(q4t, c + 2, m_c, 0)
            weighted(c + 1, m_b, m_c, 1)
            return m_c, m_d

        m_a, m_b = lax.fori_loop(0, nk // 2 - 1, pair, (m_init, m_first), unroll=True)
        m_c = scores(q4t, nk - 1, m_b, 1)
        weighted(nk - 2, m_a, m_b, 0)
        m_next = scores(load_q(jnp.minimum(i + 1, nq - 1)), 0, m_init, 0)
        weighted(nk - 1, m_b, m_c, 1)
        acc = acc_ref[...]
        ot = acc[0:ATT_HEAD_DIM, :] / acc[ATT_HEAD_DIM:ATT_HEAD_DIM + 1, :]
        o = jnp.concatenate([jnp.transpose(ot[:, tq * j:tq * (j + 1)]) for j in range(ATT_GROUP)], axis=1)
        rows = pl.ds(pl.multiple_of(i * tq, tq), tq)
        o_ref[rows, :] = (o * _silu(z_ref[rows, :].astype(F32))).astype(BF16)
        return m_next

    lax.fori_loop(0, nq, tile, scores(load_q(0), 0, m_init, 0))


def _attn(qbt, kb, vbt, zb, batch, seq_len):
    n = zb.shape[0]
    t = seq_len
    tq = ATT_Q_ROWS
    width = ATT_GROUP * ATT_HEAD_DIM
    ospec = pl.BlockSpec((t, width), lambda b, g: (b, g))
    block_bytes = (3 * _nbytes((t, width), BF16) + _nbytes((t, V7X_LANES), BF16)
                   + _nbytes((2 * ATT_HEAD_DIM, t), BF16))
    scratch_bytes = (_nbytes((2, ATT_KEY_CHUNK, ATT_GROUP * tq), F32)
                     + _nbytes((2 * ATT_HEAD_DIM, ATT_GROUP * tq), F32))
    return pl.pallas_call(
        _attn_body,
        grid=(batch, ATT_KV_HEADS),
        in_specs=[pl.BlockSpec((width, t), lambda b, g: (g, b)),
                  pl.BlockSpec((1, t, ATT_HEAD_DIM), lambda b, g: (g, b, 0)),
                  pl.BlockSpec((2 * ATT_HEAD_DIM, t), lambda b, g: (g, b)),
                  ospec],
        out_specs=ospec,
        out_shape=jax.ShapeDtypeStruct((n, ATT_Q_DIM), BF16),
        scratch_shapes=[pltpu.VMEM((2, ATT_KEY_CHUNK, ATT_GROUP * tq), F32),
                        pltpu.VMEM((2 * ATT_HEAD_DIM, ATT_GROUP * tq), F32)],
        compiler_params=pltpu.CompilerParams(
            dimension_semantics=("arbitrary", "arbitrary"),
            vmem_limit_bytes=_vmem_limit(block_bytes, scratch_bytes)),
        name="attn",
    )(qbt, kb, vbt, zb)


def _outproj_body(oa_ref, ob_ref, ma_ref, mb_ref, x_ref, wa_ref, wb_ref, wo_ref, fg_ref, o_ref, *, final):
    ya = jnp.dot(oa_ref[...], wa_ref[...], preferred_element_type=F32)
    yb = jnp.dot(ob_ref[...], wb_ref[...], preferred_element_type=F32)
    merged = (jax.nn.sigmoid(ma_ref[...].astype(F32)) * ya
              + jax.nn.sigmoid(mb_ref[...].astype(F32)) * yb)
    y = x_ref[...] + jnp.dot(merged.astype(BF16), wo_ref[...], preferred_element_type=F32)
    if final:
        ms = jnp.mean(y * y, axis=-1, keepdims=True)
        y = y * lax.rsqrt(ms + EPS) * fg_ref[...]
    o_ref[...] = y


def _outproj(oa, ob, ma, mb, x, lw, final_gain, final):
    n = x.shape[0]
    tm = OUTPROJ_ROWS
    row = lambda i: (i, 0)
    const = lambda i: (0, 0)
    act = pl.BlockSpec((tm, D_MODEL), row)
    wspec = pl.BlockSpec((D_MODEL, D_MODEL), const, pipeline_mode=pl.Buffered(1))
    block_bytes = 4 * _nbytes((tm, D_MODEL), BF16) + 2 * _nbytes((tm, D_MODEL), F32)
    weight_bytes = 3 * _nbytes((D_MODEL, D_MODEL), BF16)
    return pl.pallas_call(
        functools.partial(_outproj_body, final=final),
        grid=(n // tm,),
        in_specs=[act, act, act, act, act, wspec, wspec, wspec,
                  pl.BlockSpec((1, D_MODEL), const, pipeline_mode=pl.Buffered(1))],
        out_specs=act,
        out_shape=jax.ShapeDtypeStruct((n, D_MODEL), F32),
        compiler_params=pltpu.CompilerParams(
            dimension_semantics=("arbitrary",),
            vmem_limit_bytes=_vmem_limit(block_bytes, weight_bytes)),
        name="outproj",
    )(oa, ob, ma, mb, x, lw["w_a"], lw["w_b"], lw["w_o"], final_gain)


def _rope_tables(seq_len):
    rows = seq_len // GRID_W
    r = jnp.repeat(jnp.arange(rows, dtype=F32), GRID_W)
    c = jnp.tile(jnp.arange(GRID_W, dtype=F32), rows)
    nf = ROPE_AXIS_DIM // 2
    inv = ROPE_THETA ** (-jnp.arange(nf, dtype=F32) / nf)
    ang_r = r[:, None] * inv
    ang_c = c[:, None] * inv
    cos = jnp.concatenate([jnp.cos(ang_r), jnp.cos(ang_r), jnp.cos(ang_c), jnp.cos(ang_c)], axis=-1)
    sin = jnp.concatenate([-jnp.sin(ang_r), jnp.sin(ang_r), -jnp.sin(ang_c), jnp.sin(ang_c)], axis=-1)
    reps = V7X_LANES // ATT_HEAD_DIM
    return jnp.tile(cos, (1, reps)), jnp.tile(sin, (1, reps))


def _layer_weights(l, norm_g, w_in, w_gate_f, b_gate_f, w_gate_b, b_gate_b, gla_norm_g, q_norm_g, k_norm_g,
                   w_branch_a, w_branch_b, w_out):
    pts = np.cumsum((0,) + IN_SIZES)
    col = lambda i: w_in[l][:, pts[i]:pts[i + 1]]
    w = w_in[l]
    w_gate = jnp.zeros((D_MODEL, V7X_LANES), F32).at[:, :2 * GATE_RANK].set(w[:, pts[3]:pts[5]])
    w_gf = jnp.zeros((V7X_LANES, GLA_KEY_DIM), F32).at[:GATE_RANK].set(w_gate_f[l])
    w_gb = jnp.zeros((V7X_LANES, GLA_KEY_DIM), F32).at[GATE_RANK:2 * GATE_RANK].set(w_gate_b[l])
    return {
        "norm_g": norm_g[l].reshape(1, D_MODEL),
        "w_gla": jnp.concatenate([col(0), col(1), col(2), col(5)], axis=1).astype(BF16),
        "w_gate": w_gate.astype(BF16),
        "w_gf": w_gf.astype(BF16),
        "w_gb": w_gb.astype(BF16),
        "b_gf": b_gate_f[l].reshape(1, GLA_KEY_DIM),
        "b_gb": b_gate_b[l].reshape(1, GLA_KEY_DIM),
        "w_att": jnp.concatenate([col(6), col(7), col(8), col(9)], axis=1).astype(BF16),
        "w_m": jnp.concatenate([col(10), col(11)], axis=1).astype(BF16),
        "q_gain": (jnp.tile(q_norm_g[l], ATT_Q_HEADS) * ATT_Q_SCALE).reshape(1, ATT_Q_DIM),
        "k_gain": jnp.tile(k_norm_g[l], ATT_KV_HEADS).reshape(1, ATT_KV_DIM),
        "gla_gain": gla_norm_g[l].reshape(1, GLA_DV),
        "w_a": w_branch_a[l].astype(BF16),
        "w_b": w_branch_b[l].astype(BF16),
        "w_o": w_out[l].astype(BF16),
    }


def _trunk(x3, layers, final_gain, masks, sgn, bd):
    batch, seq_len, _ = x3.shape
    x = x3.reshape(batch * seq_len, D_MODEL)
    cos, sin = _rope_tables(seq_len)
    for l, lw in enumerate(layers):
        qa, ka, va, za, gf, gb, qbt, kb, vbt, zb, ma, mb = _inproj(x, lw, cos, sin, bd, seq_len)
        oa = _gla(qa, ka, va, gf, gb, za, lw["gla_gain"], masks, sgn, batch, seq_len)
        ob = _attn(qbt, kb, vbt, zb, batch, seq_len)
        x = _outproj(oa, ob, ma, mb, x, lw, final_gain, final=(l == len(layers) - 1))
    return x.reshape(batch, seq_len, D_MODEL)


def kernel(x_prompt, x_sample, norm_g, w_in, w_gate_f, b_gate_f, w_gate_b, b_gate_b, gla_norm_g, q_norm_g,
           k_norm_g, w_branch_a, w_branch_b, w_out, final_norm_g):
    layers = [_layer_weights(l, norm_g, w_in, w_gate_f, b_gate_f, w_gate_b, b_gate_b, gla_norm_g, q_norm_g,
                             k_norm_g, w_branch_a, w_branch_b, w_out) for l in range(DEPTH)]
    final_gain = final_norm_g.reshape(1, D_MODEL)
    masks, sgn = _gla_masks()
    head = np.arange(V7X_MXU_DIM) // ATT_HEAD_DIM
    bd = jnp.asarray((head[:, None] == head[None, :]).astype(np.float32) / ATT_HEAD_DIM, dtype=BF16)
    y_prompt = _trunk(x_prompt, layers, final_gain, masks, sgn, bd)
    y_sample = _trunk(x_sample, layers, final_gain, masks, sgn, bd)
    return (y_prompt, y_sample)
```

```python
import functools

import numpy as np
import jax
import jax.numpy as jnp
from jax import lax
from jax.experimental import pallas as pl
from jax.experimental.pallas import tpu as pltpu

F32, BF16 = jnp.float32, jnp.bfloat16

D_MODEL = 1024
DEPTH = 4
EPS = 1e-6
GRID_W = 64
GLA_HEADS = 4
GLA_KEY_DIM = D_MODEL // 2
GLA_VALUE_DIM = D_MODEL
GLA_DK = GLA_KEY_DIM // GLA_HEADS
GLA_DV = GLA_VALUE_DIM // GLA_HEADS
GATE_RANK = 16
GATE_NORMALIZER = 16.0
ATT_HEAD_DIM = 64
ATT_Q_HEADS = D_MODEL // ATT_HEAD_DIM
ATT_KV_HEADS = 4
ATT_GROUP = ATT_Q_HEADS // ATT_KV_HEADS
ATT_Q_DIM = ATT_Q_HEADS * ATT_HEAD_DIM
ATT_KV_DIM = ATT_KV_HEADS * ATT_HEAD_DIM
ROPE_AXIS_DIM = ATT_HEAD_DIM // 2
ROPE_THETA = 10000.0
IN_SIZES = (GLA_KEY_DIM, GLA_KEY_DIM, GLA_VALUE_DIM, GATE_RANK, GATE_RANK, GLA_VALUE_DIM,
            ATT_Q_DIM, ATT_KV_DIM, ATT_KV_DIM, ATT_Q_DIM, D_MODEL, D_MODEL)

V7X_LANES = 128
V7X_SUBLANES = 8
V7X_MXU_DIM = 256
V7X_VMEM_BYTES = 64 * 1024 * 1024
V7X_VMEM_BUDGET = V7X_VMEM_BYTES - 8 * 1024 * 1024

INPROJ_ROWS = 512
OUTPROJ_ROWS = 512
GLA_CHUNK = 64
GLA_LEVELS = (1, 2, 4, 8, 16, 32)
GLA_CHUNKS_PER_STEP = 8
GLA_FINAL_ROWS = 256
GLA_BOUNDED_DECAY = 40.0
GLA_MASK_DIAG = 0
GLA_MASK_TREE_FWD = 1
GLA_MASK_TREE_BWD = GLA_MASK_TREE_FWD + len(GLA_LEVELS)
GLA_MASK_TRI_FWD = GLA_MASK_TREE_BWD + len(GLA_LEVELS)
GLA_MASK_TRI_BWD = GLA_MASK_TRI_FWD + 1
ATT_Q_SCALE = ATT_HEAD_DIM ** -0.5 * float(np.log2(np.e))
ATT_Q_ROWS = 256
ATT_KEY_CHUNK = 512

_NT = (((1,), (1,)), ((), ()))
_TN = (((0,), (0,)), ((), ()))


def _vmem_limit(block_bytes, scratch_bytes=0):
    want = 2 * block_bytes + scratch_bytes + 16 * 1024 * 1024
    return int(min(want, V7X_VMEM_BUDGET))


def _nbytes(shape, dtype):
    return int(np.prod(shape)) * jnp.dtype(dtype).itemsize


def _log_sigmoid(z):
    return jnp.minimum(z, 0.0) - jnp.log(1.0 + jnp.exp(-jnp.abs(z)))


def _sigmoid(z):
    return 0.5 * jnp.tanh(0.5 * z) + 0.5


def _silu(z):
    return z * _sigmoid(z)


def _inproj_body(x_ref, ng_ref, wgla_ref, wgate_ref, wgf_ref, wgb_ref, bgf_ref, bgb_ref,
                 watt_ref, wm_ref, qng_ref, kng_ref, cos_ref, sin_ref, bd_ref,
                 qa_ref, ka_ref, va_ref, za_ref, gf_ref, gb_ref,
                 qbt_ref, kb_ref, vbt_ref, zb_ref, ma_ref, mb_ref):
    x = x_ref[...]
    ms = jnp.mean(x * x, axis=-1, keepdims=True)
    h = (x * lax.rsqrt(ms + EPS) * ng_ref[...]).astype(BF16)

    def proj(w_ref, lo, hi):
        return jnp.dot(h, w_ref[:, lo:hi], preferred_element_type=F32)

    pg = proj(wgate_ref, 0, V7X_LANES).astype(BF16)
    gate_rows = pg.shape[0] // 4

    def decay_gates(part):
        rows = slice(gate_rows * part, gate_rows * (part + 1))
        zf = jnp.dot(pg[rows], wgf_ref[...], preferred_element_type=F32) + bgf_ref[...]
        zb = jnp.dot(pg[rows], wgb_ref[...], preferred_element_type=F32) + bgb_ref[...]
        gf_ref[rows, :] = _log_sigmoid(zf) * (1.0 / GATE_NORMALIZER)
        gb_ref[rows, :] = _log_sigmoid(zb) * (1.0 / GATE_NORMALIZER)

    cos = cos_ref[...]
    sin = sin_ref[...]
    bd = bd_ref[...]
    lane = lax.broadcasted_iota(jnp.int32, cos.shape, 1)
    first_half = (lane & (ROPE_AXIS_DIM // 2)) == 0

    def norm_rope(p, gain):
        sq = (p * p).astype(BF16)
        msq = jnp.dot(sq, bd, preferred_element_type=F32)
        r = lax.rsqrt(msq + EPS)
        xg = p * gain
        outs = []
        for s in range(2):
            xs = xg[:, V7X_LANES * s:V7X_LANES * (s + 1)]
            partner = jnp.where(first_half,
                                pltpu.roll(xs, V7X_LANES - ROPE_AXIS_DIM // 2, 1),
                                pltpu.roll(xs, ROPE_AXIS_DIM // 2, 1))
            outs.append(xs * cos + partner * sin)
        return jnp.concatenate(outs, axis=1) * r

    def attn_q(c):
        lo = V7X_MXU_DIM * c
        p = proj(watt_ref, lo, lo + V7X_MXU_DIM)
        y = norm_rope(p, qng_ref[:, lo:lo + V7X_MXU_DIM])
        qbt_ref[lo:lo + V7X_MXU_DIM, :] = jnp.transpose(y).astype(BF16)

    def attn_kv():
        kk = norm_rope(proj(watt_ref, 1024, 1280), kng_ref[...]).astype(BF16)
        vt = jnp.transpose(proj(watt_ref, 1280, 1536)).astype(BF16)
        ones = jnp.ones((ATT_HEAD_DIM, vt.shape[1]), BF16)
        for g in range(ATT_KV_HEADS):
            kb_ref[g] = kk[:, ATT_HEAD_DIM * g:ATT_HEAD_DIM * (g + 1)]
            vbt_ref[2 * ATT_HEAD_DIM * g:2 * ATT_HEAD_DIM * g + ATT_HEAD_DIM, :] = (
                vt[ATT_HEAD_DIM * g:ATT_HEAD_DIM * (g + 1), :])
            vbt_ref[2 * ATT_HEAD_DIM * g + ATT_HEAD_DIM:2 * ATT_HEAD_DIM * (g + 1), :] = ones

    qa_ref[...] = (proj(wgla_ref, 0, 512) * (GLA_DK ** -0.5)).astype(BF16)
    attn_kv()
    ka_ref[...] = proj(wgla_ref, 512, 1024).astype(BF16)
    attn_q(0)
    va_ref[...] = proj(wgla_ref, 1024, 2048).astype(BF16)
    decay_gates(0)
    za_ref[...] = proj(wgla_ref, 2048, 3072).astype(BF16)
    attn_q(1)
    decay_gates(1)
    zb_ref[...] = proj(watt_ref, 1536, 2560).astype(BF16)
    attn_q(2)
    decay_gates(2)
    ma_ref[...] = proj(wm_ref, 0, 1024).astype(BF16)
    attn_q(3)
    decay_gates(3)
    mb_ref[...] = proj(wm_ref, 1024, 2048).astype(BF16)


def _inproj(x, lw, cos, sin, bd, seq_len):
    n = x.shape[0]
    tm = INPROJ_ROWS
    pos_blocks = seq_len // tm
    row = lambda i: (i, 0)
    const = lambda i: (0, 0)

    def wspec(shape):
        return pl.BlockSpec(shape, const, pipeline_mode=pl.Buffered(1))

    in_specs = [
        pl.BlockSpec((tm, D_MODEL), row),
        wspec((1, D_MODEL)),
        wspec((D_MODEL, 3072)),
        wspec((D_MODEL, V7X_LANES)),
        wspec((V7X_LANES, GLA_KEY_DIM)),
        wspec((V7X_LANES, GLA_KEY_DIM)),
        wspec((1, GLA_KEY_DIM)),
        wspec((1, GLA_KEY_DIM)),
        wspec((D_MODEL, 2560)),
        wspec((D_MODEL, 2048)),
        wspec((1, ATT_Q_DIM)),
        wspec((1, ATT_KV_DIM)),
        pl.BlockSpec((tm, V7X_LANES), lambda i: (i % pos_blocks, 0)),
        pl.BlockSpec((tm, V7X_LANES), lambda i: (i % pos_blocks, 0)),
        wspec((V7X_MXU_DIM, V7X_MXU_DIM)),
    ]
    out_shape = [
        jax.ShapeDtypeStruct((n, GLA_KEY_DIM), BF16),
        jax.ShapeDtypeStruct((n, GLA_KEY_DIM), BF16),
        jax.ShapeDtypeStruct((n, GLA_VALUE_DIM), BF16),
        jax.ShapeDtypeStruct((n, GLA_VALUE_DIM), BF16),
        jax.ShapeDtypeStruct((n, GLA_KEY_DIM), F32),
        jax.ShapeDtypeStruct((n, GLA_KEY_DIM), F32),
        jax.ShapeDtypeStruct((ATT_Q_DIM, n), BF16),
        jax.ShapeDtypeStruct((ATT_KV_HEADS, n, ATT_HEAD_DIM), BF16),
        jax.ShapeDtypeStruct((2 * ATT_KV_DIM, n), BF16),
        jax.ShapeDtypeStruct((n, ATT_Q_DIM), BF16),
        jax.ShapeDtypeStruct((n, D_MODEL), BF16),
        jax.ShapeDtypeStruct((n, D_MODEL), BF16),
    ]
    out_specs = [
        pl.BlockSpec((tm, GLA_KEY_DIM), row),
        pl.BlockSpec((tm, GLA_KEY_DIM), row),
        pl.BlockSpec((tm, GLA_VALUE_DIM), row),
        pl.BlockSpec((tm, GLA_VALUE_DIM), row),
        pl.BlockSpec((tm, GLA_KEY_DIM), row),
        pl.BlockSpec((tm, GLA_KEY_DIM), row),
        pl.BlockSpec((ATT_Q_DIM, tm), lambda i: (0, i)),
        pl.BlockSpec((ATT_KV_HEADS, tm, ATT_HEAD_DIM), lambda i: (0, i, 0)),
        pl.BlockSpec((2 * ATT_KV_DIM, tm), lambda i: (0, i)),
        pl.BlockSpec((tm, ATT_Q_DIM), row),
        pl.BlockSpec((tm, D_MODEL), row),
        pl.BlockSpec((tm, D_MODEL), row),
    ]
    weight_bytes = _nbytes((D_MODEL, 3072 + V7X_LANES + 2560 + 2048), BF16)
    block_bytes = (_nbytes((tm, D_MODEL), F32) + _nbytes((tm, 7 * 1024 + 512), BF16)
                   + 2 * _nbytes((tm, GLA_KEY_DIM), F32) + 2 * _nbytes((tm, V7X_LANES), F32))
    return pl.pallas_call(
        _inproj_body,
        grid=(n // tm,),
        in_specs=in_specs,
        out_specs=out_specs,
        out_shape=out_shape,
        compiler_params=pltpu.CompilerParams(
            dimension_semantics=("arbitrary",),
            vmem_limit_bytes=_vmem_limit(block_bytes, weight_bytes)),
        name="inproj",
    )(x, lw["norm_g"], lw["w_gla"], lw["w_gate"], lw["w_gf"], lw["w_gb"], lw["b_gf"], lw["b_gb"],
      lw["w_att"], lw["w_m"], lw["q_gain"], lw["k_gain"], cos, sin, bd)


def _gla_boundary(beta, m, reverse):
    c, width = beta.shape
    two_m = 2 * m
    off = m if reverse else m - 1
    pieces = []
    if two_m >= V7X_SUBLANES:
        for v in range(c // V7X_SUBLANES):
            r = (V7X_SUBLANES * v // two_m) * two_m + off
            pieces.append(jnp.broadcast_to(beta[r:r + 1, :], (V7X_SUBLANES, width)))
    else:
        sub = lax.broadcasted_iota(jnp.int32, (V7X_SUBLANES, width), 0)
        per = V7X_SUBLANES // two_m
        for v in range(c // V7X_SUBLANES):
            rows = [jnp.broadcast_to(beta[V7X_SUBLANES * v + two_m * j + off:V7X_SUBLANES * v + two_m * j + off + 1, :],
                                     (V7X_SUBLANES, width)) for j in range(per)]
            acc = rows[-1]
            for j in range(per - 2, -1, -1):
                acc = jnp.where(sub < two_m * (j + 1), rows[j], acc)
            pieces.append(acc)
    return jnp.concatenate(pieces, axis=0)


def _gla_tree_scores(q, k, g, beta, mask_ref, sgn_ref, reverse):
    row = lax.broadcasted_iota(jnp.int32, g.shape, 0)
    nlev = len(GLA_LEVELS)
    mask_base = GLA_MASK_TREE_BWD if reverse else GLA_MASK_TREE_FWD
    sgn_base = (nlev - 1) if reverse else 0
    a = None
    if not reverse:
        a = mask_ref[GLA_MASK_DIAG] * lax.dot_general(q, k, _NT, preferred_element_type=F32)
    for li, m in enumerate(GLA_LEVELS):
        if m == 1:
            parity = (row & 1) == (0 if reverse else 1)
            f = jnp.exp(jnp.where(parity, g, 0.0))
        else:
            f = jnp.exp2((beta - _gla_boundary(beta, m, reverse)) * sgn_ref[sgn_base + li - 1])
        fb = f.astype(BF16)
        am = mask_ref[mask_base + li] * lax.dot_general(q * fb, k * fb, _NT, preferred_element_type=F32)
        a = am if a is None else a + am
    return a


def _gla_chunk_local(q, k, v, g, mask_ref, sgn_ref, reverse, bounded):
    c = GLA_CHUNK
    row = lax.broadcasted_iota(jnp.int32, g.shape, 0)
    beta = g
    for sh in GLA_LEVELS:
        if reverse:
            beta = beta + jnp.where(row < c - sh, pltpu.roll(beta, c - sh, 0), 0.0)
        else:
            beta = beta + jnp.where(row >= sh, pltpu.roll(beta, sh, 0), 0.0)
    qd = q * jnp.exp(beta).astype(BF16)
    tot = beta[0:1, :] if reverse else beta[c - 1:c, :]
    kd = k * jnp.exp(tot - beta).astype(BF16)
    decay_col = jnp.transpose(jnp.broadcast_to(jnp.exp(tot), (V7X_SUBLANES, GLA_DK)))[:, 0:1]
    if bounded:
        kg = k * jnp.exp(-beta).astype(BF16)
        a = (mask_ref[GLA_MASK_TRI_BWD if reverse else GLA_MASK_TRI_FWD]
             * lax.dot_general(qd, kg, _NT, preferred_element_type=F32))
    else:
        a = _gla_tree_scores(q, k, g, beta, mask_ref, sgn_ref, reverse)
    kv = lax.dot_general(kd, v, _TN, preferred_element_type=F32)
    return qd, a.astype(BF16), kv, decay_col


def _gla_chunk_state(local, v, s_ref):
    qd, a, kv, decay_col = local
    s = s_ref[...]
    o = jnp.dot(qd, s.astype(BF16), preferred_element_type=F32)
    s_ref[...] = decay_col * s + kv
    return o + jnp.dot(a, v, preferred_element_type=F32)


def _gla_body(q_ref, k_ref, v_ref, gf_ref, gb_ref, za_ref, gn_ref, mask_ref, sgn_ref, o_ref,
              acc_ref, sf_ref, sb_ref):
    t = q_ref.shape[0]
    c = GLA_CHUNK
    n = t // c
    acc_ref[...] = jnp.zeros_like(acc_ref)
    sf_ref[...] = jnp.zeros_like(sf_ref)
    sb_ref[...] = jnp.zeros_like(sb_ref)

    chunk_tot = jnp.minimum(jnp.sum(gf_ref[...].reshape(n, c, GLA_DK), axis=1),
                            jnp.sum(gb_ref[...].reshape(n, c, GLA_DK), axis=1))
    bounded = jnp.min(chunk_tot) >= -GLA_BOUNDED_DECAY

    def scan(is_bounded):
        def step(i, carry):
            work = []
            for u in range(GLA_CHUNKS_PER_STEP):
                j = i * GLA_CHUNKS_PER_STEP + u
                for reverse, g_ref, s_ref in ((False, gf_ref, sf_ref), (True, gb_ref, sb_ref)):
                    sl = pl.ds(pl.multiple_of(((n - 1 - j) if reverse else j) * c, c), c)
                    local = _gla_chunk_local(q_ref[sl, :], k_ref[sl, :], v_ref[sl, :], g_ref[sl, :],
                                             mask_ref, sgn_ref, reverse, is_bounded)
                    work.append((sl, local, s_ref))
            for sl, local, s_ref in work:
                acc_ref[sl, :] += _gla_chunk_state(local, v_ref[sl, :], s_ref)
            return carry

        lax.fori_loop(0, n // GLA_CHUNKS_PER_STEP, step, 0)

    @pl.when(bounded)
    def _():
        scan(True)

    @pl.when(jnp.logical_not(bounded))
    def _():
        scan(False)

    rows = GLA_FINAL_ROWS

    def fin(i, carry):
        sl = pl.ds(pl.multiple_of(i * rows, rows), rows)
        o = acc_ref[sl, :]
        ms = jnp.mean(o * o, axis=-1, keepdims=True)
        y = o * lax.rsqrt(ms + EPS) * gn_ref[...]
        o_ref[sl, :] = (y * _silu(za_ref[sl, :].astype(F32))).astype(BF16)
        return carry

    lax.fori_loop(0, t // rows, fin, 0)


def _gla_masks():
    c = GLA_CHUNK
    ri = np.arange(c)[:, None]
    ci = np.arange(c)[None, :]
    masks = [ri == ci]
    for m in GLA_LEVELS:
        same = (ri // (2 * m)) == (ci // (2 * m))
        masks.append(same & ((ri % (2 * m)) >= m) & ((ci % (2 * m)) < m))
    for m in GLA_LEVELS:
        same = (ri // (2 * m)) == (ci // (2 * m))
        masks.append(same & ((ri % (2 * m)) < m) & ((ci % (2 * m)) >= m))
    masks.append(ri >= ci)
    masks.append(ri < ci)
    sgn = []
    row = np.arange(c)[:, None] * np.ones((1, GLA_DK), np.int64)
    for reverse in (False, True):
        for m in GLA_LEVELS[1:]:
            upper = (row % (2 * m)) >= m
            sgn.append(np.where(upper != reverse, np.log2(np.e), -np.log2(np.e)))
    return jnp.asarray(np.stack(masks).astype(np.float32)), jnp.asarray(np.stack(sgn).astype(np.float32))


def _gla(qa, ka, va, gf, gb, za, gn, masks, sgn, batch, seq_len):
    n = qa.shape[0]
    t = seq_len
    assert t % (GLA_CHUNK * GLA_CHUNKS_PER_STEP) == 0 and t % GLA_FINAL_ROWS == 0, t
    kspec = pl.BlockSpec((t, GLA_DK), lambda b, h: (b, h))
    vspec = pl.BlockSpec((t, GLA_DV), lambda b, h: (b, h))
    nm = masks.shape[0]
    block_bytes = (2 * _nbytes((t, GLA_DK), BF16) + 3 * _nbytes((t, GLA_DV), BF16)
                   + 2 * _nbytes((t, GLA_DK), F32) + _nbytes(masks.shape, F32) + _nbytes(sgn.shape, F32))
    scratch_bytes = _nbytes((t, GLA_DV), F32) + 2 * _nbytes((GLA_DK, GLA_DV), F32)
    return pl.pallas_call(
        _gla_body,
        grid=(batch, GLA_HEADS),
        in_specs=[kspec, kspec, vspec, kspec, kspec, vspec,
                  pl.BlockSpec((1, GLA_DV), lambda b, h: (0, 0)),
                  pl.BlockSpec((nm, GLA_CHUNK, GLA_CHUNK), lambda b, h: (0, 0, 0)),
                  pl.BlockSpec(sgn.shape, lambda b, h: (0, 0, 0))],
        out_specs=vspec,
        out_shape=jax.ShapeDtypeStruct((n, GLA_VALUE_DIM), BF16),
        scratch_shapes=[pltpu.VMEM((t, GLA_DV), F32),
                        pltpu.VMEM((GLA_DK, GLA_DV), F32),
                        pltpu.VMEM((GLA_DK, GLA_DV), F32)],
        compiler_params=pltpu.CompilerParams(
            dimension_semantics=("arbitrary", "arbitrary"),
            vmem_limit_bytes=_vmem_limit(block_bytes, scratch_bytes)),
        name="gla",
    )(qa, ka, va, gf, gb, za, gn, masks, sgn)


def _attn_body(qt_ref, k_ref, vt_ref, z_ref, o_ref, s_ref, acc_ref):
    t = k_ref.shape[1]
    tq = ATT_Q_ROWS
    nq = t // tq
    kc = ATT_KEY_CHUNK
    nk = t // kc
    cols = ATT_GROUP * tq

    def load_q(i):
        qt = qt_ref[:, pl.ds(pl.multiple_of(i * tq, tq), tq)]
        return jnp.concatenate([qt[ATT_HEAD_DIM * j:ATT_HEAD_DIM * (j + 1), :] for j in range(ATT_GROUP)], axis=1)

    def scores(q4t, c, m, buf):
        sl = pl.ds(pl.multiple_of(c * kc, kc), kc)
        s = jnp.dot(k_ref[0, sl, :], q4t, preferred_element_type=F32)
        s_ref[buf] = s
        return jnp.maximum(m, jnp.max(s, axis=0, keepdims=True))

    def weighted(c, m_old, m_new, buf):
        sl = pl.ds(pl.multiple_of(c * kc, kc), kc)
        p = jnp.exp2(s_ref[buf] - m_new).astype(BF16)
        acc_ref[...] = (jnp.exp2(m_old - m_new) * acc_ref[...]
                        + jnp.dot(vt_ref[:, sl], p, preferred_element_type=F32))

    m_init = jnp.full((1, cols), -jnp.inf, F32)

    def tile(i, m_first):
        q4t = load_q(i)
        acc_ref[...] = jnp.zeros_like(acc_ref)

        def pair(j, carry):
            m_a, m_b = carry
            c = 2 * j
            m_c = scores(q4t, c + 1, m_b, 1)
            weighted(c, m_a, m_b, 0)
            m_d = scores(q4t, c + 2, m_c, 0)
            weighted(c + 1, m_b, m_c, 1)
            return m_c, m_d

        m_a, m_b = lax.fori_loop(0, nk // 2 - 1, pair, (m_init, m_first), unroll=True)
        m_c = scores(q4t, nk - 1, m_b, 1)
        weighted(nk - 2, m_a, m_b, 0)
        m_next = scores(load_q(jnp.minimum(i + 1, nq - 1)), 0, m_init, 0)
        weighted(nk - 1, m_b, m_c, 1)
        acc = acc_ref[...]
        ot = acc[0:ATT_HEAD_DIM, :] / acc[ATT_HEAD_DIM:ATT_HEAD_DIM + 1, :]
        o = jnp.concatenate([jnp.transpose(ot[:, tq * j:tq * (j + 1)]) for j in range(ATT_GROUP)], axis=1)
        rows = pl.ds(pl.multiple_of(i * tq, tq), tq)
        o_ref[rows, :] = (o * _silu(z_ref[rows, :].astype(F32))).astype(BF16)
        return m_next

    lax.fori_loop(0, nq, tile, scores(load_q(0), 0, m_init, 0), unroll=2)


def _attn(qbt, kb, vbt, zb, batch, seq_len):
    n = zb.shape[0]
    t = seq_len
    tq = ATT_Q_ROWS
    width = ATT_GROUP * ATT_HEAD_DIM
    ospec = pl.BlockSpec((t, width), lambda b, g: (b, g))
    block_bytes = (3 * _nbytes((t, width), BF16) + _nbytes((t, V7X_LANES), BF16)
                   + _nbytes((2 * ATT_HEAD_DIM, t), BF16))
    scratch_bytes = (_nbytes((2, ATT_KEY_CHUNK, ATT_GROUP * tq), F32)
                     + _nbytes((2 * ATT_HEAD_DIM, ATT_GROUP * tq), F32))
    return pl.pallas_call(
        _attn_body,
        grid=(batch, ATT_KV_HEADS),
        in_specs=[pl.BlockSpec((width, t), lambda b, g: (g, b)),
                  pl.BlockSpec((1, t, ATT_HEAD_DIM), lambda b, g: (g, b, 0)),
                  pl.BlockSpec((2 * ATT_HEAD_DIM, t), lambda b, g: (g, b)),
                  ospec],
        out_specs=ospec,
        out_shape=jax.ShapeDtypeStruct((n, ATT_Q_DIM), BF16),
        scratch_shapes=[pltpu.VMEM((2, ATT_KEY_CHUNK, ATT_GROUP * tq), F32),
                        pltpu.VMEM((2 * ATT_HEAD_DIM, ATT_GROUP * tq), F32)],
        compiler_params=pltpu.CompilerParams(
            dimension_semantics=("arbitrary", "arbitrary"),
            vmem_limit_bytes=_vmem_limit(block_bytes, scratch_bytes)),
        name="attn",
    )(qbt, kb, vbt, zb)


def _outproj_body(oa_ref, ob_ref, ma_ref, mb_ref, x_ref, wa_ref, wb_ref, wo_ref, fg_ref, o_ref, *, final):
    ya = jnp.dot(oa_ref[...], wa_ref[...], preferred_element_type=F32)
    yb = jnp.dot(ob_ref[...], wb_ref[...], preferred_element_type=F32)
    merged = (_sigmoid(ma_ref[...].astype(F32)) * ya
              + _sigmoid(mb_ref[...].astype(F32)) * yb)
    y = x_ref[...] + jnp.dot(merged.astype(BF16), wo_ref[...], preferred_element_type=F32)
    if final:
        ms = jnp.mean(y * y, axis=-1, keepdims=True)
        y = y * lax.rsqrt(ms + EPS) * fg_ref[...]
    o_ref[...] = y


def _outproj(oa, ob, ma, mb, x, lw, final_gain, final):
    n = x.shape[0]
    tm = OUTPROJ_ROWS
    row = lambda i: (i, 0)
    const = lambda i: (0, 0)
    act = pl.BlockSpec((tm, D_MODEL), row)
    wspec = pl.BlockSpec((D_MODEL, D_MODEL), const, pipeline_mode=pl.Buffered(1))
    block_bytes = 4 * _nbytes((tm, D_MODEL), BF16) + 2 * _nbytes((tm, D_MODEL), F32)
    weight_bytes = 3 * _nbytes((D_MODEL, D_MODEL), BF16)
    return pl.pallas_call(
        functools.partial(_outproj_body, final=final),
        grid=(n // tm,),
        in_specs=[act, act, act, act, act, wspec, wspec, wspec,
                  pl.BlockSpec((1, D_MODEL), const, pipeline_mode=pl.Buffered(1))],
        out_specs=act,
        out_shape=jax.ShapeDtypeStruct((n, D_MODEL), F32),
        compiler_params=pltpu.CompilerParams(
            dimension_semantics=("arbitrary",),
            vmem_limit_bytes=_vmem_limit(block_bytes, weight_bytes)),
        name="outproj",
    )(oa, ob, ma, mb, x, lw["w_a"], lw["w_b"], lw["w_o"], final_gain)


def _rope_tables(seq_len):
    rows = seq_len // GRID_W
    r = jnp.repeat(jnp.arange(rows, dtype=F32), GRID_W)
    c = jnp.tile(jnp.arange(GRID_W, dtype=F32), rows)
    nf = ROPE_AXIS_DIM // 2
    inv = ROPE_THETA ** (-jnp.arange(nf, dtype=F32) / nf)
    ang_r = r[:, None] * inv
    ang_c = c[:, None] * inv
    cos = jnp.concatenate([jnp.cos(ang_r), jnp.cos(ang_r), jnp.cos(ang_c), jnp.cos(ang_c)], axis=-1)
    sin = jnp.concatenate([-jnp.sin(ang_r), jnp.sin(ang_r), -jnp.sin(ang_c), jnp.sin(ang_c)], axis=-1)
    reps = V7X_LANES // ATT_HEAD_DIM
    return jnp.tile(cos, (1, reps)), jnp.tile(sin, (1, reps))


def _layer_weights(l, norm_g, w_in, w_gate_f, b_gate_f, w_gate_b, b_gate_b, gla_norm_g, q_norm_g, k_norm_g,
                   w_branch_a, w_branch_b, w_out):
    pts = np.cumsum((0,) + IN_SIZES)
    col = lambda i: w_in[l][:, pts[i]:pts[i + 1]]
    w = w_in[l]
    w_gate = jnp.zeros((D_MODEL, V7X_LANES), F32).at[:, :2 * GATE_RANK].set(w[:, pts[3]:pts[5]])
    w_gf = jnp.zeros((V7X_LANES, GLA_KEY_DIM), F32).at[:GATE_RANK].set(w_gate_f[l])
    w_gb = jnp.zeros((V7X_LANES, GLA_KEY_DIM), F32).at[GATE_RANK:2 * GATE_RANK].set(w_gate_b[l])
    return {
        "norm_g": norm_g[l].reshape(1, D_MODEL),
        "w_gla": jnp.concatenate([col(0), col(1), col(2), col(5)], axis=1).astype(BF16),
        "w_gate": w_gate.astype(BF16),
        "w_gf": w_gf.astype(BF16),
        "w_gb": w_gb.astype(BF16),
        "b_gf": b_gate_f[l].reshape(1, GLA_KEY_DIM),
        "b_gb": b_gate_b[l].reshape(1, GLA_KEY_DIM),
        "w_att": jnp.concatenate([col(6), col(7), col(8), col(9)], axis=1).astype(BF16),
        "w_m": jnp.concatenate([col(10), col(11)], axis=1).astype(BF16),
        "q_gain": (jnp.tile(q_norm_g[l], ATT_Q_HEADS) * ATT_Q_SCALE).reshape(1, ATT_Q_DIM),
        "k_gain": jnp.tile(k_norm_g[l], ATT_KV_HEADS).reshape(1, ATT_KV_DIM),
        "gla_gain": gla_norm_g[l].reshape(1, GLA_DV),
        "w_a": w_branch_a[l].astype(BF16),
        "w_b": w_branch_b[l].astype(BF16),
        "w_o": w_out[l].astype(BF16),
    }


def _trunk(x3, layers, final_gain, masks, sgn, bd):
    batch, seq_len, _ = x3.shape
    x = x3.reshape(batch * seq_len, D_MODEL)
    cos, sin = _rope_tables(seq_len)
    for l, lw in enumerate(layers):
        qa, ka, va, za, gf, gb, qbt, kb, vbt, zb, ma, mb = _inproj(x, lw, cos, sin, bd, seq_len)
        oa = _gla(qa, ka, va, gf, gb, za, lw["gla_gain"], masks, sgn, batch, seq_len)
        ob = _attn(qbt, kb, vbt, zb, batch, seq_len)
        x = _outproj(oa, ob, ma, mb, x, lw, final_gain, final=(l == len(layers) - 1))
    return x.reshape(batch, seq_len, D_MODEL)


def kernel(x_prompt, x_sample, norm_g, w_in, w_gate_f, b_gate_f, w_gate_b, b_gate_b, gla_norm_g, q_norm_g,
           k_norm_g, w_branch_a, w_branch_b, w_out, final_norm_g):
    layers = [_layer_weights(l, norm_g, w_in, w_gate_f, b_gate_f, w_gate_b, b_gate_b, gla_norm_g, q_norm_g,
                             k_norm_g, w_branch_a, w_branch_b, w_out) for l in range(DEPTH)]
    final_gain = final_norm_g.reshape(1, D_MODEL)
    masks, sgn = _gla_masks()
    head = np.arange(V7X_MXU_DIM) // ATT_HEAD_DIM
    bd = jnp.asarray((head[:, None] == head[None, :]).astype(np.float32) / ATT_HEAD_DIM, dtype=BF16)
    y_prompt = _trunk(x_prompt, layers, final_gain, masks, sgn, bd)
    y_sample = _trunk(x_sample, layers, final_gain, masks, sgn, bd)
    return (y_prompt, y_sample)
```

```python
import functools

import numpy as np
import jax
import jax.numpy as jnp
from jax import lax
from jax.experimental import pallas as pl
from jax.experimental.pallas import tpu as pltpu

F32, BF16 = jnp.float32, jnp.bfloat16

D_MODEL = 1024
DEPTH = 4
EPS = 1e-6
GRID_W = 64
GLA_HEADS = 4
GLA_KEY_DIM = D_MODEL // 2
GLA_VALUE_DIM = D_MODEL
GLA_DK = GLA_KEY_DIM // GLA_HEADS
GLA_DV = GLA_VALUE_DIM // GLA_HEADS
GATE_RANK = 16
GATE_NORMALIZER = 16.0
ATT_HEAD_DIM = 64
ATT_Q_HEADS = D_MODEL // ATT_HEAD_DIM
ATT_KV_HEADS = 4
ATT_GROUP = ATT_Q_HEADS // ATT_KV_HEADS
ATT_Q_DIM = ATT_Q_HEADS * ATT_HEAD_DIM
ATT_KV_DIM = ATT_KV_HEADS * ATT_HEAD_DIM
ROPE_AXIS_DIM = ATT_HEAD_DIM // 2
ROPE_THETA = 10000.0
IN_SIZES = (GLA_KEY_DIM, GLA_KEY_DIM, GLA_VALUE_DIM, GATE_RANK, GATE_RANK, GLA_VALUE_DIM,
            ATT_Q_DIM, ATT_KV_DIM, ATT_KV_DIM, ATT_Q_DIM, D_MODEL, D_MODEL)

V7X_LANES = 128
V7X_SUBLANES = 8
V7X_MXU_DIM = 256
V7X_VMEM_BYTES = 64 * 1024 * 1024
V7X_VMEM_BUDGET = V7X_VMEM_BYTES - 8 * 1024 * 1024

INPROJ_ROWS = 512
OUTPROJ_ROWS = 512
GLA_CHUNK = 64
GLA_LEVELS = (1, 2, 4, 8, 16, 32)
GLA_CHUNKS_PER_STEP = 16
GLA_FINAL_ROWS = 256
GLA_BOUNDED_DECAY = 40.0
GLA_MASK_DIAG = 0
GLA_MASK_TREE_FWD = 1
GLA_MASK_TREE_BWD = GLA_MASK_TREE_FWD + len(GLA_LEVELS)
GLA_MASK_TRI_FWD = GLA_MASK_TREE_BWD + len(GLA_LEVELS)
GLA_MASK_TRI_BWD = GLA_MASK_TRI_FWD + 1
ATT_Q_SCALE = ATT_HEAD_DIM ** -0.5 * float(np.log2(np.e))
ATT_Q_ROWS = 256
ATT_KEY_CHUNK = 512

_NT = (((1,), (1,)), ((), ()))
_TN = (((0,), (0,)), ((), ()))


def _vmem_limit(block_bytes, scratch_bytes=0):
    want = 2 * block_bytes + scratch_bytes + 16 * 1024 * 1024
    return int(min(want, V7X_VMEM_BUDGET))


def _nbytes(shape, dtype):
    return int(np.prod(shape)) * jnp.dtype(dtype).itemsize


def _log_sigmoid(z):
    return jnp.minimum(z, 0.0) - jnp.log(1.0 + jnp.exp(-jnp.abs(z)))


def _sigmoid(z):
    return 0.5 * jnp.tanh(0.5 * z) + 0.5


def _silu(z):
    half = 0.5 * z
    return half * (1.0 + jnp.tanh(half))


def _inproj_body(x_ref, ng_ref, wgla_ref, wgate_ref, wgf_ref, wgb_ref, bgf_ref, bgb_ref,
                 watt_ref, wm_ref, qng_ref, kng_ref, cos_ref, sin_ref, bd_ref,
                 qa_ref, ka_ref, va_ref, za_ref, gf_ref, gb_ref,
                 qbt_ref, kb_ref, vbt_ref, zb_ref, ma_ref, mb_ref):
    x = x_ref[...]
    ms = jnp.mean(x * x, axis=-1, keepdims=True)
    h = (x * lax.rsqrt(ms + EPS) * ng_ref[...]).astype(BF16)

    def proj(w_ref, lo, hi):
        return jnp.dot(h, w_ref[:, lo:hi], preferred_element_type=F32)

    pg = proj(wgate_ref, 0, V7X_LANES).astype(BF16)
    gate_rows = pg.shape[0] // 4

    def decay_gates(part):
        rows = slice(gate_rows * part, gate_rows * (part + 1))
        zf = jnp.dot(pg[rows], wgf_ref[...], preferred_element_type=F32) + bgf_ref[...]
        zb = jnp.dot(pg[rows], wgb_ref[...], preferred_element_type=F32) + bgb_ref[...]
        gf_ref[rows, :] = _log_sigmoid(zf) * (1.0 / GATE_NORMALIZER)
        gb_ref[rows, :] = _log_sigmoid(zb) * (1.0 / GATE_NORMALIZER)

    cos = cos_ref[...]
    sin = sin_ref[...]
    bd = bd_ref[...]
    lane = lax.broadcasted_iota(jnp.int32, cos.shape, 1)
    first_half = (lane & (ROPE_AXIS_DIM // 2)) == 0

    def norm_rope(p, gain):
        sq = (p * p).astype(BF16)
        msq = jnp.dot(sq, bd, preferred_element_type=F32)
        r = lax.rsqrt(msq + EPS)
        xg = p * gain
        outs = []
        for s in range(2):
            xs = xg[:, V7X_LANES * s:V7X_LANES * (s + 1)]
            partner = jnp.where(first_half,
                                pltpu.roll(xs, V7X_LANES - ROPE_AXIS_DIM // 2, 1),
                                pltpu.roll(xs, ROPE_AXIS_DIM // 2, 1))
            outs.append(xs * cos + partner * sin)
        return jnp.concatenate(outs, axis=1) * r

    def attn_q(c):
        lo = V7X_MXU_DIM * c
        p = proj(watt_ref, lo, lo + V7X_MXU_DIM)
        y = norm_rope(p, qng_ref[:, lo:lo + V7X_MXU_DIM])
        qbt_ref[lo:lo + V7X_MXU_DIM, :] = jnp.transpose(y).astype(BF16)

    def attn_kv():
        kk = norm_rope(proj(watt_ref, 1024, 1280), kng_ref[...]).astype(BF16)
        vt = jnp.transpose(proj(watt_ref, 1280, 1536)).astype(BF16)
        ones = jnp.ones((ATT_HEAD_DIM, vt.shape[1]), BF16)
        for g in range(ATT_KV_HEADS):
            kb_ref[g] = kk[:, ATT_HEAD_DIM * g:ATT_HEAD_DIM * (g + 1)]
            vbt_ref[2 * ATT_HEAD_DIM * g:2 * ATT_HEAD_DIM * g + ATT_HEAD_DIM, :] = (
                vt[ATT_HEAD_DIM * g:ATT_HEAD_DIM * (g + 1), :])
            vbt_ref[2 * ATT_HEAD_DIM * g + ATT_HEAD_DIM:2 * ATT_HEAD_DIM * (g + 1), :] = ones

    qa_ref[...] = (proj(wgla_ref, 0, 512) * (GLA_DK ** -0.5)).astype(BF16)
    attn_kv()
    ka_ref[...] = proj(wgla_ref, 512, 1024).astype(BF16)
    attn_q(0)
    va_ref[...] = proj(wgla_ref, 1024, 2048).astype(BF16)
    decay_gates(0)
    za_ref[...] = proj(wgla_ref, 2048, 3072).astype(BF16)
    attn_q(1)
    decay_gates(1)
    zb_ref[...] = proj(watt_ref, 1536, 2560).astype(BF16)
    attn_q(2)
    decay_gates(2)
    ma_ref[...] = proj(wm_ref, 0, 1024).astype(BF16)
    attn_q(3)
    decay_gates(3)
    mb_ref[...] = proj(wm_ref, 1024, 2048).astype(BF16)


def _inproj(x, lw, cos, sin, bd, seq_len):
    n = x.shape[0]
    tm = INPROJ_ROWS
    pos_blocks = seq_len // tm
    row = lambda i: (i, 0)
    const = lambda i: (0, 0)

    def wspec(shape):
        return pl.BlockSpec(shape, const, pipeline_mode=pl.Buffered(1))

    in_specs = [
        pl.BlockSpec((tm, D_MODEL), row),
        wspec((1, D_MODEL)),
        wspec((D_MODEL, 3072)),
        wspec((D_MODEL, V7X_LANES)),
        wspec((V7X_LANES, GLA_KEY_DIM)),
        wspec((V7X_LANES, GLA_KEY_DIM)),
        wspec((1, GLA_KEY_DIM)),
        wspec((1, GLA_KEY_DIM)),
        wspec((D_MODEL, 2560)),
        wspec((D_MODEL, 2048)),
        wspec((1, ATT_Q_DIM)),
        wspec((1, ATT_KV_DIM)),
        pl.BlockSpec((tm, V7X_LANES), lambda i: (i % pos_blocks, 0)),
        pl.BlockSpec((tm, V7X_LANES), lambda i: (i % pos_blocks, 0)),
        wspec((V7X_MXU_DIM, V7X_MXU_DIM)),
    ]
    out_shape = [
        jax.ShapeDtypeStruct((n, GLA_KEY_DIM), BF16),
        jax.ShapeDtypeStruct((n, GLA_KEY_DIM), BF16),
        jax.ShapeDtypeStruct((n, GLA_VALUE_DIM), BF16),
        jax.ShapeDtypeStruct((n, GLA_VALUE_DIM), BF16),
        jax.ShapeDtypeStruct((n, GLA_KEY_DIM), F32),
        jax.ShapeDtypeStruct((n, GLA_KEY_DIM), F32),
        jax.ShapeDtypeStruct((ATT_Q_DIM, n), BF16),
        jax.ShapeDtypeStruct((ATT_KV_HEADS, n, ATT_HEAD_DIM), BF16),
        jax.ShapeDtypeStruct((2 * ATT_KV_DIM, n), BF16),
        jax.ShapeDtypeStruct((n, ATT_Q_DIM), BF16),
        jax.ShapeDtypeStruct((n, D_MODEL), BF16),
        jax.ShapeDtypeStruct((n, D_MODEL), BF16),
    ]
    out_specs = [
        pl.BlockSpec((tm, GLA_KEY_DIM), row),
        pl.BlockSpec((tm, GLA_KEY_DIM), row),
        pl.BlockSpec((tm, GLA_VALUE_DIM), row),
        pl.BlockSpec((tm, GLA_VALUE_DIM), row),
        pl.BlockSpec((tm, GLA_KEY_DIM), row),
        pl.BlockSpec((tm, GLA_KEY_DIM), row),
        pl.BlockSpec((ATT_Q_DIM, tm), lambda i: (0, i)),
        pl.BlockSpec((ATT_KV_HEADS, tm, ATT_HEAD_DIM), lambda i: (0, i, 0)),
        pl.BlockSpec((2 * ATT_KV_DIM, tm), lambda i: (0, i)),
        pl.BlockSpec((tm, ATT_Q_DIM), row),
        pl.BlockSpec((tm, D_MODEL), row),
        pl.BlockSpec((tm, D_MODEL), row),
    ]
    weight_bytes = _nbytes((D_MODEL, 3072 + V7X_LANES + 2560 + 2048), BF16)
    block_bytes = (_nbytes((tm, D_MODEL), F32) + _nbytes((tm, 7 * 1024 + 512), BF16)
                   + 2 * _nbytes((tm, GLA_KEY_DIM), F32) + 2 * _nbytes((tm, V7X_LANES), F32))
    return pl.pallas_call(
        _inproj_body,
        grid=(n // tm,),
        in_specs=in_specs,
        out_specs=out_specs,
        out_shape=out_shape,
        compiler_params=pltpu.CompilerParams(
            dimension_semantics=("arbitrary",),
            vmem_limit_bytes=_vmem_limit(block_bytes, weight_bytes)),
        name="inproj",
    )(x, lw["norm_g"], lw["w_gla"], lw["w_gate"], lw["w_gf"], lw["w_gb"], lw["b_gf"], lw["b_gb"],
      lw["w_att"], lw["w_m"], lw["q_gain"], lw["k_gain"], cos, sin, bd)


def _gla_boundary(beta, m, reverse):
    c, width = beta.shape
    two_m = 2 * m
    off = m if reverse else m - 1
    pieces = []
    if two_m >= V7X_SUBLANES:
        for v in range(c // V7X_SUBLANES):
            r = (V7X_SUBLANES * v // two_m) * two_m + off
            pieces.append(jnp.broadcast_to(beta[r:r + 1, :], (V7X_SUBLANES, width)))
    else:
        sub = lax.broadcasted_iota(jnp.int32, (V7X_SUBLANES, width), 0)
        per = V7X_SUBLANES // two_m
        for v in range(c // V7X_SUBLANES):
            rows = [jnp.broadcast_to(beta[V7X_SUBLANES * v + two_m * j + off:V7X_SUBLANES * v + two_m * j + off + 1, :],
                                     (V7X_SUBLANES, width)) for j in range(per)]
            acc = rows[-1]
            for j in range(per - 2, -1, -1):
                acc = jnp.where(sub < two_m * (j + 1), rows[j], acc)
            pieces.append(acc)
    return jnp.concatenate(pieces, axis=0)


def _gla_tree_scores(q, k, g, beta, mask_ref, sgn_ref, reverse):
    row = lax.broadcasted_iota(jnp.int32, g.shape, 0)
    nlev = len(GLA_LEVELS)
    mask_base = GLA_MASK_TREE_BWD if reverse else GLA_MASK_TREE_FWD
    sgn_base = (nlev - 1) if reverse else 0
    a = None
    if not reverse:
        a = mask_ref[GLA_MASK_DIAG] * lax.dot_general(q, k, _NT, preferred_element_type=F32)
    for li, m in enumerate(GLA_LEVELS):
        if m == 1:
            parity = (row & 1) == (0 if reverse else 1)
            f = jnp.exp(jnp.where(parity, g, 0.0))
        else:
            f = jnp.exp2((beta - _gla_boundary(beta, m, reverse)) * sgn_ref[sgn_base + li - 1])
        fb = f.astype(BF16)
        am = mask_ref[mask_base + li] * lax.dot_general(q * fb, k * fb, _NT, preferred_element_type=F32)
        a = am if a is None else a + am
    return a


def _gla_chunk_local(q, k, v, g, mask_ref, sgn_ref, reverse, bounded):
    c = GLA_CHUNK
    row = lax.broadcasted_iota(jnp.int32, g.shape, 0)
    beta = g
    for sh in GLA_LEVELS:
        if reverse:
            beta = beta + jnp.where(row < c - sh, pltpu.roll(beta, c - sh, 0), 0.0)
        else:
            beta = beta + jnp.where(row >= sh, pltpu.roll(beta, sh, 0), 0.0)
    qd = q * jnp.exp(beta).astype(BF16)
    tot = beta[0:1, :] if reverse else beta[c - 1:c, :]
    kd = k * jnp.exp(tot - beta).astype(BF16)
    decay_col = jnp.transpose(jnp.broadcast_to(jnp.exp(tot), (V7X_SUBLANES, GLA_DK)))[:, 0:1]
    if bounded:
        kg = k * jnp.exp(-beta).astype(BF16)
        a = (mask_ref[GLA_MASK_TRI_BWD if reverse else GLA_MASK_TRI_FWD]
             * lax.dot_general(qd, kg, _NT, preferred_element_type=F32))
    else:
        a = _gla_tree_scores(q, k, g, beta, mask_ref, sgn_ref, reverse)
    kv = lax.dot_general(kd, v, _TN, preferred_element_type=F32)
    return qd, a.astype(BF16), kv, decay_col


def _gla_chunk_state(local, v, s_ref):
    qd, a, kv, decay_col = local
    s = s_ref[...]
    o = jnp.dot(qd, s.astype(BF16), preferred_element_type=F32)
    s_ref[...] = decay_col * s + kv
    return o + jnp.dot(a, v, preferred_element_type=F32)


def _gla_body(q_ref, k_ref, v_ref, gf_ref, gb_ref, za_ref, gn_ref, mask_ref, sgn_ref, o_ref,
              acc_ref, sf_ref, sb_ref):
    t = q_ref.shape[0]
    c = GLA_CHUNK
    n = t // c
    acc_ref[...] = jnp.zeros_like(acc_ref)
    sf_ref[...] = jnp.zeros_like(sf_ref)
    sb_ref[...] = jnp.zeros_like(sb_ref)

    chunk_tot = jnp.minimum(jnp.sum(gf_ref[...].reshape(n, c, GLA_DK), axis=1),
                            jnp.sum(gb_ref[...].reshape(n, c, GLA_DK), axis=1))
    bounded = jnp.min(chunk_tot) >= -GLA_BOUNDED_DECAY

    def scan(is_bounded):
        def step(i, carry):
            work = []
            for u in range(GLA_CHUNKS_PER_STEP):
                j = i * GLA_CHUNKS_PER_STEP + u
                for reverse, g_ref, s_ref in ((False, gf_ref, sf_ref), (True, gb_ref, sb_ref)):
                    sl = pl.ds(pl.multiple_of(((n - 1 - j) if reverse else j) * c, c), c)
                    local = _gla_chunk_local(q_ref[sl, :], k_ref[sl, :], v_ref[sl, :], g_ref[sl, :],
                                             mask_ref, sgn_ref, reverse, is_bounded)
                    work.append((sl, local, s_ref))
            for sl, local, s_ref in work:
                acc_ref[sl, :] += _gla_chunk_state(local, v_ref[sl, :], s_ref)
            return carry

        lax.fori_loop(0, n // GLA_CHUNKS_PER_STEP, step, 0)

    @pl.when(bounded)
    def _():
        scan(True)

    @pl.when(jnp.logical_not(bounded))
    def _():
        scan(False)

    rows = GLA_FINAL_ROWS

    def fin(i, carry):
        sl = pl.ds(pl.multiple_of(i * rows, rows), rows)
        o = acc_ref[sl, :]
        ms = jnp.mean(o * o, axis=-1, keepdims=True)
        y = o * lax.rsqrt(ms + EPS) * gn_ref[...]
        o_ref[sl, :] = (y * _silu(za_ref[sl, :].astype(F32))).astype(BF16)
        return carry

    lax.fori_loop(0, t // rows, fin, 0, unroll=4)


def _gla_masks():
    c = GLA_CHUNK
    ri = np.arange(c)[:, None]
    ci = np.arange(c)[None, :]
    masks = [ri == ci]
    for m in GLA_LEVELS:
        same = (ri // (2 * m)) == (ci // (2 * m))
        masks.append(same & ((ri % (2 * m)) >= m) & ((ci % (2 * m)) < m))
    for m in GLA_LEVELS:
        same = (ri // (2 * m)) == (ci // (2 * m))
        masks.append(same & ((ri % (2 * m)) < m) & ((ci % (2 * m)) >= m))
    masks.append(ri >= ci)
    masks.append(ri < ci)
    sgn = []
    row = np.arange(c)[:, None] * np.ones((1, GLA_DK), np.int64)
    for reverse in (False, True):
        for m in GLA_LEVELS[1:]:
            upper = (row % (2 * m)) >= m
            sgn.append(np.where(upper != reverse, np.log2(np.e), -np.log2(np.e)))
    return jnp.asarray(np.stack(masks).astype(np.float32)), jnp.asarray(np.stack(sgn).astype(np.float32))


def _gla(qa, ka, va, gf, gb, za, gn, masks, sgn, batch, seq_len):
    n = qa.shape[0]
    t = seq_len
    assert t % (GLA_CHUNK * GLA_CHUNKS_PER_STEP) == 0 and t % GLA_FINAL_ROWS == 0, t
    kspec = pl.BlockSpec((t, GLA_DK), lambda b, h: (b, h))
    vspec = pl.BlockSpec((t, GLA_DV), lambda b, h: (b, h))
    nm = masks.shape[0]
    block_bytes = (2 * _nbytes((t, GLA_DK), BF16) + 3 * _nbytes((t, GLA_DV), BF16)
                   + 2 * _nbytes((t, GLA_DK), F32) + _nbytes(masks.shape, F32) + _nbytes(sgn.shape, F32))
    scratch_bytes = _nbytes((t, GLA_DV), F32) + 2 * _nbytes((GLA_DK, GLA_DV), F32)
    return pl.pallas_call(
        _gla_body,
        grid=(batch, GLA_HEADS),
        in_specs=[kspec, kspec, vspec, kspec, kspec, vspec,
                  pl.BlockSpec((1, GLA_DV), lambda b, h: (0, 0)),
                  pl.BlockSpec((nm, GLA_CHUNK, GLA_CHUNK), lambda b, h: (0, 0, 0)),
                  pl.BlockSpec(sgn.shape, lambda b, h: (0, 0, 0))],
        out_specs=vspec,
        out_shape=jax.ShapeDtypeStruct((n, GLA_VALUE_DIM), BF16),
        scratch_shapes=[pltpu.VMEM((t, GLA_DV), F32),
                        pltpu.VMEM((GLA_DK, GLA_DV), F32),
                        pltpu.VMEM((GLA_DK, GLA_DV), F32)],
        compiler_params=pltpu.CompilerParams(
            dimension_semantics=("arbitrary", "arbitrary"),
            vmem_limit_bytes=_vmem_limit(block_bytes, scratch_bytes)),
        name="gla",
    )(qa, ka, va, gf, gb, za, gn, masks, sgn)


def _attn_body(qt_ref, k_ref, vt_ref, z_ref, o_ref, s_ref, acc_ref):
    t = k_ref.shape[1]
    tq = ATT_Q_ROWS
    nq = t // tq
    kc = ATT_KEY_CHUNK
    nk = t // kc
    cols = ATT_GROUP * tq

    def load_q(i):
        qt = qt_ref[:, pl.ds(pl.multiple_of(i * tq, tq), tq)]
        return jnp.concatenate([qt[ATT_HEAD_DIM * j:ATT_HEAD_DIM * (j + 1), :] for j in range(ATT_GROUP)], axis=1)

    def scores(q4t, c, m, buf):
        sl = pl.ds(pl.multiple_of(c * kc, kc), kc)
        s = jnp.dot(k_ref[0, sl, :], q4t, preferred_element_type=F32)
        s_ref[buf] = s
        return jnp.maximum(m, jnp.max(s, axis=0, keepdims=True))

    def weighted(c, m_old, m_new, buf):
        sl = pl.ds(pl.multiple_of(c * kc, kc), kc)
        p = jnp.exp2(s_ref[buf] - m_new).astype(BF16)
        acc_ref[...] = (jnp.exp2(m_old - m_new) * acc_ref[...]
                        + jnp.dot(vt_ref[:, sl], p, preferred_element_type=F32))

    m_init = jnp.full((1, cols), -jnp.inf, F32)

    def tile(i, m_first):
        q4t = load_q(i)
        acc_ref[...] = jnp.zeros_like(acc_ref)

        def pair(j, carry):
            m_a, m_b = carry
            c = 2 * j
            m_c = scores(q4t, c + 1, m_b, 1)
            weighted(c, m_a, m_b, 0)
            m_d = scores(q4t, c + 2, m_c, 0)
            weighted(c + 1, m_b, m_c, 1)
            return m_c, m_d

        m_a, m_b = lax.fori_loop(0, nk // 2 - 1, pair, (m_init, m_first), unroll=True)
        m_c = scores(q4t, nk - 1, m_b, 1)
        weighted(nk - 2, m_a, m_b, 0)
        m_next = scores(load_q(jnp.minimum(i + 1, nq - 1)), 0, m_init, 0)
        weighted(nk - 1, m_b, m_c, 1)
        acc = acc_ref[...]
        ot = acc[0:ATT_HEAD_DIM, :] / acc[ATT_HEAD_DIM:ATT_HEAD_DIM + 1, :]
        o = jnp.concatenate([jnp.transpose(ot[:, tq * j:tq * (j + 1)]) for j in range(ATT_GROUP)], axis=1)
        rows = pl.ds(pl.multiple_of(i * tq, tq), tq)
        o_ref[rows, :] = (o * _silu(z_ref[rows, :].astype(F32))).astype(BF16)
        return m_next

    lax.fori_loop(0, nq, tile, scores(load_q(0), 0, m_init, 0), unroll=2)


def _attn(qbt, kb, vbt, zb, batch, seq_len):
    n = zb.shape[0]
    t = seq_len
    tq = ATT_Q_ROWS
    width = ATT_GROUP * ATT_HEAD_DIM
    ospec = pl.BlockSpec((t, width), lambda b, g: (b, g))
    block_bytes = (3 * _nbytes((t, width), BF16) + _nbytes((t, V7X_LANES), BF16)
                   + _nbytes((2 * ATT_HEAD_DIM, t), BF16))
    scratch_bytes = (_nbytes((2, ATT_KEY_CHUNK, ATT_GROUP * tq), F32)
                     + _nbytes((2 * ATT_HEAD_DIM, ATT_GROUP * tq), F32))
    return pl.pallas_call(
        _attn_body,
        grid=(batch, ATT_KV_HEADS),
        in_specs=[pl.BlockSpec((width, t), lambda b, g: (g, b)),
                  pl.BlockSpec((1, t, ATT_HEAD_DIM), lambda b, g: (g, b, 0)),
                  pl.BlockSpec((2 * ATT_HEAD_DIM, t), lambda b, g: (g, b)),
                  ospec],
        out_specs=ospec,
        out_shape=jax.ShapeDtypeStruct((n, ATT_Q_DIM), BF16),
        scratch_shapes=[pltpu.VMEM((2, ATT_KEY_CHUNK, ATT_GROUP * tq), F32),
                        pltpu.VMEM((2 * ATT_HEAD_DIM, ATT_GROUP * tq), F32)],
        compiler_params=pltpu.CompilerParams(
            dimension_semantics=("arbitrary", "arbitrary"),
            vmem_limit_bytes=_vmem_limit(block_bytes, scratch_bytes)),
        name="attn",
    )(qbt, kb, vbt, zb)


def _outproj_body(oa_ref, ob_ref, ma_ref, mb_ref, x_ref, wa_ref, wb_ref, wo_ref, fg_ref, o_ref, *, final):
    ya = jnp.dot(oa_ref[...], wa_ref[...], preferred_element_type=F32)
    yb = jnp.dot(ob_ref[...], wb_ref[...], preferred_element_type=F32)
    merged = (_sigmoid(ma_ref[...].astype(F32)) * ya
              + _sigmoid(mb_ref[...].astype(F32)) * yb)
    y = x_ref[...] + jnp.dot(merged.astype(BF16), wo_ref[...], preferred_element_type=F32)
    if final:
        ms = jnp.mean(y * y, axis=-1, keepdims=True)
        y = y * lax.rsqrt(ms + EPS) * fg_ref[...]
    o_ref[...] = y


def _outproj(oa, ob, ma, mb, x, lw, final_gain, final):
    n = x.shape[0]
    tm = OUTPROJ_ROWS
    row = lambda i: (i, 0)
    const = lambda i: (0, 0)
    act = pl.BlockSpec((tm, D_MODEL), row)
    wspec = pl.BlockSpec((D_MODEL, D_MODEL), const, pipeline_mode=pl.Buffered(1))
    block_bytes = 4 * _nbytes((tm, D_MODEL), BF16) + 2 * _nbytes((tm, D_MODEL), F32)
    weight_bytes = 3 * _nbytes((D_MODEL, D_MODEL), BF16)
    return pl.pallas_call(
        functools.partial(_outproj_body, final=final),
        grid=(n // tm,),
        in_specs=[act, act, act, act, act, wspec, wspec, wspec,
                  pl.BlockSpec((1, D_MODEL), const, pipeline_mode=pl.Buffered(1))],
        out_specs=act,
        out_shape=jax.ShapeDtypeStruct((n, D_MODEL), F32),
        compiler_params=pltpu.CompilerParams(
            dimension_semantics=("arbitrary",),
            vmem_limit_bytes=_vmem_limit(block_bytes, weight_bytes)),
        name="outproj",
    )(oa, ob, ma, mb, x, lw["w_a"], lw["w_b"], lw["w_o"], final_gain)


def _rope_tables(seq_len):
    rows = seq_len // GRID_W
    r = jnp.repeat(jnp.arange(rows, dtype=F32), GRID_W)
    c = jnp.tile(jnp.arange(GRID_W, dtype=F32), rows)
    nf = ROPE_AXIS_DIM // 2
    inv = ROPE_THETA ** (-jnp.arange(nf, dtype=F32) / nf)
    ang_r = r[:, None] * inv
    ang_c = c[:, None] * inv
    cos = jnp.concatenate([jnp.cos(ang_r), jnp.cos(ang_r), jnp.cos(ang_c), jnp.cos(ang_c)], axis=-1)
    sin = jnp.concatenate([-jnp.sin(ang_r), jnp.sin(ang_r), -jnp.sin(ang_c), jnp.sin(ang_c)], axis=-1)
    reps = V7X_LANES // ATT_HEAD_DIM
    return jnp.tile(cos, (1, reps)), jnp.tile(sin, (1, reps))


def _layer_weights(l, norm_g, w_in, w_gate_f, b_gate_f, w_gate_b, b_gate_b, gla_norm_g, q_norm_g, k_norm_g,
                   w_branch_a, w_branch_b, w_out):
    pts = np.cumsum((0,) + IN_SIZES)
    col = lambda i: w_in[l][:, pts[i]:pts[i + 1]]
    w = w_in[l]
    w_gate = jnp.zeros((D_MODEL, V7X_LANES), F32).at[:, :2 * GATE_RANK].set(w[:, pts[3]:pts[5]])
    w_gf = jnp.zeros((V7X_LANES, GLA_KEY_DIM), F32).at[:GATE_RANK].set(w_gate_f[l])
    w_gb = jnp.zeros((V7X_LANES, GLA_KEY_DIM), F32).at[GATE_RANK:2 * GATE_RANK].set(w_gate_b[l])
    return {
        "norm_g": norm_g[l].reshape(1, D_MODEL),
        "w_gla": jnp.concatenate([col(0), col(1), col(2), col(5)], axis=1).astype(BF16),
        "w_gate": w_gate.astype(BF16),
        "w_gf": w_gf.astype(BF16),
        "w_gb": w_gb.astype(BF16),
        "b_gf": b_gate_f[l].reshape(1, GLA_KEY_DIM),
        "b_gb": b_gate_b[l].reshape(1, GLA_KEY_DIM),
        "w_att": jnp.concatenate([col(6), col(7), col(8), col(9)], axis=1).astype(BF16),
        "w_m": jnp.concatenate([col(10), col(11)], axis=1).astype(BF16),
        "q_gain": (jnp.tile(q_norm_g[l], ATT_Q_HEADS) * ATT_Q_SCALE).reshape(1, ATT_Q_DIM),
        "k_gain": jnp.tile(k_norm_g[l], ATT_KV_HEADS).reshape(1, ATT_KV_DIM),
        "gla_gain": gla_norm_g[l].reshape(1, GLA_DV),
        "w_a": w_branch_a[l].astype(BF16),
        "w_b": w_branch_b[l].astype(BF16),
        "w_o": w_out[l].astype(BF16),
    }


def _trunk(x3, layers, final_gain, masks, sgn, bd):
    batch, seq_len, _ = x3.shape
    x = x3.reshape(batch * seq_len, D_MODEL)
    cos, sin = _rope_tables(seq_len)
    for l, lw in enumerate(layers):
        qa, ka, va, za, gf, gb, qbt, kb, vbt, zb, ma, mb = _inproj(x, lw, cos, sin, bd, seq_len)
        oa = _gla(qa, ka, va, gf, gb, za, lw["gla_gain"], masks, sgn, batch, seq_len)
        ob = _attn(qbt, kb, vbt, zb, batch, seq_len)
        x = _outproj(oa, ob, ma, mb, x, lw, final_gain, final=(l == len(layers) - 1))
    return x.reshape(batch, seq_len, D_MODEL)


def kernel(x_prompt, x_sample, norm_g, w_in, w_gate_f, b_gate_f, w_gate_b, b_gate_b, gla_norm_g, q_norm_g,
           k_norm_g, w_branch_a, w_branch_b, w_out, final_norm_g):
    layers = [_layer_weights(l, norm_g, w_in, w_gate_f, b_gate_f, w_gate_b, b_gate_b, gla_norm_g, q_norm_g,
                             k_norm_g, w_branch_a, w_branch_b, w_out) for l in range(DEPTH)]
    final_gain = final_norm_g.reshape(1, D_MODEL)
    masks, sgn = _gla_masks()
    head = np.arange(V7X_MXU_DIM) // ATT_HEAD_DIM
    bd = jnp.asarray((head[:, None] == head[None, :]).astype(np.float32) / ATT_HEAD_DIM, dtype=BF16)
    y_prompt = _trunk(x_prompt, layers, final_gain, masks, sgn, bd)
    y_sample = _trunk(x_sample, layers, final_gain, masks, sgn, bd)
    return (y_prompt, y_sample)
```

```python
import functools

import numpy as np
import jax
import jax.numpy as jnp
from jax import lax
from jax.experimental import pallas as pl
from jax.experimental.pallas import tpu as pltpu

F32, BF16 = jnp.float32, jnp.bfloat16

D_MODEL = 1024
DEPTH = 4
EPS = 1e-6
GRID_W = 64
GLA_HEADS = 4
GLA_KEY_DIM = D_MODEL // 2
GLA_VALUE_DIM = D_MODEL
GLA_DK = GLA_KEY_DIM // GLA_HEADS
GLA_DV = GLA_VALUE_DIM // GLA_HEADS
GATE_RANK = 16
GATE_NORMALIZER = 16.0
ATT_HEAD_DIM = 64
ATT_Q_HEADS = D_MODEL // ATT_HEAD_DIM
ATT_KV_HEADS = 4
ATT_GROUP = ATT_Q_HEADS // ATT_KV_HEADS
ATT_Q_DIM = ATT_Q_HEADS * ATT_HEAD_DIM
ATT_KV_DIM = ATT_KV_HEADS * ATT_HEAD_DIM
ROPE_AXIS_DIM = ATT_HEAD_DIM // 2
ROPE_THETA = 10000.0
IN_SIZES = (GLA_KEY_DIM, GLA_KEY_DIM, GLA_VALUE_DIM, GATE_RANK, GATE_RANK, GLA_VALUE_DIM,
            ATT_Q_DIM, ATT_KV_DIM, ATT_KV_DIM, ATT_Q_DIM, D_MODEL, D_MODEL)

V7X_LANES = 128
V7X_SUBLANES = 8
V7X_MXU_DIM = 256
V7X_VMEM_BYTES = 64 * 1024 * 1024
V7X_VMEM_BUDGET = V7X_VMEM_BYTES - 8 * 1024 * 1024

INPROJ_ROWS = 512
OUTPROJ_ROWS = 512
GLA_CHUNK = 64
GLA_LEVELS = (1, 2, 4, 8, 16, 32)
GLA_CHUNKS_PER_STEP = 16
GLA_FINAL_ROWS = 256
GLA_BOUNDED_DECAY = 40.0
GLA_MASK_DIAG = 0
GLA_MASK_TREE_FWD = 1
GLA_MASK_TREE_BWD = GLA_MASK_TREE_FWD + len(GLA_LEVELS)
GLA_MASK_TRI_FWD = GLA_MASK_TREE_BWD + len(GLA_LEVELS)
GLA_MASK_TRI_BWD = GLA_MASK_TRI_FWD + 1
ATT_Q_SCALE = ATT_HEAD_DIM ** -0.5 * float(np.log2(np.e))
ATT_Q_ROWS = 256
ATT_KEY_CHUNK = 512

_NT = (((1,), (1,)), ((), ()))
_TN = (((0,), (0,)), ((), ()))


def _vmem_limit(block_bytes, scratch_bytes=0):
    want = 2 * block_bytes + scratch_bytes + 16 * 1024 * 1024
    return int(min(want, V7X_VMEM_BUDGET))


def _nbytes(shape, dtype):
    return int(np.prod(shape)) * jnp.dtype(dtype).itemsize


def _log_sigmoid(z):
    return jnp.minimum(z, 0.0) - jnp.log(1.0 + jnp.exp(-jnp.abs(z)))


def _sigmoid(z):
    return 0.5 * jnp.tanh(0.5 * z) + 0.5


def _silu(z):
    half = 0.5 * z
    return half * (1.0 + jnp.tanh(half))


def _inproj_body(x_ref, ng_ref, wgla_ref, wgate_ref, wgf_ref, wgb_ref, bgf_ref, bgb_ref,
                 watt_ref, wm_ref, qng_ref, kng_ref, cos_ref, sin_ref, bd_ref,
                 qa_ref, ka_ref, va_ref, za_ref, gf_ref, gb_ref,
                 qbt_ref, kb_ref, vbt_ref, zb_ref, ma_ref, mb_ref):
    x = x_ref[...]
    ms = jnp.mean(x * x, axis=-1, keepdims=True)
    h = (x * lax.rsqrt(ms + EPS) * ng_ref[...]).astype(BF16)

    def proj(w_ref, lo, hi):
        return jnp.dot(h, w_ref[:, lo:hi], preferred_element_type=F32)

    pg = proj(wgate_ref, 0, V7X_LANES).astype(BF16)
    gate_rows = pg.shape[0] // 4

    def decay_gates(part):
        rows = slice(gate_rows * part, gate_rows * (part + 1))
        zf = jnp.dot(pg[rows], wgf_ref[...], preferred_element_type=F32) + bgf_ref[...]
        zb = jnp.dot(pg[rows], wgb_ref[...], preferred_element_type=F32) + bgb_ref[...]
        gf_ref[rows, :] = _log_sigmoid(zf) * (1.0 / GATE_NORMALIZER)
        gb_ref[rows, :] = _log_sigmoid(zb) * (1.0 / GATE_NORMALIZER)

    cos = cos_ref[...]
    sin = sin_ref[...]
    bd = bd_ref[...]
    lane = lax.broadcasted_iota(jnp.int32, cos.shape, 1)
    first_half = (lane & (ROPE_AXIS_DIM // 2)) == 0

    def norm_rope(p, gain):
        sq = (p * p).astype(BF16)
        msq = jnp.dot(sq, bd, preferred_element_type=F32)
        r = lax.rsqrt(msq + EPS)
        xg = p * gain
        outs = []
        for s in range(2):
            xs = xg[:, V7X_LANES * s:V7X_LANES * (s + 1)]
            partner = jnp.where(first_half,
                                pltpu.roll(xs, V7X_LANES - ROPE_AXIS_DIM // 2, 1),
                                pltpu.roll(xs, ROPE_AXIS_DIM // 2, 1))
            outs.append(xs * cos + partner * sin)
        return jnp.concatenate(outs, axis=1) * r

    def attn_q(c):
        lo = V7X_MXU_DIM * c
        p = proj(watt_ref, lo, lo + V7X_MXU_DIM)
        y = norm_rope(p, qng_ref[:, lo:lo + V7X_MXU_DIM])
        qbt_ref[lo:lo + V7X_MXU_DIM, :] = jnp.transpose(y).astype(BF16)

    def attn_kv():
        kk = norm_rope(proj(watt_ref, 1024, 1280), kng_ref[...]).astype(BF16)
        vt = jnp.transpose(proj(watt_ref, 1280, 1536)).astype(BF16)
        ones = jnp.ones((ATT_HEAD_DIM, vt.shape[1]), BF16)
        for g in range(ATT_KV_HEADS):
            kb_ref[g] = kk[:, ATT_HEAD_DIM * g:ATT_HEAD_DIM * (g + 1)]
            vbt_ref[2 * ATT_HEAD_DIM * g:2 * ATT_HEAD_DIM * g + ATT_HEAD_DIM, :] = (
                vt[ATT_HEAD_DIM * g:ATT_HEAD_DIM * (g + 1), :])
            vbt_ref[2 * ATT_HEAD_DIM * g + ATT_HEAD_DIM:2 * ATT_HEAD_DIM * (g + 1), :] = ones

    qa_ref[...] = (proj(wgla_ref, 0, 512) * (GLA_DK ** -0.5)).astype(BF16)
    attn_kv()
    ka_ref[...] = proj(wgla_ref, 512, 1024).astype(BF16)
    attn_q(0)
    va_ref[...] = proj(wgla_ref, 1024, 2048).astype(BF16)
    decay_gates(0)
    za_ref[...] = proj(wgla_ref, 2048, 3072).astype(BF16)
    attn_q(1)
    decay_gates(1)
    zb_ref[...] = proj(watt_ref, 1536, 2560).astype(BF16)
    attn_q(2)
    decay_gates(2)
    ma_ref[...] = proj(wm_ref, 0, 1024).astype(BF16)
    attn_q(3)
    decay_gates(3)
    mb_ref[...] = proj(wm_ref, 1024, 2048).astype(BF16)


def _layer_spec(shape, layer):
    index = (layer,) + (0,) * len(shape)
    return pl.BlockSpec((None,) + tuple(shape), lambda *_: index, pipeline_mode=pl.Buffered(1))


def _inproj(x, lw, layer, cos, sin, bd, seq_len):
    n = x.shape[0]
    tm = INPROJ_ROWS
    pos_blocks = seq_len // tm
    row = lambda i: (i, 0)
    wspec = functools.partial(_layer_spec, layer=layer)

    in_specs = [
        pl.BlockSpec((tm, D_MODEL), row),
        wspec((1, D_MODEL)),
        wspec((D_MODEL, 3072)),
        wspec((D_MODEL, V7X_LANES)),
        wspec((V7X_LANES, GLA_KEY_DIM)),
        wspec((V7X_LANES, GLA_KEY_DIM)),
        wspec((1, GLA_KEY_DIM)),
        wspec((1, GLA_KEY_DIM)),
        wspec((D_MODEL, 2560)),
        wspec((D_MODEL, 2048)),
        wspec((1, ATT_Q_DIM)),
        wspec((1, ATT_KV_DIM)),
        pl.BlockSpec((tm, V7X_LANES), lambda i: (i % pos_blocks, 0)),
        pl.BlockSpec((tm, V7X_LANES), lambda i: (i % pos_blocks, 0)),
        pl.BlockSpec((V7X_MXU_DIM, V7X_MXU_DIM), lambda i: (0, 0), pipeline_mode=pl.Buffered(1)),
    ]
    out_shape = [
        jax.ShapeDtypeStruct((n, GLA_KEY_DIM), BF16),
        jax.ShapeDtypeStruct((n, GLA_KEY_DIM), BF16),
        jax.ShapeDtypeStruct((n, GLA_VALUE_DIM), BF16),
        jax.ShapeDtypeStruct((n, GLA_VALUE_DIM), BF16),
        jax.ShapeDtypeStruct((n, GLA_KEY_DIM), F32),
        jax.ShapeDtypeStruct((n, GLA_KEY_DIM), F32),
        jax.ShapeDtypeStruct((ATT_Q_DIM, n), BF16),
        jax.ShapeDtypeStruct((ATT_KV_HEADS, n, ATT_HEAD_DIM), BF16),
        jax.ShapeDtypeStruct((2 * ATT_KV_DIM, n), BF16),
        jax.ShapeDtypeStruct((n, ATT_Q_DIM), BF16),
        jax.ShapeDtypeStruct((n, D_MODEL), BF16),
        jax.ShapeDtypeStruct((n, D_MODEL), BF16),
    ]
    out_specs = [
        pl.BlockSpec((tm, GLA_KEY_DIM), row),
        pl.BlockSpec((tm, GLA_KEY_DIM), row),
        pl.BlockSpec((tm, GLA_VALUE_DIM), row),
        pl.BlockSpec((tm, GLA_VALUE_DIM), row),
        pl.BlockSpec((tm, GLA_KEY_DIM), row),
        pl.BlockSpec((tm, GLA_KEY_DIM), row),
        pl.BlockSpec((ATT_Q_DIM, tm), lambda i: (0, i)),
        pl.BlockSpec((ATT_KV_HEADS, tm, ATT_HEAD_DIM), lambda i: (0, i, 0)),
        pl.BlockSpec((2 * ATT_KV_DIM, tm), lambda i: (0, i)),
        pl.BlockSpec((tm, ATT_Q_DIM), row),
        pl.BlockSpec((tm, D_MODEL), row),
        pl.BlockSpec((tm, D_MODEL), row),
    ]
    weight_bytes = _nbytes((D_MODEL, 3072 + V7X_LANES + 2560 + 2048), BF16)
    block_bytes = (_nbytes((tm, D_MODEL), F32) + _nbytes((tm, 7 * 1024 + 512), BF16)
                   + 2 * _nbytes((tm, GLA_KEY_DIM), F32) + 2 * _nbytes((tm, V7X_LANES), F32))
    return pl.pallas_call(
        _inproj_body,
        grid=(n // tm,),
        in_specs=in_specs,
        out_specs=out_specs,
        out_shape=out_shape,
        compiler_params=pltpu.CompilerParams(
            dimension_semantics=("arbitrary",),
            vmem_limit_bytes=_vmem_limit(block_bytes, weight_bytes)),
        name="inproj",
    )(x, lw["norm_g"], lw["w_gla"], lw["w_gate"], lw["w_gf"], lw["w_gb"], lw["b_gf"], lw["b_gb"],
      lw["w_att"], lw["w_m"], lw["q_gain"], lw["k_gain"], cos, sin, bd)


def _gla_boundary(beta, m, reverse):
    c, width = beta.shape
    two_m = 2 * m
    off = m if reverse else m - 1
    pieces = []
    if two_m >= V7X_SUBLANES:
        for v in range(c // V7X_SUBLANES):
            r = (V7X_SUBLANES * v // two_m) * two_m + off
            pieces.append(jnp.broadcast_to(beta[r:r + 1, :], (V7X_SUBLANES, width)))
    else:
        sub = lax.broadcasted_iota(jnp.int32, (V7X_SUBLANES, width), 0)
        per = V7X_SUBLANES // two_m
        for v in range(c // V7X_SUBLANES):
            rows = [jnp.broadcast_to(beta[V7X_SUBLANES * v + two_m * j + off:V7X_SUBLANES * v + two_m * j + off + 1, :],
                                     (V7X_SUBLANES, width)) for j in range(per)]
            acc = rows[-1]
            for j in range(per - 2, -1, -1):
                acc = jnp.where(sub < two_m * (j + 1), rows[j], acc)
            pieces.append(acc)
    return jnp.concatenate(pieces, axis=0)


def _gla_tree_scores(q, k, g, beta, mask_ref, sgn_ref, reverse):
    row = lax.broadcasted_iota(jnp.int32, g.shape, 0)
    nlev = len(GLA_LEVELS)
    mask_base = GLA_MASK_TREE_BWD if reverse else GLA_MASK_TREE_FWD
    sgn_base = (nlev - 1) if reverse else 0
    a = None
    if not reverse:
        a = mask_ref[GLA_MASK_DIAG] * lax.dot_general(q, k, _NT, preferred_element_type=F32)
    for li, m in enumerate(GLA_LEVELS):
        if m == 1:
            parity = (row & 1) == (0 if reverse else 1)
            f = jnp.exp(jnp.where(parity, g, 0.0))
        else:
            f = jnp.exp2((beta - _gla_boundary(beta, m, reverse)) * sgn_ref[sgn_base + li - 1])
        fb = f.astype(BF16)
        am = mask_ref[mask_base + li] * lax.dot_general(q * fb, k * fb, _NT, preferred_element_type=F32)
        a = am if a is None else a + am
    return a


def _gla_chunk_local(q, k, v, g, mask_ref, sgn_ref, reverse, bounded):
    c = GLA_CHUNK
    row = lax.broadcasted_iota(jnp.int32, g.shape, 0)
    beta = g
    for sh in GLA_LEVELS:
        if reverse:
            beta = beta + jnp.where(row < c - sh, pltpu.roll(beta, c - sh, 0), 0.0)
        else:
            beta = beta + jnp.where(row >= sh, pltpu.roll(beta, sh, 0), 0.0)
    qd = q * jnp.exp(beta).astype(BF16)
    tot = beta[0:1, :] if reverse else beta[c - 1:c, :]
    kd = k * jnp.exp(tot - beta).astype(BF16)
    decay_col = jnp.transpose(jnp.broadcast_to(jnp.exp(tot), (V7X_SUBLANES, GLA_DK)))[:, 0:1]
    if bounded:
        kg = k * jnp.exp(-beta).astype(BF16)
        a = (mask_ref[GLA_MASK_TRI_BWD if reverse else GLA_MASK_TRI_FWD]
             * lax.dot_general(qd, kg, _NT, preferred_element_type=F32))
    else:
        a = _gla_tree_scores(q, k, g, beta, mask_ref, sgn_ref, reverse)
    kv = lax.dot_general(kd, v, _TN, preferred_element_type=F32)
    return qd, a.astype(BF16), kv, decay_col


def _gla_chunk_state(local, v, s_ref):
    qd, a, kv, decay_col = local
    s = s_ref[...]
    o = jnp.dot(qd, s.astype(BF16), preferred_element_type=F32)
    s_ref[...] = decay_col * s + kv
    return o + jnp.dot(a, v, preferred_element_type=F32)


def _gla_body(q_ref, k_ref, v_ref, gf_ref, gb_ref, za_ref, gn_ref, mask_ref, sgn_ref, o_ref,
              acc_ref, sf_ref, sb_ref):
    t = q_ref.shape[0]
    c = GLA_CHUNK
    n = t // c
    acc_ref[...] = jnp.zeros_like(acc_ref)
    sf_ref[...] = jnp.zeros_like(sf_ref)
    sb_ref[...] = jnp.zeros_like(sb_ref)

    chunk_tot = jnp.minimum(jnp.sum(gf_ref[...].reshape(n, c, GLA_DK), axis=1),
                            jnp.sum(gb_ref[...].reshape(n, c, GLA_DK), axis=1))
    bounded = jnp.min(chunk_tot) >= -GLA_BOUNDED_DECAY

    def scan(is_bounded):
        def step(i, carry):
            work = []
            for u in range(GLA_CHUNKS_PER_STEP):
                j = i * GLA_CHUNKS_PER_STEP + u
                for reverse, g_ref, s_ref in ((False, gf_ref, sf_ref), (True, gb_ref, sb_ref)):
                    sl = pl.ds(pl.multiple_of(((n - 1 - j) if reverse else j) * c, c), c)
                    local = _gla_chunk_local(q_ref[sl, :], k_ref[sl, :], v_ref[sl, :], g_ref[sl, :],
                                             mask_ref, sgn_ref, reverse, is_bounded)
                    work.append((sl, local, s_ref))
            for sl, local, s_ref in work:
                acc_ref[sl, :] += _gla_chunk_state(local, v_ref[sl, :], s_ref)
            return carry

        lax.fori_loop(0, n // GLA_CHUNKS_PER_STEP, step, 0)

    @pl.when(bounded)
    def _():
        scan(True)

    @pl.when(jnp.logical_not(bounded))
    def _():
        scan(False)

    rows = GLA_FINAL_ROWS

    def fin(i, carry):
        sl = pl.ds(pl.multiple_of(i * rows, rows), rows)
        o = acc_ref[sl, :]
        ms = jnp.mean(o * o, axis=-1, keepdims=True)
        y = o * lax.rsqrt(ms + EPS) * gn_ref[...]
        o_ref[sl, :] = (y * _silu(za_ref[sl, :].astype(F32))).astype(BF16)
        return carry

    lax.fori_loop(0, t // rows, fin, 0, unroll=4)


def _gla_masks():
    c = GLA_CHUNK
    ri = np.arange(c)[:, None]
    ci = np.arange(c)[None, :]
    masks = [ri == ci]
    for m in GLA_LEVELS:
        same = (ri // (2 * m)) == (ci // (2 * m))
        masks.append(same & ((ri % (2 * m)) >= m) & ((ci % (2 * m)) < m))
    for m in GLA_LEVELS:
        same = (ri // (2 * m)) == (ci // (2 * m))
        masks.append(same & ((ri % (2 * m)) < m) & ((ci % (2 * m)) >= m))
    masks.append(ri >= ci)
    masks.append(ri < ci)
    sgn = []
    row = np.arange(c)[:, None] * np.ones((1, GLA_DK), np.int64)
    for reverse in (False, True):
        for m in GLA_LEVELS[1:]:
            upper = (row % (2 * m)) >= m
            sgn.append(np.where(upper != reverse, np.log2(np.e), -np.log2(np.e)))
    return jnp.asarray(np.stack(masks).astype(np.float32)), jnp.asarray(np.stack(sgn).astype(np.float32))


def _gla(qa, ka, va, gf, gb, za, gn, layer, masks, sgn, batch, seq_len):
    n = qa.shape[0]
    t = seq_len
    assert t % (GLA_CHUNK * GLA_CHUNKS_PER_STEP) == 0 and t % GLA_FINAL_ROWS == 0, t
    kspec = pl.BlockSpec((t, GLA_DK), lambda b, h: (b, h))
    vspec = pl.BlockSpec((t, GLA_DV), lambda b, h: (b, h))
    nm = masks.shape[0]
    block_bytes = (2 * _nbytes((t, GLA_DK), BF16) + 3 * _nbytes((t, GLA_DV), BF16)
                   + 2 * _nbytes((t, GLA_DK), F32) + _nbytes(masks.shape, F32) + _nbytes(sgn.shape, F32))
    scratch_bytes = _nbytes((t, GLA_DV), F32) + 2 * _nbytes((GLA_DK, GLA_DV), F32)
    return pl.pallas_call(
        _gla_body,
        grid=(batch, GLA_HEADS),
        in_specs=[kspec, kspec, vspec, kspec, kspec, vspec,
                  _layer_spec((1, GLA_DV), layer),
                  pl.BlockSpec((nm, GLA_CHUNK, GLA_CHUNK), lambda b, h: (0, 0, 0)),
                  pl.BlockSpec(sgn.shape, lambda b, h: (0, 0, 0))],
        out_specs=vspec,
        out_shape=jax.ShapeDtypeStruct((n, GLA_VALUE_DIM), BF16),
        scratch_shapes=[pltpu.VMEM((t, GLA_DV), F32),
                        pltpu.VMEM((GLA_DK, GLA_DV), F32),
                        pltpu.VMEM((GLA_DK, GLA_DV), F32)],
        compiler_params=pltpu.CompilerParams(
            dimension_semantics=("arbitrary", "arbitrary"),
            vmem_limit_bytes=_vmem_limit(block_bytes, scratch_bytes)),
        name="gla",
    )(qa, ka, va, gf, gb, za, gn, masks, sgn)


def _attn_body(qt_ref, k_ref, vt_ref, z_ref, o_ref, s_ref, acc_ref):
    t = k_ref.shape[1]
    tq = ATT_Q_ROWS
    nq = t // tq
    kc = ATT_KEY_CHUNK
    nk = t // kc
    cols = ATT_GROUP * tq

    def load_q(i):
        qt = qt_ref[:, pl.ds(pl.multiple_of(i * tq, tq), tq)]
        return jnp.concatenate([qt[ATT_HEAD_DIM * j:ATT_HEAD_DIM * (j + 1), :] for j in range(ATT_GROUP)], axis=1)

    def scores(q4t, c, m, buf):
        sl = pl.ds(pl.multiple_of(c * kc, kc), kc)
        s = jnp.dot(k_ref[0, sl, :], q4t, preferred_element_type=F32)
        s_ref[buf] = s
        return jnp.maximum(m, jnp.max(s, axis=0, keepdims=True))

    def weighted(c, m_old, m_new, buf):
        sl = pl.ds(pl.multiple_of(c * kc, kc), kc)
        p = jnp.exp2(s_ref[buf] - m_new).astype(BF16)
        acc_ref[...] = (jnp.exp2(m_old - m_new) * acc_ref[...]
                        + jnp.dot(vt_ref[:, sl], p, preferred_element_type=F32))

    m_init = jnp.full((1, cols), -jnp.inf, F32)

    def tile(i, m_first):
        q4t = load_q(i)
        acc_ref[...] = jnp.zeros_like(acc_ref)

        def pair(j, carry):
            m_a, m_b = carry
            c = 2 * j
            m_c = scores(q4t, c + 1, m_b, 1)
            weighted(c, m_a, m_b, 0)
            m_d = scores(q4t, c + 2, m_c, 0)
            weighted(c + 1, m_b, m_c, 1)
            return m_c, m_d

        m_a, m_b = lax.fori_loop(0, nk // 2 - 1, pair, (m_init, m_first), unroll=True)
        m_c = scores(q4t, nk - 1, m_b, 1)
        weighted(nk - 2, m_a, m_b, 0)
        m_next = scores(load_q(jnp.minimum(i + 1, nq - 1)), 0, m_init, 0)
        weighted(nk - 1, m_b, m_c, 1)
        acc = acc_ref[...]
        ot = acc[0:ATT_HEAD_DIM, :] / acc[ATT_HEAD_DIM:ATT_HEAD_DIM + 1, :]
        o = jnp.concatenate([jnp.transpose(ot[:, tq * j:tq * (j + 1)]) for j in range(ATT_GROUP)], axis=1)
        rows = pl.ds(pl.multiple_of(i * tq, tq), tq)
        o_ref[rows, :] = (o * _silu(z_ref[rows, :].astype(F32))).astype(BF16)
        return m_next

    lax.fori_loop(0, nq, tile, scores(load_q(0), 0, m_init, 0), unroll=2)


def _attn(qbt, kb, vbt, zb, batch, seq_len):
    n = zb.shape[0]
    t = seq_len
    tq = ATT_Q_ROWS
    width = ATT_GROUP * ATT_HEAD_DIM
    ospec = pl.BlockSpec((t, width), lambda b, g: (b, g))
    block_bytes = (3 * _nbytes((t, width), BF16) + _nbytes((t, V7X_LANES), BF16)
                   + _nbytes((2 * ATT_HEAD_DIM, t), BF16))
    scratch_bytes = (_nbytes((2, ATT_KEY_CHUNK, ATT_GROUP * tq), F32)
                     + _nbytes((2 * ATT_HEAD_DIM, ATT_GROUP * tq), F32))
    return pl.pallas_call(
        _attn_body,
        grid=(batch, ATT_KV_HEADS),
        in_specs=[pl.BlockSpec((width, t), lambda b, g: (g, b)),
                  pl.BlockSpec((1, t, ATT_HEAD_DIM), lambda b, g: (g, b, 0)),
                  pl.BlockSpec((2 * ATT_HEAD_DIM, t), lambda b, g: (g, b)),
                  ospec],
        out_specs=ospec,
        out_shape=jax.ShapeDtypeStruct((n, ATT_Q_DIM), BF16),
        scratch_shapes=[pltpu.VMEM((2, ATT_KEY_CHUNK, ATT_GROUP * tq), F32),
                        pltpu.VMEM((2 * ATT_HEAD_DIM, ATT_GROUP * tq), F32)],
        compiler_params=pltpu.CompilerParams(
            dimension_semantics=("arbitrary", "arbitrary"),
            vmem_limit_bytes=_vmem_limit(block_bytes, scratch_bytes)),
        name="attn",
    )(qbt, kb, vbt, zb)


def _outproj_body(oa_ref, ob_ref, ma_ref, mb_ref, x_ref, wa_ref, wb_ref, wo_ref, fg_ref, o_ref, *, final):
    ya = jnp.dot(oa_ref[...], wa_ref[...], preferred_element_type=F32)
    yb = jnp.dot(ob_ref[...], wb_ref[...], preferred_element_type=F32)
    merged = (_sigmoid(ma_ref[...].astype(F32)) * ya
              + _sigmoid(mb_ref[...].astype(F32)) * yb)
    y = x_ref[...] + jnp.dot(merged.astype(BF16), wo_ref[...], preferred_element_type=F32)
    if final:
        ms = jnp.mean(y * y, axis=-1, keepdims=True)
        y = y * lax.rsqrt(ms + EPS) * fg_ref[...]
    o_ref[...] = y


def _outproj(oa, ob, ma, mb, x, lw, layer, final_gain, final):
    n = x.shape[0]
    tm = OUTPROJ_ROWS
    row = lambda i: (i, 0)
    const = lambda i: (0, 0)
    act = pl.BlockSpec((tm, D_MODEL), row)
    wspec = _layer_spec((D_MODEL, D_MODEL), layer)
    block_bytes = 4 * _nbytes((tm, D_MODEL), BF16) + 2 * _nbytes((tm, D_MODEL), F32)
    weight_bytes = 3 * _nbytes((D_MODEL, D_MODEL), BF16)
    return pl.pallas_call(
        functools.partial(_outproj_body, final=final),
        grid=(n // tm,),
        in_specs=[act, act, act, act, act, wspec, wspec, wspec,
                  pl.BlockSpec((1, D_MODEL), const, pipeline_mode=pl.Buffered(1))],
        out_specs=act,
        out_shape=jax.ShapeDtypeStruct((n, D_MODEL), F32),
        compiler_params=pltpu.CompilerParams(
            dimension_semantics=("arbitrary",),
            vmem_limit_bytes=_vmem_limit(block_bytes, weight_bytes)),
        name="outproj",
    )(oa, ob, ma, mb, x, lw["w_a"], lw["w_b"], lw["w_o"], final_gain)


def _rope_tables(seq_len):
    rows = seq_len // GRID_W
    r = jnp.repeat(jnp.arange(rows, dtype=F32), GRID_W)
    c = jnp.tile(jnp.arange(GRID_W, dtype=F32), rows)
    nf = ROPE_AXIS_DIM // 2
    inv = ROPE_THETA ** (-jnp.arange(nf, dtype=F32) / nf)
    ang_r = r[:, None] * inv
    ang_c = c[:, None] * inv
    cos = jnp.concatenate([jnp.cos(ang_r), jnp.cos(ang_r), jnp.cos(ang_c), jnp.cos(ang_c)], axis=-1)
    sin = jnp.concatenate([-jnp.sin(ang_r), jnp.sin(ang_r), -jnp.sin(ang_c), jnp.sin(ang_c)], axis=-1)
    reps = V7X_LANES // ATT_HEAD_DIM
    return jnp.tile(cos, (1, reps)), jnp.tile(sin, (1, reps))


def _stacked_weights(norm_g, w_in, w_gate_f, b_gate_f, w_gate_b, b_gate_b, gla_norm_g, q_norm_g, k_norm_g,
                     w_branch_a, w_branch_b, w_out):
    depth = w_in.shape[0]
    pts = np.cumsum((0,) + IN_SIZES)
    wb = w_in.astype(BF16)
    col = lambda i: wb[:, :, pts[i]:pts[i + 1]]
    gate_pad = V7X_LANES - 2 * GATE_RANK
    return {
        "norm_g": norm_g.reshape(depth, 1, D_MODEL),
        "w_gla": jnp.concatenate([col(0), col(1), col(2), col(5)], axis=2),
        "w_gate": jnp.pad(wb[:, :, pts[3]:pts[5]], ((0, 0), (0, 0), (0, gate_pad))),
        "w_gf": jnp.pad(w_gate_f.astype(BF16), ((0, 0), (0, V7X_LANES - GATE_RANK), (0, 0))),
        "w_gb": jnp.pad(w_gate_b.astype(BF16), ((0, 0), (GATE_RANK, gate_pad), (0, 0))),
        "b_gf": b_gate_f.reshape(depth, 1, GLA_KEY_DIM),
        "b_gb": b_gate_b.reshape(depth, 1, GLA_KEY_DIM),
        "w_att": jnp.concatenate([col(6), col(7), col(8), col(9)], axis=2),
        "w_m": jnp.concatenate([col(10), col(11)], axis=2),
        "q_gain": (jnp.tile(q_norm_g, (1, ATT_Q_HEADS)) * ATT_Q_SCALE).reshape(depth, 1, ATT_Q_DIM),
        "k_gain": jnp.tile(k_norm_g, (1, ATT_KV_HEADS)).reshape(depth, 1, ATT_KV_DIM),
        "gla_gain": gla_norm_g.reshape(depth, 1, GLA_DV),
        "w_a": w_branch_a.astype(BF16),
        "w_b": w_branch_b.astype(BF16),
        "w_o": w_out.astype(BF16),
    }


def _trunk(x3, lw, final_gain, masks, sgn, bd):
    batch, seq_len, _ = x3.shape
    x = x3.reshape(batch * seq_len, D_MODEL)
    cos, sin = _rope_tables(seq_len)
    for layer in range(DEPTH):
        qa, ka, va, za, gf, gb, qbt, kb, vbt, zb, ma, mb = _inproj(x, lw, layer, cos, sin, bd, seq_len)
        oa = _gla(qa, ka, va, gf, gb, za, lw["gla_gain"], layer, masks, sgn, batch, seq_len)
        ob = _attn(qbt, kb, vbt, zb, batch, seq_len)
        x = _outproj(oa, ob, ma, mb, x, lw, layer, final_gain, final=(layer == DEPTH - 1))
    return x.reshape(batch, seq_len, D_MODEL)


def kernel(x_prompt, x_sample, norm_g, w_in, w_gate_f, b_gate_f, w_gate_b, b_gate_b, gla_norm_g, q_norm_g,
           k_norm_g, w_branch_a, w_branch_b, w_out, final_norm_g):
    lw = _stacked_weights(norm_g, w_in, w_gate_f, b_gate_f, w_gate_b, b_gate_b, gla_norm_g, q_norm_g,
                          k_norm_g, w_branch_a, w_branch_b, w_out)
    final_gain = final_norm_g.reshape(1, D_MODEL)
    masks, sgn = _gla_masks()
    head = np.arange(V7X_MXU_DIM) // ATT_HEAD_DIM
    bd = jnp.asarray((head[:, None] == head[None, :]).astype(np.float32) / ATT_HEAD_DIM, dtype=BF16)
    y_prompt = _trunk(x_prompt, lw, final_gain, masks, sgn, bd)
    y_sample = _trunk(x_sample, lw, final_gain, masks, sgn, bd)
    return (y_prompt, y_sample)
```

```python
import functools

import numpy as np
import jax
import jax.numpy as jnp
from jax import lax
from jax.experimental import pallas as pl
from jax.experimental.pallas import tpu as pltpu

F32, BF16 = jnp.float32, jnp.bfloat16

D_MODEL = 1024
DEPTH = 4
EPS = 1e-6
GRID_W = 64
GLA_HEADS = 4
GLA_KEY_DIM = D_MODEL // 2
GLA_VALUE_DIM = D_MODEL
GLA_DK = GLA_KEY_DIM // GLA_HEADS
GLA_DV = GLA_VALUE_DIM // GLA_HEADS
GATE_RANK = 16
GATE_NORMALIZER = 16.0
ATT_HEAD_DIM = 64
ATT_Q_HEADS = D_MODEL // ATT_HEAD_DIM
ATT_KV_HEADS = 4
ATT_GROUP = ATT_Q_HEADS // ATT_KV_HEADS
ATT_Q_DIM = ATT_Q_HEADS * ATT_HEAD_DIM
ATT_KV_DIM = ATT_KV_HEADS * ATT_HEAD_DIM
ROPE_AXIS_DIM = ATT_HEAD_DIM // 2
ROPE_THETA = 10000.0
IN_SIZES = (GLA_KEY_DIM, GLA_KEY_DIM, GLA_VALUE_DIM, GATE_RANK, GATE_RANK, GLA_VALUE_DIM,
            ATT_Q_DIM, ATT_KV_DIM, ATT_KV_DIM, ATT_Q_DIM, D_MODEL, D_MODEL)

V7X_LANES = 128
V7X_SUBLANES = 8
V7X_MXU_DIM = 256
V7X_VMEM_BYTES = 64 * 1024 * 1024
V7X_VMEM_BUDGET = V7X_VMEM_BYTES - 8 * 1024 * 1024

INPROJ_ROWS = 512
OUTPROJ_ROWS = 1024
GLA_CHUNK = 64
GLA_LEVELS = (1, 2, 4, 8, 16, 32)
GLA_CHUNKS_PER_STEP = 16
GLA_FINAL_ROWS = 256
GLA_BOUNDED_DECAY = 40.0
GLA_MASK_DIAG = 0
GLA_MASK_TREE_FWD = 1
GLA_MASK_TREE_BWD = GLA_MASK_TREE_FWD + len(GLA_LEVELS)
GLA_MASK_TRI_FWD = GLA_MASK_TREE_BWD + len(GLA_LEVELS)
GLA_MASK_TRI_BWD = GLA_MASK_TRI_FWD + 1
ATT_Q_SCALE = ATT_HEAD_DIM ** -0.5 * float(np.log2(np.e))
ATT_Q_ROWS = 256
ATT_KEY_CHUNK = 512

_NT = (((1,), (1,)), ((), ()))
_TN = (((0,), (0,)), ((), ()))


def _vmem_limit(block_bytes, scratch_bytes=0):
    want = 2 * block_bytes + scratch_bytes + 16 * 1024 * 1024
    return int(min(want, V7X_VMEM_BUDGET))


def _nbytes(shape, dtype):
    return int(np.prod(shape)) * jnp.dtype(dtype).itemsize


def _log_sigmoid(z):
    return jnp.minimum(z, 0.0) - jnp.log(1.0 + jnp.exp(-jnp.abs(z)))


def _sigmoid(z):
    return 0.5 * jnp.tanh(0.5 * z) + 0.5


def _silu(z):
    half = 0.5 * z
    return half * (1.0 + jnp.tanh(half))


def _inproj_body(x_ref, ng_ref, wgla_ref, wgate_ref, wgf_ref, wgb_ref, bgf_ref, bgb_ref,
                 watt_ref, wm_ref, qng_ref, kng_ref, cos_ref, sin_ref, bd_ref,
                 qa_ref, ka_ref, va_ref, za_ref, gf_ref, gb_ref,
                 qbt_ref, kb_ref, vbt_ref, zb_ref, ma_ref, mb_ref):
    x = x_ref[...]
    ms = jnp.mean(x * x, axis=-1, keepdims=True)
    h = (x * lax.rsqrt(ms + EPS) * ng_ref[...]).astype(BF16)

    def proj(w_ref, lo, hi):
        return jnp.dot(h, w_ref[:, lo:hi], preferred_element_type=F32)

    pg = proj(wgate_ref, 0, V7X_LANES).astype(BF16)
    gate_rows = pg.shape[0] // 4

    def decay_gates(part):
        rows = slice(gate_rows * part, gate_rows * (part + 1))
        zf = jnp.dot(pg[rows], wgf_ref[...], preferred_element_type=F32) + bgf_ref[...]
        zb = jnp.dot(pg[rows], wgb_ref[...], preferred_element_type=F32) + bgb_ref[...]
        gf_ref[rows, :] = _log_sigmoid(zf) * (1.0 / GATE_NORMALIZER)
        gb_ref[rows, :] = _log_sigmoid(zb) * (1.0 / GATE_NORMALIZER)

    cos = cos_ref[...]
    sin = sin_ref[...]
    bd = bd_ref[...]
    lane = lax.broadcasted_iota(jnp.int32, cos.shape, 1)
    first_half = (lane & (ROPE_AXIS_DIM // 2)) == 0

    def norm_rope(p, gain):
        sq = (p * p).astype(BF16)
        msq = jnp.dot(sq, bd, preferred_element_type=F32)
        r = lax.rsqrt(msq + EPS)
        xg = p * gain
        outs = []
        for s in range(2):
            xs = xg[:, V7X_LANES * s:V7X_LANES * (s + 1)]
            partner = jnp.where(first_half,
                                pltpu.roll(xs, V7X_LANES - ROPE_AXIS_DIM // 2, 1),
                                pltpu.roll(xs, ROPE_AXIS_DIM // 2, 1))
            outs.append(xs * cos + partner * sin)
        return jnp.concatenate(outs, axis=1) * r

    def attn_q(c):
        lo = V7X_MXU_DIM * c
        p = proj(watt_ref, lo, lo + V7X_MXU_DIM)
        y = norm_rope(p, qng_ref[:, lo:lo + V7X_MXU_DIM])
        qbt_ref[lo:lo + V7X_MXU_DIM, :] = jnp.transpose(y).astype(BF16)

    def attn_kv():
        kk = norm_rope(proj(watt_ref, 1024, 1280), kng_ref[...]).astype(BF16)
        vt = jnp.transpose(proj(watt_ref, 1280, 1536)).astype(BF16)
        ones = jnp.ones((ATT_HEAD_DIM, vt.shape[1]), BF16)
        for g in range(ATT_KV_HEADS):
            kb_ref[g] = kk[:, ATT_HEAD_DIM * g:ATT_HEAD_DIM * (g + 1)]
            vbt_ref[2 * ATT_HEAD_DIM * g:2 * ATT_HEAD_DIM * g + ATT_HEAD_DIM, :] = (
                vt[ATT_HEAD_DIM * g:ATT_HEAD_DIM * (g + 1), :])
            vbt_ref[2 * ATT_HEAD_DIM * g + ATT_HEAD_DIM:2 * ATT_HEAD_DIM * (g + 1), :] = ones

    qa_ref[...] = (proj(wgla_ref, 0, 512) * (GLA_DK ** -0.5)).astype(BF16)
    attn_kv()
    ka_ref[...] = proj(wgla_ref, 512, 1024).astype(BF16)
    attn_q(0)
    va_ref[...] = proj(wgla_ref, 1024, 2048).astype(BF16)
    decay_gates(0)
    za_ref[...] = proj(wgla_ref, 2048, 3072).astype(BF16)
    attn_q(1)
    decay_gates(1)
    zb_ref[...] = proj(watt_ref, 1536, 2560).astype(BF16)
    attn_q(2)
    decay_gates(2)
    ma_ref[...] = proj(wm_ref, 0, 1024).astype(BF16)
    attn_q(3)
    decay_gates(3)
    mb_ref[...] = proj(wm_ref, 1024, 2048).astype(BF16)


def _layer_spec(shape, layer):
    index = (layer,) + (0,) * len(shape)
    return pl.BlockSpec((None,) + tuple(shape), lambda *_: index, pipeline_mode=pl.Buffered(1))


def _inproj(x, lw, layer, cos, sin, bd, seq_len):
    n = x.shape[0]
    tm = INPROJ_ROWS
    pos_blocks = seq_len // tm
    row = lambda i: (i, 0)
    wspec = functools.partial(_layer_spec, layer=layer)

    in_specs = [
        pl.BlockSpec((tm, D_MODEL), row),
        wspec((1, D_MODEL)),
        wspec((D_MODEL, 3072)),
        wspec((D_MODEL, V7X_LANES)),
        wspec((V7X_LANES, GLA_KEY_DIM)),
        wspec((V7X_LANES, GLA_KEY_DIM)),
        wspec((1, GLA_KEY_DIM)),
        wspec((1, GLA_KEY_DIM)),
        wspec((D_MODEL, 2560)),
        wspec((D_MODEL, 2048)),
        wspec((1, ATT_Q_DIM)),
        wspec((1, ATT_KV_DIM)),
        pl.BlockSpec((tm, V7X_LANES), lambda i: (i % pos_blocks, 0)),
        pl.BlockSpec((tm, V7X_LANES), lambda i: (i % pos_blocks, 0)),
        pl.BlockSpec((V7X_MXU_DIM, V7X_MXU_DIM), lambda i: (0, 0), pipeline_mode=pl.Buffered(1)),
    ]
    out_shape = [
        jax.ShapeDtypeStruct((n, GLA_KEY_DIM), BF16),
        jax.ShapeDtypeStruct((n, GLA_KEY_DIM), BF16),
        jax.ShapeDtypeStruct((n, GLA_VALUE_DIM), BF16),
        jax.ShapeDtypeStruct((n, GLA_VALUE_DIM), BF16),
        jax.ShapeDtypeStruct((n, GLA_KEY_DIM), F32),
        jax.ShapeDtypeStruct((n, GLA_KEY_DIM), F32),
        jax.ShapeDtypeStruct((ATT_Q_DIM, n), BF16),
        jax.ShapeDtypeStruct((ATT_KV_HEADS, n, ATT_HEAD_DIM), BF16),
        jax.ShapeDtypeStruct((2 * ATT_KV_DIM, n), BF16),
        jax.ShapeDtypeStruct((n, ATT_Q_DIM), BF16),
        jax.ShapeDtypeStruct((n, D_MODEL), BF16),
        jax.ShapeDtypeStruct((n, D_MODEL), BF16),
    ]
    out_specs = [
        pl.BlockSpec((tm, GLA_KEY_DIM), row),
        pl.BlockSpec((tm, GLA_KEY_DIM), row),
        pl.BlockSpec((tm, GLA_VALUE_DIM), row),
        pl.BlockSpec((tm, GLA_VALUE_DIM), row),
        pl.BlockSpec((tm, GLA_KEY_DIM), row),
        pl.BlockSpec((tm, GLA_KEY_DIM), row),
        pl.BlockSpec((ATT_Q_DIM, tm), lambda i: (0, i)),
        pl.BlockSpec((ATT_KV_HEADS, tm, ATT_HEAD_DIM), lambda i: (0, i, 0)),
        pl.BlockSpec((2 * ATT_KV_DIM, tm), lambda i: (0, i)),
        pl.BlockSpec((tm, ATT_Q_DIM), row),
        pl.BlockSpec((tm, D_MODEL), row),
        pl.BlockSpec((tm, D_MODEL), row),
    ]
    weight_bytes = _nbytes((D_MODEL, 3072 + V7X_LANES + 2560 + 2048), BF16)
    block_bytes = (_nbytes((tm, D_MODEL), F32) + _nbytes((tm, 7 * 1024 + 512), BF16)
                   + 2 * _nbytes((tm, GLA_KEY_DIM), F32) + 2 * _nbytes((tm, V7X_LANES), F32))
    return pl.pallas_call(
        _inproj_body,
        grid=(n // tm,),
        in_specs=in_specs,
        out_specs=out_specs,
        out_shape=out_shape,
        compiler_params=pltpu.CompilerParams(
            dimension_semantics=("arbitrary",),
            vmem_limit_bytes=_vmem_limit(block_bytes, weight_bytes)),
        name="inproj",
    )(x, lw["norm_g"], lw["w_gla"], lw["w_gate"], lw["w_gf"], lw["w_gb"], lw["b_gf"], lw["b_gb"],
      lw["w_att"], lw["w_m"], lw["q_gain"], lw["k_gain"], cos, sin, bd)


def _gla_boundary(beta, m, reverse):
    c, width = beta.shape
    two_m = 2 * m
    off = m if reverse else m - 1
    pieces = []
    if two_m >= V7X_SUBLANES:
        for v in range(c // V7X_SUBLANES):
            r = (V7X_SUBLANES * v // two_m) * two_m + off
            pieces.append(jnp.broadcast_to(beta[r:r + 1, :], (V7X_SUBLANES, width)))
    else:
        sub = lax.broadcasted_iota(jnp.int32, (V7X_SUBLANES, width), 0)
        per = V7X_SUBLANES // two_m
        for v in range(c // V7X_SUBLANES):
            rows = [jnp.broadcast_to(beta[V7X_SUBLANES * v + two_m * j + off:V7X_SUBLANES * v + two_m * j + off + 1, :],
                                     (V7X_SUBLANES, width)) for j in range(per)]
            acc = rows[-1]
            for j in range(per - 2, -1, -1):
                acc = jnp.where(sub < two_m * (j + 1), rows[j], acc)
            pieces.append(acc)
    return jnp.concatenate(pieces, axis=0)


def _gla_tree_scores(q, k, g, beta, mask_ref, sgn_ref, reverse):
    row = lax.broadcasted_iota(jnp.int32, g.shape, 0)
    nlev = len(GLA_LEVELS)
    mask_base = GLA_MASK_TREE_BWD if reverse else GLA_MASK_TREE_FWD
    sgn_base = (nlev - 1) if reverse else 0
    a = None
    if not reverse:
        a = mask_ref[GLA_MASK_DIAG] * lax.dot_general(q, k, _NT, preferred_element_type=F32)
    for li, m in enumerate(GLA_LEVELS):
        if m == 1:
            parity = (row & 1) == (0 if reverse else 1)
            f = jnp.exp(jnp.where(parity, g, 0.0))
        else:
            f = jnp.exp2((beta - _gla_boundary(beta, m, reverse)) * sgn_ref[sgn_base + li - 1])
        fb = f.astype(BF16)
        am = mask_ref[mask_base + li] * lax.dot_general(q * fb, k * fb, _NT, preferred_element_type=F32)
        a = am if a is None else a + am
    return a


def _gla_chunk_local(q, k, v, g, mask_ref, sgn_ref, reverse, bounded):
    c = GLA_CHUNK
    row = lax.broadcasted_iota(jnp.int32, g.shape, 0)
    beta = g
    for sh in GLA_LEVELS:
        if reverse:
            beta = beta + jnp.where(row < c - sh, pltpu.roll(beta, c - sh, 0), 0.0)
        else:
            beta = beta + jnp.where(row >= sh, pltpu.roll(beta, sh, 0), 0.0)
    qd = q * jnp.exp(beta).astype(BF16)
    tot = beta[0:1, :] if reverse else beta[c - 1:c, :]
    kd = k * jnp.exp(tot - beta).astype(BF16)
    decay_col = jnp.transpose(jnp.broadcast_to(jnp.exp(tot), (V7X_SUBLANES, GLA_DK)))[:, 0:1]
    if bounded:
        kg = k * jnp.exp(-beta).astype(BF16)
        a = (mask_ref[GLA_MASK_TRI_BWD if reverse else GLA_MASK_TRI_FWD]
             * lax.dot_general(qd, kg, _NT, preferred_element_type=F32))
    else:
        a = _gla_tree_scores(q, k, g, beta, mask_ref, sgn_ref, reverse)
    kv = lax.dot_general(kd, v, _TN, preferred_element_type=F32)
    return qd, a.astype(BF16), kv, decay_col


def _gla_chunk_state(local, v, s_ref):
    qd, a, kv, decay_col = local
    s = s_ref[...]
    o = jnp.dot(qd, s.astype(BF16), preferred_element_type=F32)
    s_ref[...] = decay_col * s + kv
    return o + jnp.dot(a, v, preferred_element_type=F32)


def _gla_body(q_ref, k_ref, v_ref, gf_ref, gb_ref, za_ref, gn_ref, mask_ref, sgn_ref, o_ref,
              acc_ref, sf_ref, sb_ref):
    t = q_ref.shape[0]
    c = GLA_CHUNK
    n = t // c
    acc_ref[...] = jnp.zeros_like(acc_ref)
    sf_ref[...] = jnp.zeros_like(sf_ref)
    sb_ref[...] = jnp.zeros_like(sb_ref)

    chunk_tot = jnp.minimum(jnp.sum(gf_ref[...].reshape(n, c, GLA_DK), axis=1),
                            jnp.sum(gb_ref[...].reshape(n, c, GLA_DK), axis=1))
    bounded = jnp.min(chunk_tot) >= -GLA_BOUNDED_DECAY

    def scan(is_bounded):
        def step(i, carry):
            work = []
            for u in range(GLA_CHUNKS_PER_STEP):
                j = i * GLA_CHUNKS_PER_STEP + u
                for reverse, g_ref, s_ref in ((False, gf_ref, sf_ref), (True, gb_ref, sb_ref)):
                    sl = pl.ds(pl.multiple_of(((n - 1 - j) if reverse else j) * c, c), c)
                    local = _gla_chunk_local(q_ref[sl, :], k_ref[sl, :], v_ref[sl, :], g_ref[sl, :],
                                             mask_ref, sgn_ref, reverse, is_bounded)
                    work.append((sl, local, s_ref))
            for sl, local, s_ref in work:
                acc_ref[sl, :] += _gla_chunk_state(local, v_ref[sl, :], s_ref)
            return carry

        lax.fori_loop(0, n // GLA_CHUNKS_PER_STEP, step, 0)

    @pl.when(bounded)
    def _():
        scan(True)

    @pl.when(jnp.logical_not(bounded))
    def _():
        scan(False)

    rows = GLA_FINAL_ROWS

    def fin(i, carry):
        sl = pl.ds(pl.multiple_of(i * rows, rows), rows)
        o = acc_ref[sl, :]
        ms = jnp.mean(o * o, axis=-1, keepdims=True)
        y = o * lax.rsqrt(ms + EPS) * gn_ref[...]
        o_ref[sl, :] = (y * _silu(za_ref[sl, :].astype(F32))).astype(BF16)
        return carry

    lax.fori_loop(0, t // rows, fin, 0, unroll=4)


def _gla_masks():
    c = GLA_CHUNK
    ri = np.arange(c)[:, None]
    ci = np.arange(c)[None, :]
    masks = [ri == ci]
    for m in GLA_LEVELS:
        same = (ri // (2 * m)) == (ci // (2 * m))
        masks.append(same & ((ri % (2 * m)) >= m) & ((ci % (2 * m)) < m))
    for m in GLA_LEVELS:
        same = (ri // (2 * m)) == (ci // (2 * m))
        masks.append(same & ((ri % (2 * m)) < m) & ((ci % (2 * m)) >= m))
    masks.append(ri >= ci)
    masks.append(ri < ci)
    sgn = []
    row = np.arange(c)[:, None] * np.ones((1, GLA_DK), np.int64)
    for reverse in (False, True):
        for m in GLA_LEVELS[1:]:
            upper = (row % (2 * m)) >= m
            sgn.append(np.where(upper != reverse, np.log2(np.e), -np.log2(np.e)))
    return jnp.asarray(np.stack(masks).astype(np.float32)), jnp.asarray(np.stack(sgn).astype(np.float32))


def _gla(qa, ka, va, gf, gb, za, gn, layer, masks, sgn, batch, seq_len):
    n = qa.shape[0]
    t = seq_len
    assert t % (GLA_CHUNK * GLA_CHUNKS_PER_STEP) == 0 and t % GLA_FINAL_ROWS == 0, t
    kspec = pl.BlockSpec((t, GLA_DK), lambda b, h: (b, h))
    vspec = pl.BlockSpec((t, GLA_DV), lambda b, h: (b, h))
    nm = masks.shape[0]
    block_bytes = (2 * _nbytes((t, GLA_DK), BF16) + 3 * _nbytes((t, GLA_DV), BF16)
                   + 2 * _nbytes((t, GLA_DK), F32) + _nbytes(masks.shape, F32) + _nbytes(sgn.shape, F32))
    scratch_bytes = _nbytes((t, GLA_DV), F32) + 2 * _nbytes((GLA_DK, GLA_DV), F32)
    return pl.pallas_call(
        _gla_body,
        grid=(batch, GLA_HEADS),
        in_specs=[kspec, kspec, vspec, kspec, kspec, vspec,
                  _layer_spec((1, GLA_DV), layer),
                  pl.BlockSpec((nm, GLA_CHUNK, GLA_CHUNK), lambda b, h: (0, 0, 0)),
                  pl.BlockSpec(sgn.shape, lambda b, h: (0, 0, 0))],
        out_specs=vspec,
        out_shape=jax.ShapeDtypeStruct((n, GLA_VALUE_DIM), BF16),
        scratch_shapes=[pltpu.VMEM((t, GLA_DV), F32),
                        pltpu.VMEM((GLA_DK, GLA_DV), F32),
                        pltpu.VMEM((GLA_DK, GLA_DV), F32)],
        compiler_params=pltpu.CompilerParams(
            dimension_semantics=("arbitrary", "arbitrary"),
            vmem_limit_bytes=_vmem_limit(block_bytes, scratch_bytes)),
        name="gla",
    )(qa, ka, va, gf, gb, za, gn, masks, sgn)


def _attn_body(qt_ref, k_ref, vt_ref, z_ref, o_ref, s_ref, acc_ref):
    t = k_ref.shape[1]
    tq = ATT_Q_ROWS
    nq = t // tq
    kc = ATT_KEY_CHUNK
    nk = t // kc
    cols = ATT_GROUP * tq

    def load_q(i):
        qt = qt_ref[:, pl.ds(pl.multiple_of(i * tq, tq), tq)]
        return jnp.concatenate([qt[ATT_HEAD_DIM * j:ATT_HEAD_DIM * (j + 1), :] for j in range(ATT_GROUP)], axis=1)

    def scores(q4t, c, m, buf):
        sl = pl.ds(pl.multiple_of(c * kc, kc), kc)
        s = jnp.dot(k_ref[0, sl, :], q4t, preferred_element_type=F32)
        s_ref[buf] = s
        return jnp.maximum(m, jnp.max(s, axis=0, keepdims=True))

    def weighted(c, m_old, m_new, buf):
        sl = pl.ds(pl.multiple_of(c * kc, kc), kc)
        p = jnp.exp2(s_ref[buf] - m_new).astype(BF16)
        acc_ref[...] = (jnp.exp2(m_old - m_new) * acc_ref[...]
                        + jnp.dot(vt_ref[:, sl], p, preferred_element_type=F32))

    m_init = jnp.full((1, cols), -jnp.inf, F32)

    def tile(i, m_first):
        q4t = load_q(i)
        acc_ref[...] = jnp.zeros_like(acc_ref)

        def pair(j, carry):
            m_a, m_b = carry
            c = 2 * j
            m_c = scores(q4t, c + 1, m_b, 1)
            weighted(c, m_a, m_b, 0)
            m_d = scores(q4t, c + 2, m_c, 0)
            weighted(c + 1, m_b, m_c, 1)
            return m_c, m_d

        m_a, m_b = lax.fori_loop(0, nk // 2 - 1, pair, (m_init, m_first), unroll=True)
        m_c = scores(q4t, nk - 1, m_b, 1)
        weighted(nk - 2, m_a, m_b, 0)
        m_next = scores(load_q(jnp.minimum(i + 1, nq - 1)), 0, m_init, 0)
        weighted(nk - 1, m_b, m_c, 1)
        acc = acc_ref[...]
        ot = acc[0:ATT_HEAD_DIM, :] / acc[ATT_HEAD_DIM:ATT_HEAD_DIM + 1, :]
        o = jnp.concatenate([jnp.transpose(ot[:, tq * j:tq * (j + 1)]) for j in range(ATT_GROUP)], axis=1)
        rows = pl.ds(pl.multiple_of(i * tq, tq), tq)
        o_ref[rows, :] = (o * _silu(z_ref[rows, :].astype(F32))).astype(BF16)
        return m_next

    lax.fori_loop(0, nq, tile, scores(load_q(0), 0, m_init, 0), unroll=2)


def _attn(qbt, kb, vbt, zb, batch, seq_len):
    n = zb.shape[0]
    t = seq_len
    tq = ATT_Q_ROWS
    width = ATT_GROUP * ATT_HEAD_DIM
    ospec = pl.BlockSpec((t, width), lambda b, g: (b, g))
    block_bytes = (3 * _nbytes((t, width), BF16) + _nbytes((t, V7X_LANES), BF16)
                   + _nbytes((2 * ATT_HEAD_DIM, t), BF16))
    scratch_bytes = (_nbytes((2, ATT_KEY_CHUNK, ATT_GROUP * tq), F32)
                     + _nbytes((2 * ATT_HEAD_DIM, ATT_GROUP * tq), F32))
    return pl.pallas_call(
        _attn_body,
        grid=(batch, ATT_KV_HEADS),
        in_specs=[pl.BlockSpec((width, t), lambda b, g: (g, b)),
                  pl.BlockSpec((1, t, ATT_HEAD_DIM), lambda b, g: (g, b, 0)),
                  pl.BlockSpec((2 * ATT_HEAD_DIM, t), lambda b, g: (g, b)),
                  ospec],
        out_specs=ospec,
        out_shape=jax.ShapeDtypeStruct((n, ATT_Q_DIM), BF16),
        scratch_shapes=[pltpu.VMEM((2, ATT_KEY_CHUNK, ATT_GROUP * tq), F32),
                        pltpu.VMEM((2 * ATT_HEAD_DIM, ATT_GROUP * tq), F32)],
        compiler_params=pltpu.CompilerParams(
            dimension_semantics=("arbitrary", "arbitrary"),
            vmem_limit_bytes=_vmem_limit(block_bytes, scratch_bytes)),
        name="attn",
    )(qbt, kb, vbt, zb)


def _outproj_body(oa_ref, ob_ref, ma_ref, mb_ref, x_ref, wa_ref, wb_ref, wo_ref, fg_ref, o_ref, *, final):
    ya = jnp.dot(oa_ref[...], wa_ref[...], preferred_element_type=F32)
    yb = jnp.dot(ob_ref[...], wb_ref[...], preferred_element_type=F32)
    merged = (_sigmoid(ma_ref[...].astype(F32)) * ya
              + _sigmoid(mb_ref[...].astype(F32)) * yb)
    y = x_ref[...] + jnp.dot(merged.astype(BF16), wo_ref[...], preferred_element_type=F32)
    if final:
        ms = jnp.mean(y * y, axis=-1, keepdims=True)
        y = y * lax.rsqrt(ms + EPS) * fg_ref[...]
    o_ref[...] = y


def _outproj(oa, ob, ma, mb, x, lw, layer, final_gain, final):
    n = x.shape[0]
    tm = OUTPROJ_ROWS
    row = lambda i: (i, 0)
    const = lambda i: (0, 0)
    act = pl.BlockSpec((tm, D_MODEL), row)
    wspec = _layer_spec((D_MODEL, D_MODEL), layer)
    block_bytes = 4 * _nbytes((tm, D_MODEL), BF16) + 2 * _nbytes((tm, D_MODEL), F32)
    weight_bytes = 3 * _nbytes((D_MODEL, D_MODEL), BF16)
    return pl.pallas_call(
        functools.partial(_outproj_body, final=final),
        grid=(n // tm,),
        in_specs=[act, act, act, act, act, wspec, wspec, wspec,
                  pl.BlockSpec((1, D_MODEL), const, pipeline_mode=pl.Buffered(1))],
        out_specs=act,
        out_shape=jax.ShapeDtypeStruct((n, D_MODEL), F32),
        compiler_params=pltpu.CompilerParams(
            dimension_semantics=("arbitrary",),
            vmem_limit_bytes=_vmem_limit(block_bytes, weight_bytes)),
        name="outproj",
    )(oa, ob, ma, mb, x, lw["w_a"], lw["w_b"], lw["w_o"], final_gain)


def _rope_tables(seq_len):
    rows = seq_len // GRID_W
    r = jnp.repeat(jnp.arange(rows, dtype=F32), GRID_W)
    c = jnp.tile(jnp.arange(GRID_W, dtype=F32), rows)
    nf = ROPE_AXIS_DIM // 2
    inv = ROPE_THETA ** (-jnp.arange(nf, dtype=F32) / nf)
    ang_r = r[:, None] * inv
    ang_c = c[:, None] * inv
    cos = jnp.concatenate([jnp.cos(ang_r), jnp.cos(ang_r), jnp.cos(ang_c), jnp.cos(ang_c)], axis=-1)
    sin = jnp.concatenate([-jnp.sin(ang_r), jnp.sin(ang_r), -jnp.sin(ang_c), jnp.sin(ang_c)], axis=-1)
    reps = V7X_LANES // ATT_HEAD_DIM
    return jnp.tile(cos, (1, reps)), jnp.tile(sin, (1, reps))


def _stacked_weights(norm_g, w_in, w_gate_f, b_gate_f, w_gate_b, b_gate_b, gla_norm_g, q_norm_g, k_norm_g,
                     w_branch_a, w_branch_b, w_out):
    depth = w_in.shape[0]
    pts = np.cumsum((0,) + IN_SIZES)
    wb = w_in.astype(BF16)
    col = lambda i: wb[:, :, pts[i]:pts[i + 1]]
    gate_pad = V7X_LANES - 2 * GATE_RANK
    return {
        "norm_g": norm_g.reshape(depth, 1, D_MODEL),
        "w_gla": jnp.concatenate([col(0), col(1), col(2), col(5)], axis=2),
        "w_gate": jnp.pad(wb[:, :, pts[3]:pts[5]], ((0, 0), (0, 0), (0, gate_pad))),
        "w_gf": jnp.pad(w_gate_f.astype(BF16), ((0, 0), (0, V7X_LANES - GATE_RANK), (0, 0))),
        "w_gb": jnp.pad(w_gate_b.astype(BF16), ((0, 0), (GATE_RANK, gate_pad), (0, 0))),
        "b_gf": b_gate_f.reshape(depth, 1, GLA_KEY_DIM),
        "b_gb": b_gate_b.reshape(depth, 1, GLA_KEY_DIM),
        "w_att": jnp.concatenate([col(6), col(7), col(8), col(9)], axis=2),
        "w_m": jnp.concatenate([col(10), col(11)], axis=2),
        "q_gain": (jnp.tile(q_norm_g, (1, ATT_Q_HEADS)) * ATT_Q_SCALE).reshape(depth, 1, ATT_Q_DIM),
        "k_gain": jnp.tile(k_norm_g, (1, ATT_KV_HEADS)).reshape(depth, 1, ATT_KV_DIM),
        "gla_gain": gla_norm_g.reshape(depth, 1, GLA_DV),
        "w_a": w_branch_a.astype(BF16),
        "w_b": w_branch_b.astype(BF16),
        "w_o": w_out.astype(BF16),
    }


def _trunk(x3, lw, final_gain, masks, sgn, bd):
    batch, seq_len, _ = x3.shape
    x = x3.reshape(batch * seq_len, D_MODEL)
    cos, sin = _rope_tables(seq_len)
    for layer in range(DEPTH):
        qa, ka, va, za, gf, gb, qbt, kb, vbt, zb, ma, mb = _inproj(x, lw, layer, cos, sin, bd, seq_len)
        oa = _gla(qa, ka, va, gf, gb, za, lw["gla_gain"], layer, masks, sgn, batch, seq_len)
        ob = _attn(qbt, kb, vbt, zb, batch, seq_len)
        x = _outproj(oa, ob, ma, mb, x, lw, layer, final_gain, final=(layer == DEPTH - 1))
    return x.reshape(batch, seq_len, D_MODEL)


def kernel(x_prompt, x_sample, norm_g, w_in, w_gate_f, b_gate_f, w_gate_b, b_gate_b, gla_norm_g, q_norm_g,
           k_norm_g, w_branch_a, w_branch_b, w_out, final_norm_g):
    lw = _stacked_weights(norm_g, w_in, w_gate_f, b_gate_f, w_gate_b, b_gate_b, gla_norm_g, q_norm_g,
                          k_norm_g, w_branch_a, w_branch_b, w_out)
    final_gain = final_norm_g.reshape(1, D_MODEL)
    masks, sgn = _gla_masks()
    head = np.arange(V7X_MXU_DIM) // ATT_HEAD_DIM
    bd = jnp.asarray((head[:, None] == head[None, :]).astype(np.float32) / ATT_HEAD_DIM, dtype=BF16)
    y_prompt = _trunk(x_prompt, lw, final_gain, masks, sgn, bd)
    y_sample = _trunk(x_sample, lw, final_gain, masks, sgn, bd)
    return (y_prompt, y_sample)
```

```python
import functools

import numpy as np
import jax
import jax.numpy as jnp
from jax import lax
from jax.experimental import pallas as pl
from jax.experimental.pallas import tpu as pltpu

F32, BF16 = jnp.float32, jnp.bfloat16

D_MODEL = 1024
DEPTH = 4
EPS = 1e-6
GRID_W = 64
GLA_HEADS = 4
GLA_KEY_DIM = D_MODEL // 2
GLA_VALUE_DIM = D_MODEL
GLA_DK = GLA_KEY_DIM // GLA_HEADS
GLA_DV = GLA_VALUE_DIM // GLA_HEADS
GATE_RANK = 16
GATE_NORMALIZER = 16.0
ATT_HEAD_DIM = 64
ATT_Q_HEADS = D_MODEL // ATT_HEAD_DIM
ATT_KV_HEADS = 4
ATT_GROUP = ATT_Q_HEADS // ATT_KV_HEADS
ATT_Q_DIM = ATT_Q_HEADS * ATT_HEAD_DIM
ATT_KV_DIM = ATT_KV_HEADS * ATT_HEAD_DIM
ROPE_AXIS_DIM = ATT_HEAD_DIM // 2
ROPE_THETA = 10000.0
IN_SIZES = (GLA_KEY_DIM, GLA_KEY_DIM, GLA_VALUE_DIM, GATE_RANK, GATE_RANK, GLA_VALUE_DIM,
            ATT_Q_DIM, ATT_KV_DIM, ATT_KV_DIM, ATT_Q_DIM, D_MODEL, D_MODEL)

V7X_LANES = 128
V7X_SUBLANES = 8
V7X_MXU_DIM = 256
V7X_VMEM_BYTES = 64 * 1024 * 1024
V7X_VMEM_BUDGET = V7X_VMEM_BYTES - 8 * 1024 * 1024

INPROJ_ROWS = 512
OUTPROJ_ROWS = 1024
GLA_CHUNK = 64
GLA_LEVELS = (1, 2, 4, 8, 16, 32)
GLA_CHUNKS_PER_STEP = 16
GLA_FINAL_ROWS = 256
GLA_BOUNDED_DECAY = 40.0
GLA_MASK_DIAG = 0
GLA_MASK_TREE_FWD = 1
GLA_MASK_TREE_BWD = GLA_MASK_TREE_FWD + len(GLA_LEVELS)
GLA_MASK_TRI_FWD = GLA_MASK_TREE_BWD + len(GLA_LEVELS)
GLA_MASK_TRI_BWD = GLA_MASK_TRI_FWD + 1
ATT_Q_SCALE = ATT_HEAD_DIM ** -0.5 * float(np.log2(np.e))
ATT_Q_ROWS = 256
ATT_KEY_CHUNK = 512

_NT = (((1,), (1,)), ((), ()))
_TN = (((0,), (0,)), ((), ()))


def _vmem_limit(block_bytes, scratch_bytes=0):
    want = 2 * block_bytes + scratch_bytes + 16 * 1024 * 1024
    return int(min(want, V7X_VMEM_BUDGET))


def _nbytes(shape, dtype):
    return int(np.prod(shape)) * jnp.dtype(dtype).itemsize


def _log_sigmoid(z):
    return jnp.minimum(z, 0.0) - jnp.log(1.0 + jnp.exp(-jnp.abs(z)))


def _sigmoid(z):
    return 0.5 * jnp.tanh(0.5 * z) + 0.5


def _silu(z):
    half = 0.5 * z
    return half * (1.0 + jnp.tanh(half))


def _inproj_body(x_ref, ng_ref, wgla_ref, wgate_ref, wgf_ref, wgb_ref, bgf_ref, bgb_ref,
                 watt_ref, wm_ref, qng_ref, kng_ref, cos_ref, sin_ref, bd_ref,
                 qa_ref, ka_ref, va_ref, za_ref, gf_ref, gb_ref,
                 qbt_ref, kb_ref, vbt_ref, zb_ref, ma_ref, mb_ref):
    x = x_ref[...]
    ms = jnp.mean(x * x, axis=-1, keepdims=True)
    h = (x * lax.rsqrt(ms + EPS) * ng_ref[...]).astype(BF16)

    def proj(w_ref, lo, hi):
        return jnp.dot(h, w_ref[:, lo:hi], preferred_element_type=F32)

    pg = proj(wgate_ref, 0, V7X_LANES).astype(BF16)
    gate_rows = pg.shape[0] // 4

    def decay_gates(part):
        rows = slice(gate_rows * part, gate_rows * (part + 1))
        zf = jnp.dot(pg[rows], wgf_ref[...], preferred_element_type=F32) + bgf_ref[...]
        zb = jnp.dot(pg[rows], wgb_ref[...], preferred_element_type=F32) + bgb_ref[...]
        gf_ref[rows, :] = _log_sigmoid(zf) * (1.0 / GATE_NORMALIZER)
        gb_ref[rows, :] = _log_sigmoid(zb) * (1.0 / GATE_NORMALIZER)

    cos = cos_ref[...]
    sin = sin_ref[...]
    bd = bd_ref[...]
    lane = lax.broadcasted_iota(jnp.int32, cos.shape, 1)
    first_half = (lane & (ROPE_AXIS_DIM // 2)) == 0

    def norm_rope(p, gain):
        sq = (p * p).astype(BF16)
        msq = jnp.dot(sq, bd, preferred_element_type=F32)
        r = lax.rsqrt(msq + EPS)
        xg = p * gain
        outs = []
        for s in range(2):
            xs = xg[:, V7X_LANES * s:V7X_LANES * (s + 1)]
            partner = jnp.where(first_half,
                                pltpu.roll(xs, V7X_LANES - ROPE_AXIS_DIM // 2, 1),
                                pltpu.roll(xs, ROPE_AXIS_DIM // 2, 1))
            outs.append(xs * cos + partner * sin)
        return jnp.concatenate(outs, axis=1) * r

    def attn_q(c):
        lo = V7X_MXU_DIM * c
        p = proj(watt_ref, lo, lo + V7X_MXU_DIM)
        y = norm_rope(p, qng_ref[:, lo:lo + V7X_MXU_DIM])
        qbt_ref[lo:lo + V7X_MXU_DIM, :] = jnp.transpose(y).astype(BF16)

    def attn_kv():
        kk = norm_rope(proj(watt_ref, 1024, 1280), kng_ref[...]).astype(BF16)
        vt = jnp.transpose(proj(watt_ref, 1280, 1536)).astype(BF16)
        ones = jnp.ones((ATT_HEAD_DIM, vt.shape[1]), BF16)
        for g in range(ATT_KV_HEADS):
            kb_ref[g] = kk[:, ATT_HEAD_DIM * g:ATT_HEAD_DIM * (g + 1)]
            vbt_ref[2 * ATT_HEAD_DIM * g:2 * ATT_HEAD_DIM * g + ATT_HEAD_DIM, :] = (
                vt[ATT_HEAD_DIM * g:ATT_HEAD_DIM * (g + 1), :])
            vbt_ref[2 * ATT_HEAD_DIM * g + ATT_HEAD_DIM:2 * ATT_HEAD_DIM * (g + 1), :] = ones

    qa_ref[...] = (proj(wgla_ref, 0, 512) * (GLA_DK ** -0.5)).astype(BF16)
    attn_kv()
    ka_ref[...] = proj(wgla_ref, 512, 1024).astype(BF16)
    attn_q(0)
    va_ref[...] = proj(wgla_ref, 1024, 2048).astype(BF16)
    decay_gates(0)
    za_ref[...] = proj(wgla_ref, 2048, 3072).astype(BF16)
    attn_q(1)
    decay_gates(1)
    zb_ref[...] = proj(watt_ref, 1536, 2560).astype(BF16)
    attn_q(2)
    decay_gates(2)
    ma_ref[...] = proj(wm_ref, 0, 1024).astype(BF16)
    attn_q(3)
    decay_gates(3)
    mb_ref[...] = proj(wm_ref, 1024, 2048).astype(BF16)


def _layer_spec(shape, layer):
    index = (layer,) + (0,) * len(shape)
    return pl.BlockSpec((None,) + tuple(shape), lambda *_: index, pipeline_mode=pl.Buffered(1))


def _inproj(x, lw, layer, cos, sin, bd, seq_len):
    n = x.shape[0]
    tm = INPROJ_ROWS
    pos_blocks = seq_len // tm
    row = lambda i: (i, 0)
    wspec = functools.partial(_layer_spec, layer=layer)

    in_specs = [
        pl.BlockSpec((tm, D_MODEL), row),
        wspec((1, D_MODEL)),
        wspec((D_MODEL, 3072)),
        wspec((D_MODEL, V7X_LANES)),
        wspec((V7X_LANES, GLA_KEY_DIM)),
        wspec((V7X_LANES, GLA_KEY_DIM)),
        wspec((1, GLA_KEY_DIM)),
        wspec((1, GLA_KEY_DIM)),
        wspec((D_MODEL, 2560)),
        wspec((D_MODEL, 2048)),
        wspec((1, ATT_Q_DIM)),
        wspec((1, ATT_KV_DIM)),
        pl.BlockSpec((tm, V7X_LANES), lambda i: (i % pos_blocks, 0)),
        pl.BlockSpec((tm, V7X_LANES), lambda i: (i % pos_blocks, 0)),
        pl.BlockSpec((V7X_MXU_DIM, V7X_MXU_DIM), lambda i: (0, 0), pipeline_mode=pl.Buffered(1)),
    ]
    out_shape = [
        jax.ShapeDtypeStruct((n, GLA_KEY_DIM), BF16),
        jax.ShapeDtypeStruct((n, GLA_KEY_DIM), BF16),
        jax.ShapeDtypeStruct((n, GLA_VALUE_DIM), BF16),
        jax.ShapeDtypeStruct((n, GLA_VALUE_DIM), BF16),
        jax.ShapeDtypeStruct((n, GLA_KEY_DIM), F32),
        jax.ShapeDtypeStruct((n, GLA_KEY_DIM), F32),
        jax.ShapeDtypeStruct((ATT_Q_DIM, n), BF16),
        jax.ShapeDtypeStruct((ATT_KV_HEADS, n, ATT_HEAD_DIM), BF16),
        jax.ShapeDtypeStruct((2 * ATT_KV_DIM, n), BF16),
        jax.ShapeDtypeStruct((n, ATT_Q_DIM), BF16),
        jax.ShapeDtypeStruct((n, D_MODEL), BF16),
        jax.ShapeDtypeStruct((n, D_MODEL), BF16),
    ]
    out_specs = [
        pl.BlockSpec((tm, GLA_KEY_DIM), row),
        pl.BlockSpec((tm, GLA_KEY_DIM), row),
        pl.BlockSpec((tm, GLA_VALUE_DIM), row),
        pl.BlockSpec((tm, GLA_VALUE_DIM), row),
        pl.BlockSpec((tm, GLA_KEY_DIM), row),
        pl.BlockSpec((tm, GLA_KEY_DIM), row),
        pl.BlockSpec((ATT_Q_DIM, tm), lambda i: (0, i)),
        pl.BlockSpec((ATT_KV_HEADS, tm, ATT_HEAD_DIM), lambda i: (0, i, 0)),
        pl.BlockSpec((2 * ATT_KV_DIM, tm), lambda i: (0, i)),
        pl.BlockSpec((tm, ATT_Q_DIM), row),
        pl.BlockSpec((tm, D_MODEL), row),
        pl.BlockSpec((tm, D_MODEL), row),
    ]
    weight_bytes = _nbytes((D_MODEL, 3072 + V7X_LANES + 2560 + 2048), BF16)
    block_bytes = (_nbytes((tm, D_MODEL), F32) + _nbytes((tm, 7 * 1024 + 512), BF16)
                   + 2 * _nbytes((tm, GLA_KEY_DIM), F32) + 2 * _nbytes((tm, V7X_LANES), F32))
    return pl.pallas_call(
        _inproj_body,
        grid=(n // tm,),
        in_specs=in_specs,
        out_specs=out_specs,
        out_shape=out_shape,
        compiler_params=pltpu.CompilerParams(
            dimension_semantics=("arbitrary",),
            vmem_limit_bytes=_vmem_limit(block_bytes, weight_bytes)),
        name="inproj",
    )(x, lw["norm_g"], lw["w_gla"], lw["w_gate"], lw["w_gf"], lw["w_gb"], lw["b_gf"], lw["b_gb"],
      lw["w_att"], lw["w_m"], lw["q_gain"], lw["k_gain"], cos, sin, bd)


def _gla_boundary(beta, m, reverse):
    c, width = beta.shape
    two_m = 2 * m
    off = m if reverse else m - 1
    pieces = []
    if two_m >= V7X_SUBLANES:
        for v in range(c // V7X_SUBLANES):
            r = (V7X_SUBLANES * v // two_m) * two_m + off
            pieces.append(jnp.broadcast_to(beta[r:r + 1, :], (V7X_SUBLANES, width)))
    else:
        sub = lax.broadcasted_iota(jnp.int32, (V7X_SUBLANES, width), 0)
        per = V7X_SUBLANES // two_m
        for v in range(c // V7X_SUBLANES):
            rows = [jnp.broadcast_to(beta[V7X_SUBLANES * v + two_m * j + off:V7X_SUBLANES * v + two_m * j + off + 1, :],
                                     (V7X_SUBLANES, width)) for j in range(per)]
            acc = rows[-1]
            for j in range(per - 2, -1, -1):
                acc = jnp.where(sub < two_m * (j + 1), rows[j], acc)
            pieces.append(acc)
    return jnp.concatenate(pieces, axis=0)


def _gla_tree_scores(q, k, g, beta, mask_ref, sgn_ref, reverse):
    row = lax.broadcasted_iota(jnp.int32, g.shape, 0)
    nlev = len(GLA_LEVELS)
    mask_base = GLA_MASK_TREE_BWD if reverse else GLA_MASK_TREE_FWD
    sgn_base = (nlev - 1) if reverse else 0
    a = None
    if not reverse:
        a = mask_ref[GLA_MASK_DIAG] * lax.dot_general(q, k, _NT, preferred_element_type=F32)
    for li, m in enumerate(GLA_LEVELS):
        if m == 1:
            parity = (row & 1) == (0 if reverse else 1)
            f = jnp.exp(jnp.where(parity, g, 0.0))
        else:
            f = jnp.exp2((beta - _gla_boundary(beta, m, reverse)) * sgn_ref[sgn_base + li - 1])
        fb = f.astype(BF16)
        am = mask_ref[mask_base + li] * lax.dot_general(q * fb, k * fb, _NT, preferred_element_type=F32)
        a = am if a is None else a + am
    return a


def _gla_chunk_local(q, k, v, g, mask_ref, sgn_ref, reverse, bounded):
    c = GLA_CHUNK
    row = lax.broadcasted_iota(jnp.int32, g.shape, 0)
    beta = g
    for sh in GLA_LEVELS:
        if reverse:
            beta = beta + jnp.where(row < c - sh, pltpu.roll(beta, c - sh, 0), 0.0)
        else:
            beta = beta + jnp.where(row >= sh, pltpu.roll(beta, sh, 0), 0.0)
    qd = q * jnp.exp(beta).astype(BF16)
    tot = beta[0:1, :] if reverse else beta[c - 1:c, :]
    kd = k * jnp.exp(tot - beta).astype(BF16)
    decay_col = jnp.transpose(jnp.broadcast_to(jnp.exp(tot), (V7X_SUBLANES, GLA_DK)))[:, 0:1]
    if bounded:
        kg = k * jnp.exp(-beta).astype(BF16)
        a = (mask_ref[GLA_MASK_TRI_BWD if reverse else GLA_MASK_TRI_FWD]
             * lax.dot_general(qd, kg, _NT, preferred_element_type=F32))
    else:
        a = _gla_tree_scores(q, k, g, beta, mask_ref, sgn_ref, reverse)
    kv = lax.dot_general(kd, v, _TN, preferred_element_type=F32)
    return qd, a.astype(BF16), kv, decay_col


def _gla_chunk_state(local, v, s_ref):
    qd, a, kv, decay_col = local
    s = s_ref[...]
    o = jnp.dot(qd, s.astype(BF16), preferred_element_type=F32)
    s_ref[...] = decay_col * s + kv
    return o + jnp.dot(a, v, preferred_element_type=F32)


def _gla_body(q_ref, k_ref, v_ref, gf_ref, gb_ref, za_ref, gn_ref, mask_ref, sgn_ref, o_ref,
              acc_ref, sf_ref, sb_ref):
    t = q_ref.shape[0]
    c = GLA_CHUNK
    n = t // c
    sf_ref[...] = jnp.zeros_like(sf_ref)
    sb_ref[...] = jnp.zeros_like(sb_ref)

    chunk_tot = jnp.minimum(jnp.sum(gf_ref[...].reshape(n, c, GLA_DK), axis=1),
                            jnp.sum(gb_ref[...].reshape(n, c, GLA_DK), axis=1))
    bounded = jnp.min(chunk_tot) >= -GLA_BOUNDED_DECAY

    def scan(is_bounded):
        def step(i, carry, first_visit):
            work = []
            for u in range(GLA_CHUNKS_PER_STEP):
                j = i * GLA_CHUNKS_PER_STEP + u
                for reverse, g_ref, s_ref in ((False, gf_ref, sf_ref), (True, gb_ref, sb_ref)):
                    sl = pl.ds(pl.multiple_of(((n - 1 - j) if reverse else j) * c, c), c)
                    local = _gla_chunk_local(q_ref[sl, :], k_ref[sl, :], v_ref[sl, :], g_ref[sl, :],
                                             mask_ref, sgn_ref, reverse, is_bounded)
                    work.append((sl, local, s_ref))
            for sl, local, s_ref in work:
                o = _gla_chunk_state(local, v_ref[sl, :], s_ref)
                acc_ref[sl, :] = o if first_visit else acc_ref[sl, :] + o
            return carry

        steps = n // GLA_CHUNKS_PER_STEP
        lax.fori_loop(0, steps // 2, functools.partial(step, first_visit=True), 0)
        lax.fori_loop(steps // 2, steps, functools.partial(step, first_visit=False), 0)

    @pl.when(bounded)
    def _():
        scan(True)

    @pl.when(jnp.logical_not(bounded))
    def _():
        scan(False)

    rows = GLA_FINAL_ROWS

    def fin(i, carry):
        sl = pl.ds(pl.multiple_of(i * rows, rows), rows)
        o = acc_ref[sl, :]
        ms = jnp.mean(o * o, axis=-1, keepdims=True)
        y = o * lax.rsqrt(ms + EPS) * gn_ref[...]
        o_ref[sl, :] = (y * _silu(za_ref[sl, :].astype(F32))).astype(BF16)
        return carry

    lax.fori_loop(0, t // rows, fin, 0, unroll=4)


def _gla_masks():
    c = GLA_CHUNK
    ri = np.arange(c)[:, None]
    ci = np.arange(c)[None, :]
    masks = [ri == ci]
    for m in GLA_LEVELS:
        same = (ri // (2 * m)) == (ci // (2 * m))
        masks.append(same & ((ri % (2 * m)) >= m) & ((ci % (2 * m)) < m))
    for m in GLA_LEVELS:
        same = (ri // (2 * m)) == (ci // (2 * m))
        masks.append(same & ((ri % (2 * m)) < m) & ((ci % (2 * m)) >= m))
    masks.append(ri >= ci)
    masks.append(ri < ci)
    sgn = []
    row = np.arange(c)[:, None] * np.ones((1, GLA_DK), np.int64)
    for reverse in (False, True):
        for m in GLA_LEVELS[1:]:
            upper = (row % (2 * m)) >= m
            sgn.append(np.where(upper != reverse, np.log2(np.e), -np.log2(np.e)))
    return jnp.asarray(np.stack(masks).astype(np.float32)), jnp.asarray(np.stack(sgn).astype(np.float32))


def _gla(qa, ka, va, gf, gb, za, gn, layer, masks, sgn, batch, seq_len):
    n = qa.shape[0]
    t = seq_len
    assert t % (2 * GLA_CHUNK * GLA_CHUNKS_PER_STEP) == 0 and t % GLA_FINAL_ROWS == 0, t
    kspec = pl.BlockSpec((t, GLA_DK), lambda b, h: (b, h))
    vspec = pl.BlockSpec((t, GLA_DV), lambda b, h: (b, h))
    nm = masks.shape[0]
    block_bytes = (2 * _nbytes((t, GLA_DK), BF16) + 3 * _nbytes((t, GLA_DV), BF16)
                   + 2 * _nbytes((t, GLA_DK), F32) + _nbytes(masks.shape, F32) + _nbytes(sgn.shape, F32))
    scratch_bytes = _nbytes((t, GLA_DV), F32) + 2 * _nbytes((GLA_DK, GLA_DV), F32)
    return pl.pallas_call(
        _gla_body,
        grid=(batch, GLA_HEADS),
        in_specs=[kspec, kspec, vspec, kspec, kspec, vspec,
                  _layer_spec((1, GLA_DV), layer),
                  pl.BlockSpec((nm, GLA_CHUNK, GLA_CHUNK), lambda b, h: (0, 0, 0)),
                  pl.BlockSpec(sgn.shape, lambda b, h: (0, 0, 0))],
        out_specs=vspec,
        out_shape=jax.ShapeDtypeStruct((n, GLA_VALUE_DIM), BF16),
        scratch_shapes=[pltpu.VMEM((t, GLA_DV), F32),
                        pltpu.VMEM((GLA_DK, GLA_DV), F32),
                        pltpu.VMEM((GLA_DK, GLA_DV), F32)],
        compiler_params=pltpu.CompilerParams(
            dimension_semantics=("arbitrary", "arbitrary"),
            vmem_limit_bytes=_vmem_limit(block_bytes, scratch_bytes)),
        name="gla",
    )(qa, ka, va, gf, gb, za, gn, masks, sgn)


def _attn_body(qt_ref, k_ref, vt_ref, z_ref, o_ref, s_ref, acc_ref):
    t = k_ref.shape[1]
    tq = ATT_Q_ROWS
    nq = t // tq
    kc = ATT_KEY_CHUNK
    nk = t // kc
    cols = ATT_GROUP * tq

    def load_q(i):
        qt = qt_ref[:, pl.ds(pl.multiple_of(i * tq, tq), tq)]
        return jnp.concatenate([qt[ATT_HEAD_DIM * j:ATT_HEAD_DIM * (j + 1), :] for j in range(ATT_GROUP)], axis=1)

    def scores(q4t, c, m, buf):
        sl = pl.ds(pl.multiple_of(c * kc, kc), kc)
        s = jnp.dot(k_ref[0, sl, :], q4t, preferred_element_type=F32)
        s_ref[buf] = s
        return jnp.maximum(m, jnp.max(s, axis=0, keepdims=True))

    def weighted(c, m_old, m_new, buf):
        sl = pl.ds(pl.multiple_of(c * kc, kc), kc)
        p = jnp.exp2(s_ref[buf] - m_new).astype(BF16)
        acc_ref[...] = (jnp.exp2(m_old - m_new) * acc_ref[...]
                        + jnp.dot(vt_ref[:, sl], p, preferred_element_type=F32))

    m_init = jnp.full((1, cols), -jnp.inf, F32)

    def tile(i, m_first):
        q4t = load_q(i)
        acc_ref[...] = jnp.zeros_like(acc_ref)

        def pair(j, carry):
            m_a, m_b = carry
            c = 2 * j
            m_c = scores(q4t, c + 1, m_b, 1)
            weighted(c, m_a, m_b, 0)
            m_d = scores(q4t, c + 2, m_c, 0)
            weighted(c + 1, m_b, m_c, 1)
            return m_c, m_d

        m_a, m_b = lax.fori_loop(0, nk // 2 - 1, pair, (m_init, m_first), unroll=True)
        m_c = scores(q4t, nk - 1, m_b, 1)
        weighted(nk - 2, m_a, m_b, 0)
        m_next = scores(load_q(jnp.minimum(i + 1, nq - 1)), 0, m_init, 0)
        weighted(nk - 1, m_b, m_c, 1)
        acc = acc_ref[...]
        ot = acc[0:ATT_HEAD_DIM, :] / acc[ATT_HEAD_DIM:ATT_HEAD_DIM + 1, :]
        o = jnp.concatenate([jnp.transpose(ot[:, tq * j:tq * (j + 1)]) for j in range(ATT_GROUP)], axis=1)
        rows = pl.ds(pl.multiple_of(i * tq, tq), tq)
        o_ref[rows, :] = (o * _silu(z_ref[rows, :].astype(F32))).astype(BF16)
        return m_next

    lax.fori_loop(0, nq, tile, scores(load_q(0), 0, m_init, 0), unroll=2)


def _attn(qbt, kb, vbt, zb, batch, seq_len):
    n = zb.shape[0]
    t = seq_len
    tq = ATT_Q_ROWS
    width = ATT_GROUP * ATT_HEAD_DIM
    ospec = pl.BlockSpec((t, width), lambda b, g: (b, g))
    block_bytes = (3 * _nbytes((t, width), BF16) + _nbytes((t, V7X_LANES), BF16)
                   + _nbytes((2 * ATT_HEAD_DIM, t), BF16))
    scratch_bytes = (_nbytes((2, ATT_KEY_CHUNK, ATT_GROUP * tq), F32)
                     + _nbytes((2 * ATT_HEAD_DIM, ATT_GROUP * tq), F32))
    return pl.pallas_call(
        _attn_body,
        grid=(batch, ATT_KV_HEADS),
        in_specs=[pl.BlockSpec((width, t), lambda b, g: (g, b)),
                  pl.BlockSpec((1, t, ATT_HEAD_DIM), lambda b, g: (g, b, 0)),
                  pl.BlockSpec((2 * ATT_HEAD_DIM, t), lambda b, g: (g, b)),
                  ospec],
        out_specs=ospec,
        out_shape=jax.ShapeDtypeStruct((n, ATT_Q_DIM), BF16),
        scratch_shapes=[pltpu.VMEM((2, ATT_KEY_CHUNK, ATT_GROUP * tq), F32),
                        pltpu.VMEM((2 * ATT_HEAD_DIM, ATT_GROUP * tq), F32)],
        compiler_params=pltpu.CompilerParams(
            dimension_semantics=("arbitrary", "arbitrary"),
            vmem_limit_bytes=_vmem_limit(block_bytes, scratch_bytes)),
        name="attn",
    )(qbt, kb, vbt, zb)


def _outproj_body(oa_ref, ob_ref, ma_ref, mb_ref, x_ref, wa_ref, wb_ref, wo_ref, fg_ref, o_ref, *, final):
    ya = jnp.dot(oa_ref[...], wa_ref[...], preferred_element_type=F32)
    yb = jnp.dot(ob_ref[...], wb_ref[...], preferred_element_type=F32)
    merged = (_sigmoid(ma_ref[...].astype(F32)) * ya
              + _sigmoid(mb_ref[...].astype(F32)) * yb)
    y = x_ref[...] + jnp.dot(merged.astype(BF16), wo_ref[...], preferred_element_type=F32)
    if final:
        ms = jnp.mean(y * y, axis=-1, keepdims=True)
        y = y * lax.rsqrt(ms + EPS) * fg_ref[...]
    o_ref[...] = y


def _outproj(oa, ob, ma, mb, x, lw, layer, final_gain, final):
    n = x.shape[0]
    tm = OUTPROJ_ROWS
    row = lambda i: (i, 0)
    const = lambda i: (0, 0)
    act = pl.BlockSpec((tm, D_MODEL), row)
    wspec = _layer_spec((D_MODEL, D_MODEL), layer)
    block_bytes = 4 * _nbytes((tm, D_MODEL), BF16) + 2 * _nbytes((tm, D_MODEL), F32)
    weight_bytes = 3 * _nbytes((D_MODEL, D_MODEL), BF16)
    return pl.pallas_call(
        functools.partial(_outproj_body, final=final),
        grid=(n // tm,),
        in_specs=[act, act, act, act, act, wspec, wspec, wspec,
                  pl.BlockSpec((1, D_MODEL), const, pipeline_mode=pl.Buffered(1))],
        out_specs=act,
        out_shape=jax.ShapeDtypeStruct((n, D_MODEL), F32),
        compiler_params=pltpu.CompilerParams(
            dimension_semantics=("arbitrary",),
            vmem_limit_bytes=_vmem_limit(block_bytes, weight_bytes)),
        name="outproj",
    )(oa, ob, ma, mb, x, lw["w_a"], lw["w_b"], lw["w_o"], final_gain)


def _rope_tables(seq_len):
    rows = seq_len // GRID_W
    r = jnp.repeat(jnp.arange(rows, dtype=F32), GRID_W)
    c = jnp.tile(jnp.arange(GRID_W, dtype=F32), rows)
    nf = ROPE_AXIS_DIM // 2
    inv = ROPE_THETA ** (-jnp.arange(nf, dtype=F32) / nf)
    ang_r = r[:, None] * inv
    ang_c = c[:, None] * inv
    cos = jnp.concatenate([jnp.cos(ang_r), jnp.cos(ang_r), jnp.cos(ang_c), jnp.cos(ang_c)], axis=-1)
    sin = jnp.concatenate([-jnp.sin(ang_r), jnp.sin(ang_r), -jnp.sin(ang_c), jnp.sin(ang_c)], axis=-1)
    reps = V7X_LANES // ATT_HEAD_DIM
    return jnp.tile(cos, (1, reps)), jnp.tile(sin, (1, reps))


def _stacked_weights(norm_g, w_in, w_gate_f, b_gate_f, w_gate_b, b_gate_b, gla_norm_g, q_norm_g, k_norm_g,
                     w_branch_a, w_branch_b, w_out):
    depth = w_in.shape[0]
    pts = np.cumsum((0,) + IN_SIZES)
    wb = w_in.astype(BF16)
    col = lambda i: wb[:, :, pts[i]:pts[i + 1]]
    gate_pad = V7X_LANES - 2 * GATE_RANK
    return {
        "norm_g": norm_g.reshape(depth, 1, D_MODEL),
        "w_gla": jnp.concatenate([col(0), col(1), col(2), col(5)], axis=2),
        "w_gate": jnp.pad(wb[:, :, pts[3]:pts[5]], ((0, 0), (0, 0), (0, gate_pad))),
        "w_gf": jnp.pad(w_gate_f.astype(BF16), ((0, 0), (0, V7X_LANES - GATE_RANK), (0, 0))),
        "w_gb": jnp.pad(w_gate_b.astype(BF16), ((0, 0), (GATE_RANK, gate_pad), (0, 0))),
        "b_gf": b_gate_f.reshape(depth, 1, GLA_KEY_DIM),
        "b_gb": b_gate_b.reshape(depth, 1, GLA_KEY_DIM),
        "w_att": jnp.concatenate([col(6), col(7), col(8), col(9)], axis=2),
        "w_m": jnp.concatenate([col(10), col(11)], axis=2),
        "q_gain": (jnp.tile(q_norm_g, (1, ATT_Q_HEADS)) * ATT_Q_SCALE).reshape(depth, 1, ATT_Q_DIM),
        "k_gain": jnp.tile(k_norm_g, (1, ATT_KV_HEADS)).reshape(depth, 1, ATT_KV_DIM),
        "gla_gain": gla_norm_g.reshape(depth, 1, GLA_DV),
        "w_a": w_branch_a.astype(BF16),
        "w_b": w_branch_b.astype(BF16),
        "w_o": w_out.astype(BF16),
    }


def _trunk(x3, lw, final_gain, masks, sgn, bd):
    batch, seq_len, _ = x3.shape
    x = x3.reshape(batch * seq_len, D_MODEL)
    cos, sin = _rope_tables(seq_len)
    for layer in range(DEPTH):
        qa, ka, va, za, gf, gb, qbt, kb, vbt, zb, ma, mb = _inproj(x, lw, layer, cos, sin, bd, seq_len)
        oa = _gla(qa, ka, va, gf, gb, za, lw["gla_gain"], layer, masks, sgn, batch, seq_len)
        ob = _attn(qbt, kb, vbt, zb, batch, seq_len)
        x = _outproj(oa, ob, ma, mb, x, lw, layer, final_gain, final=(layer == DEPTH - 1))
    return x.reshape(batch, seq_len, D_MODEL)


def kernel(x_prompt, x_sample, norm_g, w_in, w_gate_f, b_gate_f, w_gate_b, b_gate_b, gla_norm_g, q_norm_g,
           k_norm_g, w_branch_a, w_branch_b, w_out, final_norm_g):
    lw = _stacked_weights(norm_g, w_in, w_gate_f, b_gate_f, w_gate_b, b_gate_b, gla_norm_g, q_norm_g,
                          k_norm_g, w_branch_a, w_branch_b, w_out)
    final_gain = final_norm_g.reshape(1, D_MODEL)
    masks, sgn = _gla_masks()
    head = np.arange(V7X_MXU_DIM) // ATT_HEAD_DIM
    bd = jnp.asarray((head[:, None] == head[None, :]).astype(np.float32) / ATT_HEAD_DIM, dtype=BF16)
    y_prompt = _trunk(x_prompt, lw, final_gain, masks, sgn, bd)
    y_sample = _trunk(x_sample, lw, final_gain, masks, sgn, bd)
    return (y_prompt, y_sample)
```
